```python
import jax, jax.numpy as jnp
from jax import lax
import numpy as np

D_MODEL = 1024
BATCH = 4
SEQ = 4096
DEPTH = 2

N_MIXERS = 2
N_A = (DEPTH + 1) // 2
N_B = DEPTH // 2
EPS = 1e-6

MLA_HEADS = 16
MLA_Q_LORA = 384
MLA_KV_LORA = 256
MLA_NOPE = 64
MLA_ROPE = 32
MLA_V = 64
MLA_IN = MLA_Q_LORA + MLA_KV_LORA + MLA_ROPE
ROPE_THETA = 10000.0
Q_BLOCK = 128

HG_HEADS = 8
HG_DK = 128
HG_DV = D_MODEL // HG_HEADS
HG_WIDTH = HG_HEADS * HG_DK
HG_VWIDTH = HG_HEADS * HG_DV
HG_IN = 2 * HG_WIDTH + 2 * HG_VWIDTH
HG_CHUNK = 64

D_FF = 3584
N_EXPERTS = 8
TOP_K = 2
D_FF_EXPERT = 3584

kernel_name = 'hybrid_mla_hgrn2_moe'


def rms_norm(x, g):
    xf = x.astype(jnp.float32)
    y = xf * lax.rsqrt(jnp.mean(xf * xf, axis=-1, keepdims=True) + EPS)
    return (y * g.astype(jnp.float32)).astype(x.dtype)


def rope_tables(positions):
    inv_freq = ROPE_THETA ** (-jnp.arange(0, MLA_ROPE, 2, dtype=jnp.float32) / MLA_ROPE)
    ang = positions.astype(jnp.float32)[..., None] * inv_freq
    return jnp.cos(ang), jnp.sin(ang)


def apply_rope(x, cos, sin):
    xf = x.astype(jnp.float32)
    half = MLA_ROPE // 2
    x1, x2 = xf[..., :half], xf[..., half:]
    return jnp.concatenate([x1 * cos - x2 * sin, x2 * cos + x1 * sin], axis=-1).astype(x.dtype)


def causal_latent_attention(qn, qr, kn, kr, v):
    B, S, H, _ = qn.shape
    nb = S // Q_BLOCK
    scale = (MLA_NOPE + MLA_ROPE) ** -0.5
    qn_b = qn.reshape(B, nb, Q_BLOCK, H, MLA_NOPE).transpose(1, 0, 2, 3, 4)
    qr_b = qr.reshape(B, nb, Q_BLOCK, H, MLA_ROPE).transpose(1, 0, 2, 3, 4)
    key_idx = jnp.arange(S)

    def block(args):
        qn_i, qr_i, blk = args
        s = (jnp.einsum('bqhd,bkhd->bhqk', qn_i, kn)
             + jnp.einsum('bqhr,bkr->bhqk', qr_i, kr)).astype(jnp.float32) * scale
        q_idx = blk * Q_BLOCK + jnp.arange(Q_BLOCK)
        mask = key_idx[None, :] <= q_idx[:, None]
        s = jnp.where(mask, s, -jnp.inf)
        p = jax.nn.softmax(s, axis=-1).astype(v.dtype)
        return jnp.einsum('bhqk,bkhd->bqhd', p, v)

    out = lax.map(block, (qn_b, qr_b, jnp.arange(nb)))
    return out.transpose(1, 0, 2, 3, 4).reshape(B, S, H * MLA_V)


def mla_mixer(hn, cos, sin, w_in, q_norm_g, w_q_up, kv_norm_g, w_kv_up, w_o):
    B, S, _ = hn.shape
    proj = hn @ w_in
    c_q = proj[..., :MLA_Q_LORA]
    c_kv = proj[..., MLA_Q_LORA:MLA_Q_LORA + MLA_KV_LORA]
    k_r = proj[..., MLA_Q_LORA + MLA_KV_LORA:]
    q = (rms_norm(c_q, q_norm_g) @ w_q_up).reshape(B, S, MLA_HEADS, MLA_NOPE + MLA_ROPE)
    qn, qr = q[..., :MLA_NOPE], q[..., MLA_NOPE:]
    qr = apply_rope(qr, cos[:, :, None, :], sin[:, :, None, :])
    kr = apply_rope(k_r, cos, sin)
    kv = (rms_norm(c_kv, kv_norm_g) @ w_kv_up).reshape(B, S, MLA_HEADS, MLA_NOPE + MLA_V)
    kn, v = kv[..., :MLA_NOPE], kv[..., MLA_NOPE:]
    o = causal_latent_attention(qn, qr, kn, kr, v)
    return o @ w_o


def hgrn2_chunked(q, k, v, log_f):
    B, S, H, dk = q.shape
    dv = v.shape[-1]
    C = HG_CHUNK
    N = S // C
    to_chunks = lambda a: a.reshape(B, N, C, H, a.shape[-1]).transpose(0, 3, 1, 2, 4)
    q, k, v, log_f = to_chunks(q), to_chunks(k), to_chunks(v), to_chunks(log_f)
    b = jnp.cumsum(log_f, axis=3)
    b_last = b[:, :, :, -1:, :]
    qd = q * jnp.exp(b)
    kd = k * jnp.exp(-b)
    A = jnp.einsum('bhnck,bhnsk->bhncs', qd, kd)
    tril = jnp.tril(jnp.ones((C, C), dtype=bool))
    A = jnp.where(tril, A, 0.0)
    o_intra = jnp.einsum('bhncs,bhnsv->bhncv', A, v)
    kv = jnp.einsum('bhnsk,bhnsv->bhnkv', k * jnp.exp(b_last - b), v)
    decay = jnp.exp(b_last[:, :, :, 0, :])

    def step(state, inp):
        d, kv_n = inp
        return d[..., None] * state + kv_n, state

    _, s_prev = lax.scan(step, jnp.zeros((B, H, dk, dv), jnp.float32),
                         (jnp.moveaxis(decay, 2, 0), jnp.moveaxis(kv, 2, 0)))
    s_prev = jnp.moveaxis(s_prev, 0, 2)
    o = o_intra + jnp.einsum('bhnck,bhnkv->bhncv', qd, s_prev)
    return o.transpose(0, 2, 3, 1, 4).reshape(B, S, H, dv)


def hgrn2_mixer(hn, lb, w_in, out_norm_g, w_o):
    B, S, _ = hn.shape
    proj = hn @ w_in
    q = proj[..., :HG_WIDTH]
    f_pre = proj[..., HG_WIDTH:2 * HG_WIDTH]
    i_in = proj[..., 2 * HG_WIDTH:2 * HG_WIDTH + HG_VWIDTH]
    g = proj[..., 2 * HG_WIDTH + HG_VWIDTH:]
    heads_k = lambda a: a.astype(jnp.float32).reshape(B, S, HG_HEADS, HG_DK)
    heads_v = lambda a: a.astype(jnp.float32).reshape(B, S, HG_HEADS, HG_DV)
    lb = lb.astype(jnp.float32).reshape(HG_HEADS, HG_DK)
    f = lb + (1.0 - lb) * jax.nn.sigmoid(heads_k(f_pre))
    log_f = jnp.log(f)
    k = 1.0 - f
    o = hgrn2_chunked(heads_k(q) * HG_DK ** -0.5, k, heads_v(i_in), log_f)
    o = rms_norm(o, out_norm_g) * jax.nn.silu(heads_v(g))
    return o.reshape(B, S, HG_VWIDTH).astype(hn.dtype) @ w_o


def swiglu(t, w_gate, w_up, w_down):
    return (jax.nn.silu(t @ w_gate) * (t @ w_up)) @ w_down


def moe_ffn(hn, router, w_gate, w_up, w_down):
    B, S, D = hn.shape
    t = hn.reshape(B * S, D)
    logits = (t @ router).astype(jnp.float32)
    vals, idx = lax.top_k(logits, TOP_K)
    w = jax.nn.softmax(vals, axis=-1)
    gate = jnp.sum(jax.nn.one_hot(idx, N_EXPERTS, dtype=jnp.float32) * w[..., None], axis=1)
    gate = gate.astype(t.dtype)
    y = jnp.zeros_like(t)
    for e in range(N_EXPERTS):
        y = y + gate[:, e:e + 1] * swiglu(t, w_gate[e], w_up[e], w_down[e])
    return y.reshape(B, S, D)


def _w(key, shape, fan_in):
    return jax.random.normal(key, shape, jnp.float32) * (fan_in ** -0.5)


def _gain(key, shape):
    return 1.0 + 0.02 * jax.random.normal(key, shape, jnp.float32)


def setup_inputs(seed: int = 0) -> dict:
    key = jax.random.key(seed)
    ks = jax.random.split(key, 26)
    x = jax.random.normal(ks[0], (BATCH, SEQ, D_MODEL), jnp.float32)
    offset = jax.random.randint(ks[1], (BATCH, 1), 0, 1024, dtype=jnp.int32)
    positions = jnp.arange(SEQ, dtype=jnp.int32)[None, :] + offset
    return {
        'x': x,
        'positions': positions,
        'mix_norm': _gain(ks[2], (DEPTH, D_MODEL)),
        'ffn_norm': _gain(ks[3], (DEPTH, D_MODEL)),
        'final_norm': _gain(ks[4], (D_MODEL,)),
        'mla_w_in': _w(ks[5], (N_A, D_MODEL, MLA_IN), D_MODEL),
        'mla_q_norm': _gain(ks[6], (N_A, MLA_Q_LORA)),
        'mla_w_q_up': _w(ks[7], (N_A, MLA_Q_LORA, MLA_HEADS * (MLA_NOPE + MLA_ROPE)), MLA_Q_LORA),
        'mla_kv_norm': _gain(ks[8], (N_A, MLA_KV_LORA)),
        'mla_w_kv_up': _w(ks[9], (N_A, MLA_KV_LORA, MLA_HEADS * (MLA_NOPE + MLA_V)), MLA_KV_LORA),
        'mla_w_o': _w(ks[10], (N_A, MLA_HEADS * MLA_V, D_MODEL), MLA_HEADS * MLA_V),
        'hgrn_w_in': _w(ks[11], (N_B, D_MODEL, HG_IN), D_MODEL),
        'hgrn_lower_bounds': 0.5 * jax.random.normal(ks[12], (DEPTH, HG_WIDTH), jnp.float32),
        'hgrn_out_norm': _gain(ks[13], (N_B, HG_DV)),
        'hgrn_w_o': _w(ks[14], (N_B, HG_VWIDTH, D_MODEL), HG_VWIDTH),
        'ffn_w_gate': _w(ks[15], (N_A, D_MODEL, D_FF), D_MODEL),
        'ffn_w_up': _w(ks[16], (N_A, D_MODEL, D_FF), D_MODEL),
        'ffn_w_down': _w(ks[17], (N_A, D_FF, D_MODEL), D_FF),
        'moe_router': _w(ks[18], (N_B, D_MODEL, N_EXPERTS), D_MODEL),
        'moe_w_gate': _w(ks[19], (N_B, N_EXPERTS, D_MODEL, D_FF_EXPERT), D_MODEL),
        'moe_w_up': _w(ks[20], (N_B, N_EXPERTS, D_MODEL, D_FF_EXPERT), D_MODEL),
        'moe_w_down': _w(ks[21], (N_B, N_EXPERTS, D_FF_EXPERT, D_MODEL), D_FF_EXPERT),
    }


def reference(x, positions, mix_norm, ffn_norm, final_norm,
              mla_w_in, mla_q_norm, mla_w_q_up, mla_kv_norm, mla_w_kv_up, mla_w_o,
              hgrn_w_in, hgrn_lower_bounds, hgrn_out_norm, hgrn_w_o,
              ffn_w_gate, ffn_w_up, ffn_w_down,
              moe_router, moe_w_gate, moe_w_up, moe_w_down):
    cos, sin = rope_tables(positions)
    lb_all = jnp.cumsum(jax.nn.softmax(hgrn_lower_bounds.astype(jnp.float32), axis=0), axis=0)
    lb_all = lb_all - lb_all[0:1]
    h = x
    for i in range(DEPTH):
        j = i // N_MIXERS
        hn = rms_norm(h, mix_norm[i])
        if i % N_MIXERS == 0:
            h = h + mla_mixer(hn, cos, sin, mla_w_in[j], mla_q_norm[j], mla_w_q_up[j],
                              mla_kv_norm[j], mla_w_kv_up[j], mla_w_o[j])
        else:
            h = h + hgrn2_mixer(hn, lb_all[i], hgrn_w_in[j], hgrn_out_norm[j], hgrn_w_o[j])
        hn = rms_norm(h, ffn_norm[i])
        if i % 2 == 0:
            h = h + swiglu(hn, ffn_w_gate[j], ffn_w_up[j], ffn_w_down[j])
        else:
            h = h + moe_ffn(hn, moe_router[j], moe_w_gate[j], moe_w_up[j], moe_w_down[j])
    return rms_norm(h, final_norm)
```

```python
import functools

import jax
import jax.numpy as jnp
from jax import lax
from jax.experimental import pallas as pl
from jax.experimental.pallas import tpu as pltpu

EPS = 1e-6
LANES = 128

MLA_HEADS = 16
MLA_Q_LORA = 384
MLA_KV_LORA = 256
MLA_NOPE = 64
MLA_ROPE = 32
MLA_V = 64
ROPE_THETA = 10000.0

HG_HEADS = 8
HG_DK = 128
HG_CHUNK = 64

N_EXPERTS = 8

VMEM_LIMIT = 56 * 1024 * 1024

BF16 = jnp.bfloat16
F32 = jnp.float32


def _dot(a, b):
    return jnp.dot(a, b, preferred_element_type=F32)


def _dot_nt(a, b):
    return lax.dot_general(a, b, (((1,), (1,)), ((), ())), preferred_element_type=F32)


def _dot_tn(a, b):
    return lax.dot_general(a, b, (((0,), (0,)), ((), ())), preferred_element_type=F32)


def _rms(x, g):
    return x * lax.rsqrt(jnp.mean(x * x, axis=-1, keepdims=True) + EPS) * g


def _silu(x):
    return x * (1.0 / (1.0 + jnp.exp(-x)))


def _split3(x):
    hi = x.astype(BF16)
    r = x - hi.astype(F32)
    mid = r.astype(BF16)
    lo = (r - mid.astype(F32)).astype(BF16)
    return hi, mid, lo


def _mla_proj_kernel(x_ref, pos_ref, g_ref, win_ref, qg_ref, wq_ref, kvg_ref, wk_ref,
                     wv_ref, freq_ref, q_out, k_out, v_out, *, scale):
    hn = _rms(x_ref[...], g_ref[...]).astype(BF16)
    proj = _dot(hn, win_ref[...])
    cq = proj[:, :MLA_Q_LORA]
    ckv = proj[:, MLA_Q_LORA:MLA_Q_LORA + MLA_KV_LORA]
    kr = proj[:, MLA_Q_LORA + MLA_KV_LORA:]
    cqn = _rms(cq, qg_ref[...]).astype(BF16)
    ckvn = _rms(ckv, kvg_ref[...]).astype(BF16)

    ang = pos_ref[...] * freq_ref[...]
    cos = jnp.cos(ang)
    sin = jnp.sin(ang)
    lane = lax.broadcasted_iota(jnp.int32, ang.shape, 1)
    half = MLA_ROPE // 2
    c_tab = jnp.where(lane < MLA_NOPE + MLA_ROPE, cos, 0.0)
    s_from_x1 = jnp.where(lane >= MLA_NOPE + half, sin, 0.0)
    s_from_x2 = jnp.where(lane < MLA_NOPE + half, -sin, 0.0)

    def rope(t, mul):
        return (t * (c_tab * mul)
                + pltpu.roll(t, half, 1) * (s_from_x1 * mul)
                + pltpu.roll(t, LANES - half, 1) * (s_from_x2 * mul))

    q_pad = _dot(cqn, wq_ref[...])
    k_pad = _dot(ckvn, wk_ref[...])
    kr_roped = rope(kr, 1.0)
    for h in range(MLA_HEADS):
        sl = slice(h * LANES, (h + 1) * LANES)
        q_out[h] = rope(q_pad[:, sl], scale).astype(BF16)
        k_out[h] = (k_pad[:, sl] + kr_roped).astype(BF16)
    v_out[...] = _dot(ckvn, wv_ref[...]).astype(BF16)


def _mla_proj(x2d, pos, g, w_in_p, qg, wq_p, kvg, wk_p, wv, freq, *, batch, seq, tm):
    t = x2d.shape[0]
    d = x2d.shape[1]
    nt = seq // tm
    hw = MLA_HEADS * LANES
    const = lambda shape: pl.BlockSpec(shape, lambda i: (0,) * len(shape))
    head_spec = pl.BlockSpec((None, MLA_HEADS, tm, LANES), lambda i: (i // nt, 0, i % nt, 0))
    scale = (MLA_NOPE + MLA_ROPE) ** -0.5
    return pl.pallas_call(
        functools.partial(_mla_proj_kernel, scale=scale),
        grid=(t // tm,),
        in_specs=[
            pl.BlockSpec((tm, d), lambda i: (i, 0)),
            pl.BlockSpec((tm, 1), lambda i: (i, 0)),
            const((1, d)),
            const(w_in_p.shape),
            const((1, MLA_Q_LORA)),
            const(wq_p.shape),
            const((1, MLA_KV_LORA)),
            const(wk_p.shape),
            const(wv.shape),
            const((1, LANES)),
        ],
        out_specs=[head_spec, head_spec, pl.BlockSpec((tm, MLA_HEADS * MLA_V), lambda i: (i, 0))],
        out_shape=[
            jax.ShapeDtypeStruct((batch, MLA_HEADS, seq, LANES), BF16),
            jax.ShapeDtypeStruct((batch, MLA_HEADS, seq, LANES), BF16),
            jax.ShapeDtypeStruct((t, MLA_HEADS * MLA_V), BF16),
        ],
        compiler_params=pltpu.CompilerParams(
            dimension_semantics=("parallel",), vmem_limit_bytes=VMEM_LIMIT),
        name="mla_proj",
    )(x2d, pos, g, w_in_p, qg, wq_p, kvg, wk_p, wv, freq)


def _attn_kernel(q_ref, k_ref, v_ref, o_ref, *, tq):
    seq = q_ref.shape[1]
    nq = seq // tq
    row = lax.broadcasted_iota(jnp.int32, (tq, tq), 0)
    col = lax.broadcasted_iota(jnp.int32, (tq, tq), 1)
    causal = col <= row
    lane = lax.broadcasted_iota(jnp.int32, (tq, LANES), 1)

    def step(q, h, k0, carry, masked):
        m, l, acc = carry
        k = k_ref[h, pl.ds(k0, tq), :]
        s = _dot_nt(q, k)
        if masked:
            s = jnp.where(causal, s, -jnp.inf)
        m_new = jnp.maximum(m, jnp.max(s, axis=-1, keepdims=True))
        p = jnp.exp(s - m_new)
        alpha = jnp.exp(m - m_new)
        l = alpha * l + jnp.sum(p, axis=-1, keepdims=True)
        acc = alpha * acc + _dot(p.astype(BF16), v_ref[pl.ds(k0, tq), :])
        return m_new, l, acc

    def q_block(qi, _):
        q0 = pl.multiple_of(qi * tq, tq)
        qs = [q_ref[h, pl.ds(q0, tq), :] for h in range(2)]
        init = tuple((jnp.full((tq, 1), -jnp.inf, F32), jnp.zeros((tq, 1), F32),
                      jnp.zeros((tq, LANES), F32)) for _ in range(2))

        def k_block(ki, carry):
            k0 = pl.multiple_of(ki * tq, tq)
            return tuple(step(qs[h], h, k0, carry[h], False) for h in range(2))

        carry = lax.fori_loop(0, qi, k_block, init)
        outs = []
        for h in range(2):
            m, l, acc = step(qs[h], h, q0, carry[h], True)
            outs.append(acc / l)
        o_ref[pl.ds(q0, tq), :] = jnp.where(lane < MLA_V, outs[0], outs[1]).astype(BF16)
        return 0

    lax.fori_loop(0, nq, q_block, 0)


def _mla_attn(q, k, v, *, tq):
    batch, heads, seq, _ = q.shape
    v3 = v.reshape(batch, seq, heads * MLA_V)
    qk_spec = pl.BlockSpec((None, 2, seq, LANES), lambda b, hp: (b, hp, 0, 0))
    v_spec = pl.BlockSpec((None, seq, LANES), lambda b, hp: (b, 0, hp))
    return pl.pallas_call(
        functools.partial(_attn_kernel, tq=tq),
        grid=(batch, heads // 2),
        in_specs=[qk_spec, qk_spec, v_spec],
        out_specs=v_spec,
        out_shape=jax.ShapeDtypeStruct((batch, seq, heads * MLA_V), BF16),
        compiler_params=pltpu.CompilerParams(
            dimension_semantics=("parallel", "parallel"), vmem_limit_bytes=VMEM_LIMIT),
        name="mla_attn",
    )(q, k, v3)


def _attn_ffn_kernel(x_ref, o_ref, wo_ref, g_ref, wg_ref, wu_ref, wd_ref, out_ref, hn_ref):
    j = pl.program_id(1)

    @pl.when(j == 0)
    def _():
        h1 = x_ref[...] + _dot(o_ref[...], wo_ref[...])
        out_ref[...] = h1
        hn_ref[...] = _rms(h1, g_ref[...]).astype(BF16)

    hn = hn_ref[...]
    a = _silu(_dot(hn, wg_ref[...])) * _dot(hn, wu_ref[...])
    out_ref[...] += _dot(a.astype(BF16), wd_ref[...])


def _attn_ffn(x2d, o, w_o, g, w_gate, w_up, w_down, *, tm, tf):
    t, d = x2d.shape
    f = w_gate.shape[1]
    return pl.pallas_call(
        _attn_ffn_kernel,
        grid=(t // tm, f // tf),
        in_specs=[
            pl.BlockSpec((tm, d), lambda i, j: (i, 0)),
            pl.BlockSpec((tm, o.shape[1]), lambda i, j: (i, 0)),
            pl.BlockSpec(w_o.shape, lambda i, j: (0, 0)),
            pl.BlockSpec((1, d), lambda i, j: (0, 0)),
            pl.BlockSpec((d, tf), lambda i, j: (0, j)),
            pl.BlockSpec((d, tf), lambda i, j: (0, j)),
            pl.BlockSpec((tf, d), lambda i, j: (j, 0)),
        ],
        out_specs=pl.BlockSpec((tm, d), lambda i, j: (i, 0)),
        out_shape=jax.ShapeDtypeStruct((t, d), F32),
        scratch_shapes=[pltpu.VMEM((tm, d), BF16)],
        compiler_params=pltpu.CompilerParams(
            dimension_semantics=("parallel", "arbitrary"), vmem_limit_bytes=VMEM_LIMIT),
        name="attn_ffn",
    )(x2d, o, w_o, g, w_gate, w_up, w_down)


def _hgrn_kernel(h_ref, g_ref, win_ref, lbraw_ref, og_ref, wo_ref, fg_ref, router_ref,
                 h_out, hn_out, gate_out, st_ref, o_scr, *, layer):
    tt = h_ref.shape[0]
    width = HG_HEADS * HG_DK
    nchunk = tt // HG_CHUNK

    @pl.when(pl.program_id(1) == 0)
    def _():
        st_ref[...] = jnp.zeros_like(st_ref)

    lbraw = lbraw_ref[...]
    e = jnp.exp(lbraw - jnp.max(lbraw, axis=0, keepdims=True))
    sm = e / jnp.sum(e, axis=0, keepdims=True)
    lb = jnp.sum(sm[1:layer + 1], axis=0, keepdims=True)

    h_in = h_ref[...]
    hn = _rms(h_in, g_ref[...]).astype(BF16)
    proj = _dot(hn, win_ref[...])
    q_all = proj[:, :width] * (HG_DK ** -0.5)
    f_all = lb + (1.0 - lb) * (1.0 / (1.0 + jnp.exp(-proj[:, width:2 * width])))
    logf = jnp.log(f_all)
    k_all = 1.0 - f_all

    r = lax.broadcasted_iota(jnp.int32, (tt, tt), 0)
    c = lax.broadcasted_iota(jnp.int32, (tt, tt), 1)
    shift = HG_CHUNK.bit_length() - 1
    tril = (c <= r) & ((r >> shift) == (c >> shift))
    tril_b = jnp.where(tril, 1.0, 0.0).astype(BF16)
    hi, mid, lo = _split3(logf)
    b_all = _dot(tril_b, hi) + _dot(tril_b, mid) + _dot(tril_b, lo)

    for h in range(HG_HEADS):
        sl = slice(h * HG_DK, (h + 1) * HG_DK)
        q = q_all[:, sl]
        k = k_all[:, sl]
        b = b_all[:, sl]
        v = proj[:, 2 * width + h * HG_DK:2 * width + (h + 1) * HG_DK]
        gt = proj[:, 3 * width + h * HG_DK:3 * width + (h + 1) * HG_DK]
        v_b = v.astype(BF16)
        qd = (q * jnp.exp(b)).astype(BF16)
        kd = (k * jnp.exp(-b)).astype(BF16)
        a = jnp.where(tril, _dot_nt(qd, kd), 0.0)
        o = _dot(a.astype(BF16), v_b)

        st = st_ref[h]
        inter = []
        for n in range(nchunk):
            rows = slice(n * HG_CHUNK, (n + 1) * HG_CHUNK)
            b_n = b[rows]
            b_last = b_n[HG_CHUNK - 1:HG_CHUNK]
            inter.append(_dot_nt(qd[rows], st.astype(BF16)))
            kdl = (k[rows] * jnp.exp(b_last - b_n)).astype(BF16)
            st = st * jnp.exp(b_last) + _dot_tn(v_b[rows], kdl)
        st_ref[h] = st
        o = o + jnp.concatenate(inter, axis=0)
        o = _rms(o, og_ref[...]) * _silu(gt)
        o_scr[:, sl] = o.astype(BF16)

    h_new = h_in + _dot(o_scr[...], wo_ref[...])
    h_out[...] = h_new
    hn2 = _rms(h_new, fg_ref[...])
    hn_out[...] = hn2.astype(BF16)

    x_hi, x_mid, _ = _split3(hn2)
    router = router_ref[...]
    r_hi = router.astype(BF16)
    r_lo = (router - r_hi.astype(F32)).astype(BF16)
    logits = _dot(x_hi, r_hi) + _dot(x_mid, r_hi) + _dot(x_hi, r_lo)
    lane = lax.broadcasted_iota(jnp.int32, logits.shape, 1)
    neg = -jnp.inf
    logits = jnp.where(lane < N_EXPERTS, logits, neg)
    m1 = jnp.max(logits, axis=-1, keepdims=True)
    i1 = jnp.min(jnp.where(logits == m1, lane, LANES), axis=-1, keepdims=True)
    rest = jnp.where(lane == i1, neg, logits)
    m2 = jnp.max(rest, axis=-1, keepdims=True)
    i2 = jnp.min(jnp.where(rest == m2, lane, LANES), axis=-1, keepdims=True)
    e2 = jnp.exp(m2 - m1)
    w1 = 1.0 / (1.0 + e2)
    w2 = e2 / (1.0 + e2)
    gate_out[...] = jnp.where(lane == i1, w1, 0.0) + jnp.where(lane == i2, w2, 0.0)


def _hgrn(h2d, g, w_in, lb_raw, og, w_o, fg, router_p, *, batch, seq, tt, layer):
    t, d = h2d.shape
    nt = seq // tt
    width = HG_HEADS * HG_DK
    tok = lambda cols: pl.BlockSpec((tt, cols), lambda b, i: (b * nt + i, 0))
    const = lambda shape: pl.BlockSpec(shape, lambda b, i: (0,) * len(shape))
    return pl.pallas_call(
        functools.partial(_hgrn_kernel, layer=layer),
        grid=(batch, nt),
        in_specs=[tok(d), const((1, d)), const(w_in.shape), const(lb_raw.shape),
                  const((1, HG_DK)), const(w_o.shape), const((1, d)), const(router_p.shape)],
        out_specs=[tok(d), tok(d), tok(LANES)],
        out_shape=[jax.ShapeDtypeStruct((t, d), F32),
                   jax.ShapeDtypeStruct((t, d), BF16),
                   jax.ShapeDtypeStruct((t, LANES), F32)],
        scratch_shapes=[pltpu.VMEM((HG_HEADS, HG_DK, HG_DK), F32),
                        pltpu.VMEM((tt, width), BF16)],
        compiler_params=pltpu.CompilerParams(
            dimension_semantics=("parallel", "arbitrary"), vmem_limit_bytes=VMEM_LIMIT),
        name="hgrn",
    )(h2d, g, w_in, lb_raw, og, w_o, fg, router_p)


def _moe_kernel(hn_ref, gate_ref, h_ref, wg_ref, wu_ref, wd_ref, fg_ref, out_ref, acc_ref):
    e = pl.program_id(1)
    j = pl.program_id(2)

    @pl.when((e == 0) & (j == 0))
    def _():
        acc_ref[...] = h_ref[...]

    gate = gate_ref[...]
    lane = lax.broadcasted_iota(jnp.int32, gate.shape, 1)
    w = jnp.sum(jnp.where(lane == e, gate, 0.0), axis=-1, keepdims=True)
    hn = hn_ref[...]
    a = _silu(_dot(hn, wg_ref[...])) * _dot(hn, wu_ref[...]) * w
    acc_ref[...] += _dot(a.astype(BF16), wd_ref[...])

    @pl.when((e == pl.num_programs(1) - 1) & (j == pl.num_programs(2) - 1))
    def _():
        out_ref[...] = _rms(acc_ref[...], fg_ref[...])


def _moe(hn, gate, h2d, w_gate, w_up, w_down, fg, *, tm, tf):
    t, d = h2d.shape
    n_e, _, f = w_gate.shape
    return pl.pallas_call(
        _moe_kernel,
        grid=(t // tm, n_e, f // tf),
        in_specs=[
            pl.BlockSpec((tm, d), lambda i, e, j: (i, 0)),
            pl.BlockSpec((tm, LANES), lambda i, e, j: (i, 0)),
            pl.BlockSpec((tm, d), lambda i, e, j: (i, 0)),
            pl.BlockSpec((None, d, tf), lambda i, e, j: (e, 0, j)),
            pl.BlockSpec((None, d, tf), lambda i, e, j: (e, 0, j)),
            pl.BlockSpec((None, tf, d), lambda i, e, j: (e, j, 0)),
            pl.BlockSpec((1, d), lambda i, e, j: (0, 0)),
        ],
        out_specs=pl.BlockSpec((tm, d), lambda i, e, j: (i, 0)),
        out_shape=jax.ShapeDtypeStruct((t, d), F32),
        scratch_shapes=[pltpu.VMEM((tm, d), F32)],
        compiler_params=pltpu.CompilerParams(
            dimension_semantics=("parallel", "arbitrary", "arbitrary"),
            vmem_limit_bytes=VMEM_LIMIT),
        name="moe",
    )(hn, gate, h2d, w_gate, w_up, w_down, fg)


def _pad_heads(w, head_dim):
    k = w.shape[0]
    w = w.reshape(k, MLA_HEADS, head_dim)
    w = jnp.pad(w, ((0, 0), (0, 0), (0, LANES - head_dim)))
    return w.reshape(k, MLA_HEADS * LANES)


def _pick(n, cap):
    t = min(n, cap)
    while n % t:
        t //= 2
    return t


def kernel(x, positions, mix_norm, ffn_norm, final_norm, mla_w_in, mla_q_norm, mla_w_q_up,
           mla_kv_norm, mla_w_kv_up, mla_w_o, hgrn_w_in, hgrn_lower_bounds, hgrn_out_norm,
           hgrn_w_o, ffn_w_gate, ffn_w_up, ffn_w_down, moe_router, moe_w_gate, moe_w_up,
           moe_w_down):
    batch, seq, d = x.shape
    t = batch * seq
    x2d = x.reshape(t, d)
    row = lambda v: v.reshape(1, -1)

    w_in = mla_w_in[0]
    kr_cols = jnp.pad(w_in[:, MLA_Q_LORA + MLA_KV_LORA:],
                      ((0, 0), (MLA_NOPE, LANES - MLA_NOPE - MLA_ROPE)))
    w_in_p = jnp.concatenate([w_in[:, :MLA_Q_LORA + MLA_KV_LORA], kr_cols], axis=1).astype(BF16)
    wq_p = _pad_heads(mla_w_q_up[0], MLA_NOPE + MLA_ROPE).astype(BF16)
    w_kv = mla_w_kv_up[0].reshape(MLA_KV_LORA, MLA_HEADS, MLA_NOPE + MLA_V)
    wk_p = _pad_heads(w_kv[:, :, :MLA_NOPE].reshape(MLA_KV_LORA, -1), MLA_NOPE).astype(BF16)
    wv = w_kv[:, :, MLA_NOPE:].reshape(MLA_KV_LORA, -1).astype(BF16)
    inv_freq = ROPE_THETA ** (-jnp.arange(0, MLA_ROPE, 2, dtype=F32) / MLA_ROPE)
    freq = jnp.concatenate([jnp.zeros((MLA_NOPE,), F32), inv_freq, inv_freq,
                            jnp.zeros((LANES - MLA_NOPE - MLA_ROPE,), F32)]).reshape(1, LANES)
    pos = positions.astype(F32).reshape(t, 1)

    q, k, v = _mla_proj(x2d, pos, row(mix_norm[0]), w_in_p, row(mla_q_norm[0]), wq_p,
                        row(mla_kv_norm[0]), wk_p, wv, freq,
                        batch=batch, seq=seq, tm=_pick(seq, 512))
    o = _mla_attn(q, k, v, tq=_pick(seq, 256)).reshape(t, MLA_HEADS * MLA_V)
    h = _attn_ffn(x2d, o, mla_w_o[0].astype(BF16), row(ffn_norm[0]),
                  ffn_w_gate[0].astype(BF16), ffn_w_up[0].astype(BF16),
                  ffn_w_down[0].astype(BF16), tm=_pick(t, 1024), tf=896)

    router_p = jnp.pad(moe_router[0], ((0, 0), (0, LANES - N_EXPERTS)))
    h, hn, gate = _hgrn(h, row(mix_norm[1]), hgrn_w_in[0].astype(BF16), hgrn_lower_bounds,
                        row(hgrn_out_norm[0]), hgrn_w_o[0].astype(BF16), row(ffn_norm[1]),
                        router_p, batch=batch, seq=seq, tt=_pick(seq, 256), layer=1)
    out = _moe(hn, gate, h, moe_w_gate[0].astype(BF16), moe_w_up[0].astype(BF16),
               moe_w_down[0].astype(BF16), row(final_norm), tm=_pick(t, 1024), tf=896)
    return out.reshape(batch, seq, d)
```

```python
import functools

import jax
import jax.numpy as jnp
from jax import lax
from jax.experimental import pallas as pl
from jax.experimental.pallas import tpu as pltpu

EPS = 1e-6
LANES = 128

MLA_HEADS = 16
MLA_Q_LORA = 384
MLA_KV_LORA = 256
MLA_NOPE = 64
MLA_ROPE = 32
MLA_V = 64
ROPE_THETA = 10000.0

HG_HEADS = 8
HG_DK = 128
HG_CHUNK = 64

N_EXPERTS = 8

VMEM_LIMIT = 56 * 1024 * 1024

BF16 = jnp.bfloat16
F32 = jnp.float32


def _dot(a, b):
    return jnp.dot(a, b, preferred_element_type=F32)


def _dot_nt(a, b):
    return lax.dot_general(a, b, (((1,), (1,)), ((), ())), preferred_element_type=F32)


def _dot_tn(a, b):
    return lax.dot_general(a, b, (((0,), (0,)), ((), ())), preferred_element_type=F32)


def _rms(x, g):
    return x * lax.rsqrt(jnp.mean(x * x, axis=-1, keepdims=True) + EPS) * g


def _silu(x):
    return x * (1.0 / (1.0 + jnp.exp(-x)))


def _split3(x):
    hi = x.astype(BF16)
    r = x - hi.astype(F32)
    mid = r.astype(BF16)
    lo = (r - mid.astype(F32)).astype(BF16)
    return hi, mid, lo


def _mla_proj_kernel(x_ref, pos_ref, g_ref, win_ref, wkr_ref, qg_ref, wq_ref, kvg_ref, wk_ref,
                     wv_ref, freq_ref, q_out, k_out, v_out, *, scale):
    half = MLA_ROPE // 2
    x1_rows = slice(MLA_NOPE, MLA_NOPE + half)
    x2_rows = slice(MLA_NOPE + half, MLA_NOPE + MLA_ROPE)
    hn = _rms(x_ref[...], g_ref[...]).astype(BF16)
    proj = _dot(hn, win_ref[...])
    cqn = _rms(proj[:, :MLA_Q_LORA], qg_ref[...]).astype(BF16)
    ckvn = _rms(proj[:, MLA_Q_LORA:], kvg_ref[...]).astype(BF16)

    ang = freq_ref[...] * pos_ref[...]
    cos = jnp.cos(ang)
    sin = jnp.sin(ang)

    def rope_t(slot, mul):
        x1 = slot[x1_rows]
        x2 = slot[x2_rows]
        return jnp.concatenate([slot[:MLA_NOPE] * mul, (x1 * cos - x2 * sin) * mul,
                                (x2 * cos + x1 * sin) * mul, slot[MLA_NOPE + MLA_ROPE:]], axis=0)

    kr_t = _dot_nt(wkr_ref[...], hn)
    kr_roped = rope_t(kr_t, 1.0).T
    q_t = _dot_nt(wq_ref[...], cqn)
    k_pad = _dot(ckvn, wk_ref[...])
    for h in range(MLA_HEADS):
        sl = slice(h * LANES, (h + 1) * LANES)
        q_out[h] = rope_t(q_t[sl], scale).astype(BF16)
        k_out[h] = (k_pad[:, sl] + kr_roped).astype(BF16)
    v_out[...] = _dot_nt(wv_ref[...], ckvn).astype(BF16)


def _mla_proj(x2d, pos, g, w_in_p, wkr_t, qg, wq_t, kvg, wk_p, wv_t, freq, *, batch, seq, tm):
    t = x2d.shape[0]
    d = x2d.shape[1]
    nt = seq // tm
    const = lambda shape: pl.BlockSpec(shape, lambda i: (0,) * len(shape))
    scale = (MLA_NOPE + MLA_ROPE) ** -0.5 * 1.4426950408889634
    return pl.pallas_call(
        functools.partial(_mla_proj_kernel, scale=scale),
        grid=(t // tm,),
        in_specs=[
            pl.BlockSpec((tm, d), lambda i: (i, 0)),
            pl.BlockSpec((1, tm), lambda i: (0, i)),
            const((1, d)),
            const(w_in_p.shape),
            const(wkr_t.shape),
            const((1, MLA_Q_LORA)),
            const(wq_t.shape),
            const((1, MLA_KV_LORA)),
            const(wk_p.shape),
            const(wv_t.shape),
            const(freq.shape),
        ],
        out_specs=[
            pl.BlockSpec((None, MLA_HEADS, LANES, tm), lambda i: (i // nt, 0, 0, i % nt)),
            pl.BlockSpec((None, MLA_HEADS, tm, LANES), lambda i: (i // nt, 0, i % nt, 0)),
            pl.BlockSpec((None, MLA_HEADS * MLA_V, tm), lambda i: (i // nt, 0, i % nt)),
        ],
        out_shape=[
            jax.ShapeDtypeStruct((batch, MLA_HEADS, LANES, seq), BF16),
            jax.ShapeDtypeStruct((batch, MLA_HEADS, seq, LANES), BF16),
            jax.ShapeDtypeStruct((batch, MLA_HEADS * MLA_V, seq), BF16),
        ],
        compiler_params=pltpu.CompilerParams(
            dimension_semantics=("parallel",), vmem_limit_bytes=VMEM_LIMIT),
        name="mla_proj",
    )(x2d, pos, g, w_in_p, wkr_t, qg, wq_t, kvg, wk_p, wv_t, freq)


def _attn_kernel(q_ref, k_ref, v_ref, o_ref, acc_ref, *, tq):
    seq = k_ref.shape[1]
    nq = seq // tq
    causal = (lax.broadcasted_iota(jnp.int32, (tq, tq), 0)
              <= lax.broadcasted_iota(jnp.int32, (tq, tq), 1))

    def step(q_t, h, k0, carry, masked):
        m, l = carry
        s = _dot(k_ref[h, pl.ds(k0, tq), :], q_t)
        if masked:
            s = jnp.where(causal, s, -jnp.inf)
        m_new = jnp.maximum(m, jnp.max(s, axis=0, keepdims=True))
        p = jnp.exp2(s - m_new)
        alpha = jnp.exp2(m - m_new)
        l = alpha * l + jnp.sum(p, axis=0, keepdims=True)
        v_t = v_ref[h * MLA_V:(h + 1) * MLA_V, pl.ds(k0, tq)]
        acc_ref[h] = alpha * acc_ref[h] + _dot(v_t, p.astype(BF16))
        return m_new, l

    def q_block(qi, _):
        q0 = pl.multiple_of(qi * tq, tq)
        qs = [q_ref[h, :, pl.ds(q0, tq)] for h in range(2)]
        acc_ref[...] = jnp.zeros_like(acc_ref)
        init = tuple((jnp.full((1, tq), -jnp.inf, F32), jnp.zeros((1, tq), F32))
                     for _ in range(2))

        def k_block(ki, carry):
            k0 = pl.multiple_of(ki * tq, tq)
            return tuple(step(qs[h], h, k0, carry[h], False) for h in range(2))

        carry = lax.fori_loop(0, qi, k_block, init)
        outs = []
        for h in range(2):
            _, l = step(qs[h], h, q0, carry[h], True)
            outs.append(acc_ref[h] / l)
        o_ref[pl.ds(q0, tq), :] = jnp.concatenate(outs, axis=0).T.astype(BF16)
        return 0

    lax.fori_loop(0, nq, q_block, 0)


def _mla_attn(q_t, k, v_t, *, tq):
    batch, heads, seq, _ = k.shape
    return pl.pallas_call(
        functools.partial(_attn_kernel, tq=tq),
        grid=(batch, heads // 2),
        in_specs=[
            pl.BlockSpec((None, 2, LANES, seq), lambda b, hp: (b, hp, 0, 0)),
            pl.BlockSpec((None, 2, seq, LANES), lambda b, hp: (b, hp, 0, 0)),
            pl.BlockSpec((None, 2 * MLA_V, seq), lambda b, hp: (b, hp, 0)),
        ],
        out_specs=pl.BlockSpec((None, seq, 2 * MLA_V), lambda b, hp: (b, 0, hp)),
        out_shape=jax.ShapeDtypeStruct((batch, seq, heads * MLA_V), BF16),
        scratch_shapes=[pltpu.VMEM((2, MLA_V, tq), F32)],
        compiler_params=pltpu.CompilerParams(
            dimension_semantics=("parallel", "parallel"), vmem_limit_bytes=VMEM_LIMIT),
        name="mla_attn",
    )(q_t, k, v_t)


def _attn_ffn_kernel(x_ref, o_ref, wo_ref, g_ref, wg_ref, wu_ref, wd_ref, out_ref, hn_ref):
    j = pl.program_id(1)

    @pl.when(j == 0)
    def _():
        h1 = x_ref[...] + _dot(o_ref[...], wo_ref[...])
        out_ref[...] = h1
        hn_ref[...] = _rms(h1, g_ref[...]).astype(BF16)

    hn = hn_ref[...]
    a = _silu(_dot(hn, wg_ref[...])) * _dot(hn, wu_ref[...])
    out_ref[...] += _dot(a.astype(BF16), wd_ref[...])


def _attn_ffn(x2d, o, w_o, g, w_gate, w_up, w_down, *, tm, tf):
    t, d = x2d.shape
    f = w_gate.shape[1]
    return pl.pallas_call(
        _attn_ffn_kernel,
        grid=(t // tm, f // tf),
        in_specs=[
            pl.BlockSpec((tm, d), lambda i, j: (i, 0)),
            pl.BlockSpec((tm, o.shape[1]), lambda i, j: (i, 0)),
            pl.BlockSpec(w_o.shape, lambda i, j: (0, 0)),
            pl.BlockSpec((1, d), lambda i, j: (0, 0)),
            pl.BlockSpec((d, tf), lambda i, j: (0, j)),
            pl.BlockSpec((d, tf), lambda i, j: (0, j)),
            pl.BlockSpec((tf, d), lambda i, j: (j, 0)),
        ],
        out_specs=pl.BlockSpec((tm, d), lambda i, j: (i, 0)),
        out_shape=jax.ShapeDtypeStruct((t, d), F32),
        scratch_shapes=[pltpu.VMEM((tm, d), BF16)],
        compiler_params=pltpu.CompilerParams(
            dimension_semantics=("parallel", "arbitrary"), vmem_limit_bytes=VMEM_LIMIT),
        name="attn_ffn",
    )(x2d, o, w_o, g, w_gate, w_up, w_down)


def _hgrn_kernel(h_ref, g_ref, win_ref, lbraw_ref, og_ref, wo_ref, fg_ref, router_ref,
                 h_out, hn_out, gate_out, st_ref, o_scr, *, layer):
    tt = h_ref.shape[0]
    width = HG_HEADS * HG_DK
    nchunk = tt // HG_CHUNK

    @pl.when(pl.program_id(1) == 0)
    def _():
        st_ref[...] = jnp.zeros_like(st_ref)

    lbraw = lbraw_ref[...]
    e = jnp.exp(lbraw - jnp.max(lbraw, axis=0, keepdims=True))
    sm = e / jnp.sum(e, axis=0, keepdims=True)
    lb = jnp.sum(sm[1:layer + 1], axis=0, keepdims=True)

    h_in = h_ref[...]
    hn = _rms(h_in, g_ref[...]).astype(BF16)
    proj = _dot(hn, win_ref[...])
    q_all = proj[:, :width] * (HG_DK ** -0.5)
    f_all = lb + (1.0 - lb) * (1.0 / (1.0 + jnp.exp(-proj[:, width:2 * width])))
    logf = jnp.log(f_all)
    k_all = 1.0 - f_all

    r = lax.broadcasted_iota(jnp.int32, (tt, tt), 0)
    c = lax.broadcasted_iota(jnp.int32, (tt, tt), 1)
    shift = HG_CHUNK.bit_length() - 1
    tril = (c <= r) & ((r >> shift) == (c >> shift))
    tril_b = jnp.where(tril, 1.0, 0.0).astype(BF16)
    hi, mid, lo = _split3(logf)
    b_all = _dot(tril_b, hi) + _dot(tril_b, mid) + _dot(tril_b, lo)

    for h in range(HG_HEADS):
        sl = slice(h * HG_DK, (h + 1) * HG_DK)
        q = q_all[:, sl]
        k = k_all[:, sl]
        b = b_all[:, sl]
        v = proj[:, 2 * width + h * HG_DK:2 * width + (h + 1) * HG_DK]
        gt = proj[:, 3 * width + h * HG_DK:3 * width + (h + 1) * HG_DK]
        v_b = v.astype(BF16)
        qd = (q * jnp.exp(b)).astype(BF16)
        kd = (k * jnp.exp(-b)).astype(BF16)
        a = jnp.where(tril, _dot_nt(qd, kd), 0.0)
        o = _dot(a.astype(BF16), v_b)

        st = st_ref[h]
        inter = []
        for n in range(nchunk):
            rows = slice(n * HG_CHUNK, (n + 1) * HG_CHUNK)
            b_n = b[rows]
            b_last = b_n[HG_CHUNK - 1:HG_CHUNK]
            inter.append(_dot_nt(qd[rows], st.astype(BF16)))
            kdl = (k[rows] * jnp.exp(b_last - b_n)).astype(BF16)
            st = st * jnp.exp(b_last) + _dot_tn(v_b[rows], kdl)
        st_ref[h] = st
        o = o + jnp.concatenate(inter, axis=0)
        o = _rms(o, og_ref[...]) * _silu(gt)
        o_scr[:, sl] = o.astype(BF16)

    h_new = h_in + _dot(o_scr[...], wo_ref[...])
    h_out[...] = h_new
    hn2 = _rms(h_new, fg_ref[...])
    hn_out[...] = hn2.astype(BF16)

    x_hi, x_mid, _ = _split3(hn2)
    router = router_ref[...]
    r_hi = router.astype(BF16)
    r_lo = (router - r_hi.astype(F32)).astype(BF16)
    logits = _dot(x_hi, r_hi) + _dot(x_mid, r_hi) + _dot(x_hi, r_lo)
    lane = lax.broadcasted_iota(jnp.int32, logits.shape, 1)
    neg = -jnp.inf
    logits = jnp.where(lane < N_EXPERTS, logits, neg)
    m1 = jnp.max(logits, axis=-1, keepdims=True)
    i1 = jnp.min(jnp.where(logits == m1, lane, LANES), axis=-1, keepdims=True)
    rest = jnp.where(lane == i1, neg, logits)
    m2 = jnp.max(rest, axis=-1, keepdims=True)
    i2 = jnp.min(jnp.where(rest == m2, lane, LANES), axis=-1, keepdims=True)
    e2 = jnp.exp(m2 - m1)
    w1 = 1.0 / (1.0 + e2)
    w2 = e2 / (1.0 + e2)
    gate_out[...] = jnp.where(lane == i1, w1, 0.0) + jnp.where(lane == i2, w2, 0.0)


def _hgrn(h2d, g, w_in, lb_raw, og, w_o, fg, router_p, *, batch, seq, tt, layer):
    t, d = h2d.shape
    nt = seq // tt
    width = HG_HEADS * HG_DK
    tok = lambda cols: pl.BlockSpec((tt, cols), lambda b, i: (b * nt + i, 0))
    const = lambda shape: pl.BlockSpec(shape, lambda b, i: (0,) * len(shape))
    return pl.pallas_call(
        functools.partial(_hgrn_kernel, layer=layer),
        grid=(batch, nt),
        in_specs=[tok(d), const((1, d)), const(w_in.shape), const(lb_raw.shape),
                  const((1, HG_DK)), const(w_o.shape), const((1, d)), const(router_p.shape)],
        out_specs=[tok(d), tok(d), tok(LANES)],
        out_shape=[jax.ShapeDtypeStruct((t, d), F32),
                   jax.ShapeDtypeStruct((t, d), BF16),
                   jax.ShapeDtypeStruct((t, LANES), F32)],
        scratch_shapes=[pltpu.VMEM((HG_HEADS, HG_DK, HG_DK), F32),
                        pltpu.VMEM((tt, width), BF16)],
        compiler_params=pltpu.CompilerParams(
            dimension_semantics=("parallel", "arbitrary"), vmem_limit_bytes=VMEM_LIMIT),
        name="hgrn",
    )(h2d, g, w_in, lb_raw, og, w_o, fg, router_p)


def _moe_kernel(hn_ref, gate_ref, h_ref, wg_ref, wu_ref, wd_ref, fg_ref, out_ref, acc_ref):
    e = pl.program_id(1)
    j = pl.program_id(2)

    @pl.when((e == 0) & (j == 0))
    def _():
        acc_ref[...] = h_ref[...]

    gate = gate_ref[...]
    lane = lax.broadcasted_iota(jnp.int32, gate.shape, 1)
    w = jnp.sum(jnp.where(lane == e, gate, 0.0), axis=-1, keepdims=True)
    hn = hn_ref[...]
    a = _silu(_dot(hn, wg_ref[...])) * _dot(hn, wu_ref[...]) * w
    acc_ref[...] += _dot(a.astype(BF16), wd_ref[...])

    @pl.when((e == pl.num_programs(1) - 1) & (j == pl.num_programs(2) - 1))
    def _():
        out_ref[...] = _rms(acc_ref[...], fg_ref[...])


def _moe(hn, gate, h2d, w_gate, w_up, w_down, fg, *, tm, tf):
    t, d = h2d.shape
    n_e, _, f = w_gate.shape
    return pl.pallas_call(
        _moe_kernel,
        grid=(t // tm, n_e, f // tf),
        in_specs=[
            pl.BlockSpec((tm, d), lambda i, e, j: (i, 0)),
            pl.BlockSpec((tm, LANES), lambda i, e, j: (i, 0)),
            pl.BlockSpec((tm, d), lambda i, e, j: (i, 0)),
            pl.BlockSpec((None, d, tf), lambda i, e, j: (e, 0, j)),
            pl.BlockSpec((None, d, tf), lambda i, e, j: (e, 0, j)),
            pl.BlockSpec((None, tf, d), lambda i, e, j: (e, j, 0)),
            pl.BlockSpec((1, d), lambda i, e, j: (0, 0)),
        ],
        out_specs=pl.BlockSpec((tm, d), lambda i, e, j: (i, 0)),
        out_shape=jax.ShapeDtypeStruct((t, d), F32),
        scratch_shapes=[pltpu.VMEM((tm, d), F32)],
        compiler_params=pltpu.CompilerParams(
            dimension_semantics=("parallel", "arbitrary", "arbitrary"),
            vmem_limit_bytes=VMEM_LIMIT),
        name="moe",
    )(hn, gate, h2d, w_gate, w_up, w_down, fg)


def _pad_heads(w, head_dim):
    k = w.shape[0]
    w = w.reshape(k, MLA_HEADS, head_dim)
    w = jnp.pad(w, ((0, 0), (0, 0), (0, LANES - head_dim)))
    return w.reshape(k, MLA_HEADS * LANES)


def _pick(n, cap):
    t = min(n, cap)
    while n % t:
        t //= 2
    return t


def kernel(x, positions, mix_norm, ffn_norm, final_norm, mla_w_in, mla_q_norm, mla_w_q_up,
           mla_kv_norm, mla_w_kv_up, mla_w_o, hgrn_w_in, hgrn_lower_bounds, hgrn_out_norm,
           hgrn_w_o, ffn_w_gate, ffn_w_up, ffn_w_down, moe_router, moe_w_gate, moe_w_up,
           moe_w_down):
    batch, seq, d = x.shape
    t = batch * seq
    x2d = x.reshape(t, d)
    row = lambda v: v.reshape(1, -1)

    w_in = mla_w_in[0]
    n_lat = MLA_Q_LORA + MLA_KV_LORA
    w_in_p = w_in[:, :n_lat].astype(BF16)
    wkr_t = jnp.pad(w_in[:, n_lat:].T,
                    ((MLA_NOPE, LANES - MLA_NOPE - MLA_ROPE), (0, 0))).astype(BF16)
    wq_t = _pad_heads(mla_w_q_up[0], MLA_NOPE + MLA_ROPE).T.astype(BF16)
    w_kv = mla_w_kv_up[0].reshape(MLA_KV_LORA, MLA_HEADS, MLA_NOPE + MLA_V)
    wk_p = _pad_heads(w_kv[:, :, :MLA_NOPE].reshape(MLA_KV_LORA, -1), MLA_NOPE).astype(BF16)
    wv_t = w_kv[:, :, MLA_NOPE:].reshape(MLA_KV_LORA, -1).T.astype(BF16)
    inv_freq = ROPE_THETA ** (-jnp.arange(0, MLA_ROPE, 2, dtype=F32) / MLA_ROPE)
    freq = inv_freq.reshape(MLA_ROPE // 2, 1)
    pos = positions.astype(F32).reshape(1, t)

    q_t, k, v_t = _mla_proj(x2d, pos, row(mix_norm[0]), w_in_p, wkr_t, row(mla_q_norm[0]), wq_t,
                            row(mla_kv_norm[0]), wk_p, wv_t, freq,
                            batch=batch, seq=seq, tm=_pick(seq, 512))
    o = _mla_attn(q_t, k, v_t, tq=_pick(seq, 512)).reshape(t, MLA_HEADS * MLA_V)
    h = _attn_ffn(x2d, o, mla_w_o[0].astype(BF16), row(ffn_norm[0]),
                  ffn_w_gate[0].astype(BF16), ffn_w_up[0].astype(BF16),
                  ffn_w_down[0].astype(BF16), tm=_pick(t, 1024), tf=896)

    router_p = jnp.pad(moe_router[0], ((0, 0), (0, LANES - N_EXPERTS)))
    h, hn, gate = _hgrn(h, row(mix_norm[1]), hgrn_w_in[0].astype(BF16), hgrn_lower_bounds,
                        row(hgrn_out_norm[0]), hgrn_w_o[0].astype(BF16), row(ffn_norm[1]),
                        router_p, batch=batch, seq=seq, tt=_pick(seq, 256), layer=1)
    out = _moe(hn, gate, h, moe_w_gate[0].astype(BF16), moe_w_up[0].astype(BF16),
               moe_w_down[0].astype(BF16), row(final_norm), tm=_pick(t, 1024), tf=896)
    return out.reshape(batch, seq, d)
```

```python
import functools

import jax
import jax.numpy as jnp
from jax import lax
from jax.experimental import pallas as pl
from jax.experimental.pallas import tpu as pltpu

EPS = 1e-6
LANES = 128
SUBLANES = 8

MLA_HEADS = 16
MLA_Q_LORA = 384
MLA_KV_LORA = 256
MLA_NOPE = 64
MLA_ROPE = 32
MLA_V = 64
ROPE_THETA = 10000.0

HG_HEADS = 8
HG_DK = 128
HG_CHUNK = 64

N_EXPERTS = 8

VMEM_LIMIT = 56 * 1024 * 1024

BF16 = jnp.bfloat16
F32 = jnp.float32


def _dot(a, b):
    return jnp.dot(a, b, preferred_element_type=F32)


def _dot_nt(a, b):
    return lax.dot_general(a, b, (((1,), (1,)), ((), ())), preferred_element_type=F32)


def _dot_tn(a, b):
    return lax.dot_general(a, b, (((0,), (0,)), ((), ())), preferred_element_type=F32)


def _rms(x, g):
    return x * lax.rsqrt(jnp.mean(x * x, axis=-1, keepdims=True) + EPS) * g


def _silu(x):
    return x * (1.0 / (1.0 + jnp.exp(-x)))


def _split3(x):
    hi = x.astype(BF16)
    r = x - hi.astype(F32)
    mid = r.astype(BF16)
    lo = (r - mid.astype(F32)).astype(BF16)
    return hi, mid, lo


def _mla_proj_kernel(x_ref, pos_ref, g_ref, win_ref, wkr_ref, qg_ref, wq_ref, kvg_ref, wk_ref,
                     wv_ref, freq_ref, q_out, k_out, v_out, *, scale):
    half = MLA_ROPE // 2
    x1_rows = slice(MLA_NOPE, MLA_NOPE + half)
    x2_rows = slice(MLA_NOPE + half, MLA_NOPE + MLA_ROPE)
    hn = _rms(x_ref[...], g_ref[...]).astype(BF16)
    proj = _dot(hn, win_ref[...])
    cqn = _rms(proj[:, :MLA_Q_LORA], qg_ref[...]).astype(BF16)
    ckvn = _rms(proj[:, MLA_Q_LORA:], kvg_ref[...]).astype(BF16)

    ang = freq_ref[...] * pos_ref[...]
    cos = jnp.cos(ang)
    sin = jnp.sin(ang)

    def rope_t(slot, mul):
        x1 = slot[x1_rows]
        x2 = slot[x2_rows]
        return jnp.concatenate([slot[:MLA_NOPE] * mul, (x1 * cos - x2 * sin) * mul,
                                (x2 * cos + x1 * sin) * mul, slot[MLA_NOPE + MLA_ROPE:]], axis=0)

    kr_t = _dot_nt(wkr_ref[...], hn)
    kr_roped = rope_t(kr_t, 1.0).T
    q_t = _dot_nt(wq_ref[...], cqn)
    k_pad = _dot(ckvn, wk_ref[...])
    for h in range(MLA_HEADS):
        sl = slice(h * LANES, (h + 1) * LANES)
        q_out[h] = rope_t(q_t[sl], scale).astype(BF16)
        k_out[h] = (k_pad[:, sl] + kr_roped).astype(BF16)
    v_out[...] = _dot_nt(wv_ref[...], ckvn).astype(BF16)


def _mla_proj(x2d, pos, g, w_in_p, wkr_t, qg, wq_t, kvg, wk_p, wv_t, freq, *, batch, seq, tm):
    t = x2d.shape[0]
    d = x2d.shape[1]
    nt = seq // tm
    const = lambda shape: pl.BlockSpec(shape, lambda i: (0,) * len(shape))
    scale = (MLA_NOPE + MLA_ROPE) ** -0.5 * 1.4426950408889634
    return pl.pallas_call(
        functools.partial(_mla_proj_kernel, scale=scale),
        grid=(t // tm,),
        in_specs=[
            pl.BlockSpec((tm, d), lambda i: (i, 0)),
            pl.BlockSpec((1, tm), lambda i: (0, i)),
            const((1, d)),
            const(w_in_p.shape),
            const(wkr_t.shape),
            const((1, MLA_Q_LORA)),
            const(wq_t.shape),
            const((1, MLA_KV_LORA)),
            const(wk_p.shape),
            const(wv_t.shape),
            const(freq.shape),
        ],
        out_specs=[
            pl.BlockSpec((None, MLA_HEADS, LANES, tm), lambda i: (i // nt, 0, 0, i % nt)),
            pl.BlockSpec((None, MLA_HEADS, tm, LANES), lambda i: (i // nt, 0, i % nt, 0)),
            pl.BlockSpec((None, MLA_HEADS * MLA_V, tm), lambda i: (i // nt, 0, i % nt)),
        ],
        out_shape=[
            jax.ShapeDtypeStruct((batch, MLA_HEADS, LANES, seq), BF16),
            jax.ShapeDtypeStruct((batch, MLA_HEADS, seq, LANES), BF16),
            jax.ShapeDtypeStruct((batch, MLA_HEADS * MLA_V, seq), BF16),
        ],
        compiler_params=pltpu.CompilerParams(
            dimension_semantics=("parallel",), vmem_limit_bytes=VMEM_LIMIT),
        name="mla_proj",
    )(x2d, pos, g, w_in_p, wkr_t, qg, wq_t, kvg, wk_p, wv_t, freq)


def _attn_kernel(q_ref, k_ref, v_ref, o_ref, acc_ref, *, tq):
    seq = k_ref.shape[1]
    nq = seq // tq
    causal = (lax.broadcasted_iota(jnp.int32, (tq, tq), 0)
              <= lax.broadcasted_iota(jnp.int32, (tq, tq), 1))

    def step(q_t, h, k0, carry, masked):
        m, l = carry
        s = _dot(k_ref[h, pl.ds(k0, tq), :], q_t)
        if masked:
            s = jnp.where(causal, s, -jnp.inf)
        m_new = jnp.maximum(m, jnp.max(s, axis=0, keepdims=True))
        p = jnp.exp2(s - m_new)
        alpha = jnp.exp2(m - m_new)
        l = alpha * l + jnp.sum(p, axis=0, keepdims=True)
        v_t = v_ref[h * MLA_V:(h + 1) * MLA_V, pl.ds(k0, tq)]
        acc_ref[h] = alpha * acc_ref[h] + _dot(v_t, p.astype(BF16))
        return m_new, l

    def q_block(qi, _):
        q0 = pl.multiple_of(qi * tq, tq)
        qs = [q_ref[h, :, pl.ds(q0, tq)] for h in range(2)]
        acc_ref[...] = jnp.zeros_like(acc_ref)
        init = tuple((jnp.full((1, tq), -jnp.inf, F32), jnp.zeros((1, tq), F32))
                     for _ in range(2))

        def k_block(ki, carry):
            k0 = pl.multiple_of(ki * tq, tq)
            return tuple(step(qs[h], h, k0, carry[h], False) for h in range(2))

        carry = lax.fori_loop(0, qi, k_block, init)
        outs = []
        for h in range(2):
            _, l = step(qs[h], h, q0, carry[h], True)
            outs.append(acc_ref[h] / l)
        o_ref[pl.ds(q0, tq), :] = jnp.concatenate(outs, axis=0).T.astype(BF16)
        return 0

    lax.fori_loop(0, nq, q_block, 0)


def _mla_attn(q_t, k, v_t, *, tq):
    batch, heads, seq, _ = k.shape
    return pl.pallas_call(
        functools.partial(_attn_kernel, tq=tq),
        grid=(batch, heads // 2),
        in_specs=[
            pl.BlockSpec((None, 2, LANES, seq), lambda b, hp: (b, hp, 0, 0)),
            pl.BlockSpec((None, 2, seq, LANES), lambda b, hp: (b, hp, 0, 0)),
            pl.BlockSpec((None, 2 * MLA_V, seq), lambda b, hp: (b, hp, 0)),
        ],
        out_specs=pl.BlockSpec((None, seq, 2 * MLA_V), lambda b, hp: (b, 0, hp)),
        out_shape=jax.ShapeDtypeStruct((batch, seq, heads * MLA_V), BF16),
        scratch_shapes=[pltpu.VMEM((2, MLA_V, tq), F32)],
        compiler_params=pltpu.CompilerParams(
            dimension_semantics=("parallel", "parallel"), vmem_limit_bytes=VMEM_LIMIT),
        name="mla_attn",
    )(q_t, k, v_t)


def _attn_ffn_kernel(x_ref, o_ref, wo_ref, g_ref, wg_ref, wu_ref, wd_ref, out_ref, hn_ref):
    j = pl.program_id(1)

    @pl.when(j == 0)
    def _():
        h1 = x_ref[...] + _dot(o_ref[...], wo_ref[...])
        out_ref[...] = h1
        hn_ref[...] = _rms(h1, g_ref[...]).astype(BF16)

    hn = hn_ref[...]
    a = _silu(_dot(hn, wg_ref[...])) * _dot(hn, wu_ref[...])
    out_ref[...] += _dot(a.astype(BF16), wd_ref[...])


def _attn_ffn(x2d, o, w_o, g, w_gate, w_up, w_down, *, tm, tf):
    t, d = x2d.shape
    f = w_gate.shape[1]
    return pl.pallas_call(
        _attn_ffn_kernel,
        grid=(t // tm, f // tf),
        in_specs=[
            pl.BlockSpec((tm, d), lambda i, j: (i, 0)),
            pl.BlockSpec((tm, o.shape[1]), lambda i, j: (i, 0)),
            pl.BlockSpec(w_o.shape, lambda i, j: (0, 0)),
            pl.BlockSpec((1, d), lambda i, j: (0, 0)),
            pl.BlockSpec((d, tf), lambda i, j: (0, j)),
            pl.BlockSpec((d, tf), lambda i, j: (0, j)),
            pl.BlockSpec((tf, d), lambda i, j: (j, 0)),
        ],
        out_specs=pl.BlockSpec((tm, d), lambda i, j: (i, 0)),
        out_shape=jax.ShapeDtypeStruct((t, d), F32),
        scratch_shapes=[pltpu.VMEM((tm, d), BF16)],
        compiler_params=pltpu.CompilerParams(
            dimension_semantics=("parallel", "arbitrary"), vmem_limit_bytes=VMEM_LIMIT),
        name="attn_ffn",
    )(x2d, o, w_o, g, w_gate, w_up, w_down)


def _hgrn_kernel(h_ref, g_ref, win_ref, lbraw_ref, og_ref, wo_ref, fg_ref, router_ref,
                 h_out, hn_out, meta_out, cnt_out, st_ref, o_scr, cnt_ref, *, layer):
    tt = h_ref.shape[0]
    width = HG_HEADS * HG_DK
    nchunk = tt // HG_CHUNK

    @pl.when(pl.program_id(1) == 0)
    def _():
        st_ref[...] = jnp.zeros_like(st_ref)

    lbraw = lbraw_ref[...]
    e = jnp.exp(lbraw - jnp.max(lbraw, axis=0, keepdims=True))
    sm = e / jnp.sum(e, axis=0, keepdims=True)
    lb = jnp.sum(sm[1:layer + 1], axis=0, keepdims=True)

    h_in = h_ref[...]
    hn = _rms(h_in, g_ref[...]).astype(BF16)
    proj = _dot(hn, win_ref[...])
    q_all = proj[:, :width] * (HG_DK ** -0.5)
    f_all = lb + (1.0 - lb) * (1.0 / (1.0 + jnp.exp(-proj[:, width:2 * width])))
    logf = jnp.log(f_all)
    k_all = 1.0 - f_all

    r = lax.broadcasted_iota(jnp.int32, (tt, tt), 0)
    c = lax.broadcasted_iota(jnp.int32, (tt, tt), 1)
    shift = HG_CHUNK.bit_length() - 1
    tril = (c <= r) & ((r >> shift) == (c >> shift))
    tril_b = jnp.where(tril, 1.0, 0.0).astype(BF16)
    hi, mid, lo = _split3(logf)
    b_all = _dot(tril_b, hi) + _dot(tril_b, mid) + _dot(tril_b, lo)

    for h in range(HG_HEADS):
        sl = slice(h * HG_DK, (h + 1) * HG_DK)
        q = q_all[:, sl]
        k = k_all[:, sl]
        b = b_all[:, sl]
        v = proj[:, 2 * width + h * HG_DK:2 * width + (h + 1) * HG_DK]
        gt = proj[:, 3 * width + h * HG_DK:3 * width + (h + 1) * HG_DK]
        v_b = v.astype(BF16)
        qd = (q * jnp.exp(b)).astype(BF16)
        kd = (k * jnp.exp(-b)).astype(BF16)
        a = jnp.where(tril, _dot_nt(qd, kd), 0.0)
        o = _dot(a.astype(BF16), v_b)

        st = st_ref[h]
        inter = []
        for n in range(nchunk):
            rows = slice(n * HG_CHUNK, (n + 1) * HG_CHUNK)
            b_n = b[rows]
            b_last = b_n[HG_CHUNK - 1:HG_CHUNK]
            inter.append(_dot_nt(qd[rows], st.astype(BF16)))
            kdl = (k[rows] * jnp.exp(b_last - b_n)).astype(BF16)
            st = st * jnp.exp(b_last) + _dot_tn(v_b[rows], kdl)
        st_ref[h] = st
        o = o + jnp.concatenate(inter, axis=0)
        o = _rms(o, og_ref[...]) * _silu(gt)
        o_scr[:, sl] = o.astype(BF16)

    h_new = h_in + _dot(o_scr[...], wo_ref[...])
    h_out[...] = h_new
    hn2 = _rms(h_new, fg_ref[...])
    hn_out[...] = hn2

    x_hi, x_mid, _ = _split3(hn2)
    router = router_ref[...]
    r_hi = router.astype(BF16)
    r_lo = (router - r_hi.astype(F32)).astype(BF16)
    logits = _dot(x_hi, r_hi) + _dot(x_mid, r_hi) + _dot(x_hi, r_lo)
    lane = lax.broadcasted_iota(jnp.int32, logits.shape, 1)
    neg = -jnp.inf
    logits = jnp.where(lane < N_EXPERTS, logits, neg)
    m1 = jnp.max(logits, axis=-1, keepdims=True)
    i1 = jnp.min(jnp.where(logits == m1, lane, LANES), axis=-1, keepdims=True)
    rest = jnp.where(lane == i1, neg, logits)
    m2 = jnp.max(rest, axis=-1, keepdims=True)
    i2 = jnp.min(jnp.where(rest == m2, lane, LANES), axis=-1, keepdims=True)
    e2 = jnp.exp(m2 - m1)
    w1 = 1.0 / (1.0 + e2)
    w2 = e2 / (1.0 + e2)

    @pl.when((pl.program_id(0) == 0) & (pl.program_id(1) == 0))
    def _():
        cnt_ref[...] = jnp.zeros_like(cnt_ref)

    onehot = jnp.where((lane == i1) | (lane == i2), 1.0, 0.0)
    earlier = jnp.where(c < r, 1.0, 0.0).astype(BF16)
    rank = _dot(earlier, onehot.astype(BF16)) + cnt_ref[...]
    cnt = cnt_ref[...] + jnp.sum(onehot, axis=0, keepdims=True)
    cnt_ref[...] = cnt
    cnt_out[...] = jnp.broadcast_to(cnt, cnt_out.shape)
    rank1 = jnp.sum(jnp.where(lane == i1, rank, 0.0), axis=-1, keepdims=True)
    rank2 = jnp.sum(jnp.where(lane == i2, rank, 0.0), axis=-1, keepdims=True)
    fields = (i1.astype(F32), i2.astype(F32), w1, w2, rank1, rank2)
    meta = jnp.zeros(logits.shape, F32)
    for idx, val in enumerate(fields):
        meta = jnp.where(lane == idx, val, meta)
    meta_out[...] = meta


def _hgrn(h2d, g, w_in, lb_raw, og, w_o, fg, router_p, *, batch, seq, tt, layer):
    t, d = h2d.shape
    nt = seq // tt
    width = HG_HEADS * HG_DK
    tok = lambda cols: pl.BlockSpec((tt, cols), lambda b, i: (b * nt + i, 0))
    const = lambda shape: pl.BlockSpec(shape, lambda b, i: (0,) * len(shape))
    return pl.pallas_call(
        functools.partial(_hgrn_kernel, layer=layer),
        grid=(batch, nt),
        in_specs=[tok(d), const((1, d)), const(w_in.shape), const(lb_raw.shape),
                  const((1, HG_DK)), const(w_o.shape), const((1, d)), const(router_p.shape)],
        out_specs=[tok(d), tok(d), tok(LANES), const((SUBLANES, LANES))],
        out_shape=[jax.ShapeDtypeStruct((t, d), F32),
                   jax.ShapeDtypeStruct((t, d), F32),
                   jax.ShapeDtypeStruct((t, LANES), F32),
                   jax.ShapeDtypeStruct((SUBLANES, LANES), F32)],
        scratch_shapes=[pltpu.VMEM((HG_HEADS, HG_DK, HG_DK), F32),
                        pltpu.VMEM((tt, width), BF16),
                        pltpu.VMEM((1, LANES), F32)],
        compiler_params=pltpu.CompilerParams(
            dimension_semantics=("arbitrary", "arbitrary"), vmem_limit_bytes=VMEM_LIMIT),
        name="hgrn",
    )(h2d, g, w_in, lb_raw, og, w_o, fg, router_p)


def _row_copy(src_hbm, dst_hbm, src_row, dst_row, sem):
    return pltpu.make_async_copy(src_hbm.at[pl.ds(src_row, 1)], dst_hbm.at[pl.ds(dst_row, 1)], sem)


def _group_rows_kernel(d1_ref, d2_ref, x_hbm, init_hbm, xs_hbm, sems, *, chunk):
    del init_hbm
    n_chunks = x_hbm.shape[0] // chunk

    def wait_chunk(slot):
        pltpu.make_async_copy(x_hbm.at[pl.ds(0, 2 * chunk)], xs_hbm.at[pl.ds(0, 2 * chunk)],
                              sems.at[slot]).wait()

    def issue_chunk(ci, _):
        slot = ci % 2

        def issue(r, _):
            t = ci * chunk + r
            _row_copy(x_hbm, xs_hbm, t, d1_ref[t], sems.at[slot]).start()
            _row_copy(x_hbm, xs_hbm, t, d2_ref[t], sems.at[slot]).start()
            return 0

        lax.fori_loop(0, chunk, issue, 0)

        @pl.when(ci > 0)
        def _():
            wait_chunk(1 - slot)
        return 0

    lax.fori_loop(0, n_chunks, issue_chunk, 0)
    wait_chunk((n_chunks - 1) % 2)


def _group_rows(dest1, dest2, x, n_rows, *, chunk):
    t, d = x.shape
    init = jnp.zeros((n_rows, d), x.dtype)
    return pl.pallas_call(
        functools.partial(_group_rows_kernel, chunk=chunk),
        grid_spec=pltpu.PrefetchScalarGridSpec(
            num_scalar_prefetch=2,
            grid=(1,),
            in_specs=[pl.BlockSpec(memory_space=pl.ANY), pl.BlockSpec(memory_space=pl.ANY)],
            out_specs=pl.BlockSpec(memory_space=pl.ANY),
            scratch_shapes=[pltpu.SemaphoreType.DMA((2,))],
        ),
        out_shape=jax.ShapeDtypeStruct((n_rows, d), x.dtype),
        input_output_aliases={3: 0},
        compiler_params=pltpu.CompilerParams(
            dimension_semantics=("arbitrary",), has_side_effects=True),
        name="moe_group_rows",
    )(dest1, dest2, x, init)


def _expert_kernel(te_ref, tv_ref, x_ref, wg_ref, wu_ref, wd_ref, y_ref, xb_ref):
    del te_ref
    r = pl.program_id(0)
    j = pl.program_id(1)

    @pl.when(j == 0)
    def _():
        xb_ref[...] = x_ref[...].astype(BF16)
        y_ref[...] = jnp.zeros_like(y_ref)

    @pl.when(tv_ref[r] != 0)
    def _():
        x = xb_ref[...]
        a = _silu(_dot(x, wg_ref[...])) * _dot(x, wu_ref[...])
        y_ref[...] += _dot(a.astype(BF16), wd_ref[...])


def _experts(tile_expert, tile_valid, xs, w_gate, w_up, w_down, *, tm, tf):
    n_rows, d = xs.shape
    f = w_gate.shape[2]
    nj = f // tf
    col = lambda r, j, tv: jnp.where(tv[r] != 0, j, nj - 1)
    return pl.pallas_call(
        _expert_kernel,
        grid_spec=pltpu.PrefetchScalarGridSpec(
            num_scalar_prefetch=2,
            grid=(n_rows // tm, nj),
            in_specs=[
                pl.BlockSpec((tm, d), lambda r, j, te, tv: (r, 0)),
                pl.BlockSpec((None, d, tf), lambda r, j, te, tv: (te[r], 0, col(r, j, tv))),
                pl.BlockSpec((None, d, tf), lambda r, j, te, tv: (te[r], 0, col(r, j, tv))),
                pl.BlockSpec((None, tf, d), lambda r, j, te, tv: (te[r], col(r, j, tv), 0)),
            ],
            out_specs=pl.BlockSpec((tm, d), lambda r, j, te, tv: (r, 0)),
            scratch_shapes=[pltpu.VMEM((tm, d), BF16)],
        ),
        out_shape=jax.ShapeDtypeStruct((n_rows, d), F32),
        compiler_params=pltpu.CompilerParams(
            dimension_semantics=("arbitrary", "arbitrary"), vmem_limit_bytes=VMEM_LIMIT),
        name="moe_experts",
    )(tile_expert, tile_valid, xs, w_gate, w_up, w_down)


def _combine_kernel(d1_ref, d2_ref, y_hbm, h_ref, meta_ref, fg_ref, out_ref, a_ref, b_ref, sem):
    tc = h_ref.shape[0]
    base = pl.program_id(0) * tc

    def issue(r, _):
        pltpu.make_async_copy(y_hbm.at[pl.ds(d1_ref[base + r], 1)], a_ref.at[pl.ds(r, 1)],
                              sem).start()
        pltpu.make_async_copy(y_hbm.at[pl.ds(d2_ref[base + r], 1)], b_ref.at[pl.ds(r, 1)],
                              sem).start()
        return 0

    lax.fori_loop(0, tc, issue, 0)
    pltpu.make_async_copy(y_hbm.at[pl.ds(0, tc)], a_ref, sem).wait()
    pltpu.make_async_copy(y_hbm.at[pl.ds(0, tc)], b_ref, sem).wait()
    meta = meta_ref[...]
    w1 = meta[:, 2:3]
    w2 = meta[:, 3:4]
    out_ref[...] = _rms(h_ref[...] + w1 * a_ref[...] + w2 * b_ref[...], fg_ref[...])


def _combine(dest1, dest2, y, h2d, meta, fg, *, tc):
    t, d = h2d.shape
    return pl.pallas_call(
        _combine_kernel,
        grid_spec=pltpu.PrefetchScalarGridSpec(
            num_scalar_prefetch=2,
            grid=(t // tc,),
            in_specs=[
                pl.BlockSpec(memory_space=pl.ANY),
                pl.BlockSpec((tc, d), lambda i, d1, d2: (i, 0)),
                pl.BlockSpec((tc, LANES), lambda i, d1, d2: (i, 0)),
                pl.BlockSpec((1, d), lambda i, d1, d2: (0, 0)),
            ],
            out_specs=pl.BlockSpec((tc, d), lambda i, d1, d2: (i, 0)),
            scratch_shapes=[pltpu.VMEM((tc, d), F32), pltpu.VMEM((tc, d), F32),
                            pltpu.SemaphoreType.DMA(())],
        ),
        out_shape=jax.ShapeDtypeStruct((t, d), F32),
        compiler_params=pltpu.CompilerParams(
            dimension_semantics=("arbitrary",), vmem_limit_bytes=VMEM_LIMIT),
        name="moe_combine",
    )(dest1, dest2, y, h2d, meta, fg)


def _routing_plan(meta, counts, *, tm, n_tiles):
    e1 = meta[:, 0].astype(jnp.int32)
    e2 = meta[:, 1].astype(jnp.int32)
    cnt = counts[0, :N_EXPERTS].astype(jnp.int32)
    padded = (cnt + tm - 1) // tm * tm
    ends = jnp.cumsum(padded)
    offs = ends - padded
    dest1 = offs[e1] + meta[:, 4].astype(jnp.int32)
    dest2 = offs[e2] + meta[:, 5].astype(jnp.int32)
    start = jnp.arange(n_tiles, dtype=jnp.int32) * tm
    tile_valid = (start < ends[-1]).astype(jnp.int32)
    last = jnp.sum((ends < ends[-1]).astype(jnp.int32))
    passed = jnp.sum((ends[None, :] <= start[:, None]).astype(jnp.int32), axis=1)
    return dest1, dest2, jnp.minimum(passed, last), tile_valid


def _pad_heads(w, head_dim):
    k = w.shape[0]
    w = w.reshape(k, MLA_HEADS, head_dim)
    w = jnp.pad(w, ((0, 0), (0, 0), (0, LANES - head_dim)))
    return w.reshape(k, MLA_HEADS * LANES)


def _pick(n, cap):
    t = min(n, cap)
    while n % t:
        t //= 2
    return t


def kernel(x, positions, mix_norm, ffn_norm, final_norm, mla_w_in, mla_q_norm, mla_w_q_up,
           mla_kv_norm, mla_w_kv_up, mla_w_o, hgrn_w_in, hgrn_lower_bounds, hgrn_out_norm,
           hgrn_w_o, ffn_w_gate, ffn_w_up, ffn_w_down, moe_router, moe_w_gate, moe_w_up,
           moe_w_down):
    batch, seq, d = x.shape
    t = batch * seq
    x2d = x.reshape(t, d)
    row = lambda v: v.reshape(1, -1)

    w_in = mla_w_in[0]
    n_lat = MLA_Q_LORA + MLA_KV_LORA
    w_in_p = w_in[:, :n_lat].astype(BF16)
    wkr_t = jnp.pad(w_in[:, n_lat:].T,
                    ((MLA_NOPE, LANES - MLA_NOPE - MLA_ROPE), (0, 0))).astype(BF16)
    wq_t = _pad_heads(mla_w_q_up[0], MLA_NOPE + MLA_ROPE).T.astype(BF16)
    w_kv = mla_w_kv_up[0].reshape(MLA_KV_LORA, MLA_HEADS, MLA_NOPE + MLA_V)
    wk_p = _pad_heads(w_kv[:, :, :MLA_NOPE].reshape(MLA_KV_LORA, -1), MLA_NOPE).astype(BF16)
    wv_t = w_kv[:, :, MLA_NOPE:].reshape(MLA_KV_LORA, -1).T.astype(BF16)
    inv_freq = ROPE_THETA ** (-jnp.arange(0, MLA_ROPE, 2, dtype=F32) / MLA_ROPE)
    freq = inv_freq.reshape(MLA_ROPE // 2, 1)
    pos = positions.astype(F32).reshape(1, t)

    q_t, k, v_t = _mla_proj(x2d, pos, row(mix_norm[0]), w_in_p, wkr_t, row(mla_q_norm[0]), wq_t,
                            row(mla_kv_norm[0]), wk_p, wv_t, freq,
                            batch=batch, seq=seq, tm=_pick(seq, 512))
    o = _mla_attn(q_t, k, v_t, tq=_pick(seq, 512)).reshape(t, MLA_HEADS * MLA_V)
    h = _attn_ffn(x2d, o, mla_w_o[0].astype(BF16), row(ffn_norm[0]),
                  ffn_w_gate[0].astype(BF16), ffn_w_up[0].astype(BF16),
                  ffn_w_down[0].astype(BF16), tm=_pick(t, 1024), tf=896)

    router_p = jnp.pad(moe_router[0], ((0, 0), (0, LANES - N_EXPERTS)))
    h, hn, meta, counts = _hgrn(h, row(mix_norm[1]), hgrn_w_in[0].astype(BF16),
                                hgrn_lower_bounds, row(hgrn_out_norm[0]),
                                hgrn_w_o[0].astype(BF16), row(ffn_norm[1]), router_p,
                                batch=batch, seq=seq, tt=_pick(seq, 256), layer=1)
    tm = _pick(t, 512)
    n_tiles = 2 * t // tm + N_EXPERTS
    dest1, dest2, tile_expert, tile_valid = _routing_plan(meta, counts, tm=tm, n_tiles=n_tiles)
    xs = _group_rows(dest1, dest2, hn, n_tiles * tm, chunk=_pick(t, 512))
    y = _experts(tile_expert, tile_valid, xs, moe_w_gate[0].astype(BF16),
                 moe_w_up[0].astype(BF16), moe_w_down[0].astype(BF16), tm=tm, tf=1792)
    out = _combine(dest1, dest2, y, h, meta, row(final_norm), tc=_pick(t, 512))
    return out.reshape(batch, seq, d)
```

```python
import functools

import jax
import jax.numpy as jnp
from jax import lax
from jax.experimental import pallas as pl
from jax.experimental.pallas import tpu as pltpu

EPS = 1e-6
LANES = 128
SUBLANES = 8

MLA_HEADS = 16
MLA_Q_LORA = 384
MLA_KV_LORA = 256
MLA_NOPE = 64
MLA_ROPE = 32
MLA_V = 64
ROPE_THETA = 10000.0

HG_HEADS = 8
HG_DK = 128
HG_CHUNK = 64

N_EXPERTS = 8

VMEM_LIMIT = 56 * 1024 * 1024

BF16 = jnp.bfloat16
F32 = jnp.float32


def _dot(a, b):
    return jnp.dot(a, b, preferred_element_type=F32)


def _dot_nt(a, b):
    return lax.dot_general(a, b, (((1,), (1,)), ((), ())), preferred_element_type=F32)


def _dot_tn(a, b):
    return lax.dot_general(a, b, (((0,), (0,)), ((), ())), preferred_element_type=F32)


def _rms(x, g):
    return x * lax.rsqrt(jnp.mean(x * x, axis=-1, keepdims=True) + EPS) * g


def _silu(x):
    return x * (1.0 / (1.0 + jnp.exp(-x)))


def _split3(x):
    hi = x.astype(BF16)
    r = x - hi.astype(F32)
    mid = r.astype(BF16)
    lo = (r - mid.astype(F32)).astype(BF16)
    return hi, mid, lo


def _mla_proj_kernel(x_ref, pos_ref, g_ref, win_ref, wkr_ref, qg_ref, wq_ref, kvg_ref, wk_ref,
                     wv_ref, freq_ref, q_out, k_out, v_out, *, scale):
    half = MLA_ROPE // 2
    x1_rows = slice(MLA_NOPE, MLA_NOPE + half)
    x2_rows = slice(MLA_NOPE + half, MLA_NOPE + MLA_ROPE)
    hn = _rms(x_ref[...], g_ref[...]).astype(BF16)
    proj = _dot(hn, win_ref[...])
    cqn = _rms(proj[:, :MLA_Q_LORA], qg_ref[...]).astype(BF16)
    ckvn = _rms(proj[:, MLA_Q_LORA:], kvg_ref[...]).astype(BF16)

    ang = freq_ref[...] * pos_ref[...]
    cos = jnp.cos(ang)
    sin = jnp.sin(ang)

    def rope_t(slot, mul):
        x1 = slot[x1_rows]
        x2 = slot[x2_rows]
        return jnp.concatenate([slot[:MLA_NOPE] * mul, (x1 * cos - x2 * sin) * mul,
                                (x2 * cos + x1 * sin) * mul, slot[MLA_NOPE + MLA_ROPE:]], axis=0)

    kr_t = _dot_nt(wkr_ref[...], hn)
    kr_roped = rope_t(kr_t, 1.0).T
    q_t = _dot_nt(wq_ref[...], cqn)
    k_pad = _dot(ckvn, wk_ref[...])
    for h in range(MLA_HEADS):
        sl = slice(h * LANES, (h + 1) * LANES)
        q_out[h] = rope_t(q_t[sl], scale).astype(BF16)
        k_out[h] = (k_pad[:, sl] + kr_roped).astype(BF16)
    v_out[...] = _dot_nt(wv_ref[...], ckvn).astype(BF16)


def _mla_proj(x2d, pos, g, w_in_p, wkr_t, qg, wq_t, kvg, wk_p, wv_t, freq, *, batch, seq, tm):
    t = x2d.shape[0]
    d = x2d.shape[1]
    nt = seq // tm
    const = lambda shape: pl.BlockSpec(shape, lambda i: (0,) * len(shape))
    scale = (MLA_NOPE + MLA_ROPE) ** -0.5 * 1.4426950408889634
    return pl.pallas_call(
        functools.partial(_mla_proj_kernel, scale=scale),
        grid=(t // tm,),
        in_specs=[
            pl.BlockSpec((tm, d), lambda i: (i, 0)),
            pl.BlockSpec((1, tm), lambda i: (0, i)),
            const((1, d)),
            const(w_in_p.shape),
            const(wkr_t.shape),
            const((1, MLA_Q_LORA)),
            const(wq_t.shape),
            const((1, MLA_KV_LORA)),
            const(wk_p.shape),
            const(wv_t.shape),
            const(freq.shape),
        ],
        out_specs=[
            pl.BlockSpec((None, MLA_HEADS, LANES, tm), lambda i: (i // nt, 0, 0, i % nt)),
            pl.BlockSpec((None, MLA_HEADS, tm, LANES), lambda i: (i // nt, 0, i % nt, 0)),
            pl.BlockSpec((None, MLA_HEADS * MLA_V, tm), lambda i: (i // nt, 0, i % nt)),
        ],
        out_shape=[
            jax.ShapeDtypeStruct((batch, MLA_HEADS, LANES, seq), BF16),
            jax.ShapeDtypeStruct((batch, MLA_HEADS, seq, LANES), BF16),
            jax.ShapeDtypeStruct((batch, MLA_HEADS * MLA_V, seq), BF16),
        ],
        compiler_params=pltpu.CompilerParams(
            dimension_semantics=("parallel",), vmem_limit_bytes=VMEM_LIMIT),
        name="mla_proj",
    )(x2d, pos, g, w_in_p, wkr_t, qg, wq_t, kvg, wk_p, wv_t, freq)


def _attn_kernel(q_ref, k_ref, v_ref, o_ref, acc_ref, *, tq):
    seq = k_ref.shape[1]
    nq = seq // tq
    causal = (lax.broadcasted_iota(jnp.int32, (tq, tq), 0)
              <= lax.broadcasted_iota(jnp.int32, (tq, tq), 1))

    def step(q_t, h, k0, carry, masked):
        m, l = carry
        s = _dot(k_ref[h, pl.ds(k0, tq), :], q_t)
        if masked:
            s = jnp.where(causal, s, -jnp.inf)
        m_new = jnp.maximum(m, jnp.max(s, axis=0, keepdims=True))
        p = jnp.exp2(s - m_new)
        alpha = jnp.exp2(m - m_new)
        l = alpha * l + jnp.sum(p, axis=0, keepdims=True)
        v_t = v_ref[h * MLA_V:(h + 1) * MLA_V, pl.ds(k0, tq)]
        acc_ref[h] = alpha * acc_ref[h] + _dot(v_t, p.astype(BF16))
        return m_new, l

    def q_block(qi, _):
        q0 = pl.multiple_of(qi * tq, tq)
        qs = [q_ref[h, :, pl.ds(q0, tq)] for h in range(2)]
        acc_ref[...] = jnp.zeros_like(acc_ref)
        init = tuple((jnp.full((1, tq), -jnp.inf, F32), jnp.zeros((1, tq), F32))
                     for _ in range(2))

        def k_block(ki, carry):
            k0 = pl.multiple_of(ki * tq, tq)
            return tuple(step(qs[h], h, k0, carry[h], False) for h in range(2))

        carry = lax.fori_loop(0, qi, k_block, init)
        outs = []
        for h in range(2):
            _, l = step(qs[h], h, q0, carry[h], True)
            outs.append(acc_ref[h] / l)
        o_ref[pl.ds(q0, tq), :] = jnp.concatenate(outs, axis=0).T.astype(BF16)
        return 0

    lax.fori_loop(0, nq, q_block, 0)


def _mla_attn(q_t, k, v_t, *, tq):
    batch, heads, seq, _ = k.shape
    return pl.pallas_call(
        functools.partial(_attn_kernel, tq=tq),
        grid=(batch, heads // 2),
        in_specs=[
            pl.BlockSpec((None, 2, LANES, seq), lambda b, hp: (b, hp, 0, 0)),
            pl.BlockSpec((None, 2, seq, LANES), lambda b, hp: (b, hp, 0, 0)),
            pl.BlockSpec((None, 2 * MLA_V, seq), lambda b, hp: (b, hp, 0)),
        ],
        out_specs=pl.BlockSpec((None, seq, 2 * MLA_V), lambda b, hp: (b, 0, hp)),
        out_shape=jax.ShapeDtypeStruct((batch, seq, heads * MLA_V), BF16),
        scratch_shapes=[pltpu.VMEM((2, MLA_V, tq), F32)],
        compiler_params=pltpu.CompilerParams(
            dimension_semantics=("parallel", "parallel"), vmem_limit_bytes=VMEM_LIMIT),
        name="mla_attn",
    )(q_t, k, v_t)


def _attn_ffn_kernel(x_ref, o_ref, wo_ref, g_ref, wg_ref, wu_ref, wd_ref, out_ref, hn_ref):
    j = pl.program_id(1)

    @pl.when(j == 0)
    def _():
        h1 = x_ref[...] + _dot(o_ref[...], wo_ref[...])
        out_ref[...] = h1
        hn_ref[...] = _rms(h1, g_ref[...]).astype(BF16)

    hn = hn_ref[...]
    a = _silu(_dot(hn, wg_ref[...])) * _dot(hn, wu_ref[...])
    out_ref[...] += _dot(a.astype(BF16), wd_ref[...])


def _attn_ffn(x2d, o, w_o, g, w_gate, w_up, w_down, *, tm, tf):
    t, d = x2d.shape
    f = w_gate.shape[1]
    return pl.pallas_call(
        _attn_ffn_kernel,
        grid=(t // tm, f // tf),
        in_specs=[
            pl.BlockSpec((tm, d), lambda i, j: (i, 0)),
            pl.BlockSpec((tm, o.shape[1]), lambda i, j: (i, 0)),
            pl.BlockSpec(w_o.shape, lambda i, j: (0, 0)),
            pl.BlockSpec((1, d), lambda i, j: (0, 0)),
            pl.BlockSpec((d, tf), lambda i, j: (0, j)),
            pl.BlockSpec((d, tf), lambda i, j: (0, j)),
            pl.BlockSpec((tf, d), lambda i, j: (j, 0)),
        ],
        out_specs=pl.BlockSpec((tm, d), lambda i, j: (i, 0)),
        out_shape=jax.ShapeDtypeStruct((t, d), F32),
        scratch_shapes=[pltpu.VMEM((tm, d), BF16)],
        compiler_params=pltpu.CompilerParams(
            dimension_semantics=("parallel", "arbitrary"), vmem_limit_bytes=VMEM_LIMIT),
        name="attn_ffn",
    )(x2d, o, w_o, g, w_gate, w_up, w_down)


def _hgrn_kernel(h_ref, g_ref, win_ref, lbraw_ref, og_ref, wo_ref, fg_ref, router_ref,
                 h_out, hn_out, meta_out, cnt_out, st_ref, o_scr, cnt_ref, *, layer):
    tt = h_ref.shape[0]
    width = HG_HEADS * HG_DK
    nchunk = tt // HG_CHUNK

    @pl.when(pl.program_id(1) == 0)
    def _():
        st_ref[...] = jnp.zeros_like(st_ref)

    lbraw = lbraw_ref[...]
    e = jnp.exp(lbraw - jnp.max(lbraw, axis=0, keepdims=True))
    sm = e / jnp.sum(e, axis=0, keepdims=True)
    lb = jnp.sum(sm[1:layer + 1], axis=0, keepdims=True)

    h_in = h_ref[...]
    hn = _rms(h_in, g_ref[...]).astype(BF16)
    proj = _dot(hn, win_ref[...])
    q_all = proj[:, :width] * (HG_DK ** -0.5)
    f_all = lb + (1.0 - lb) * (1.0 / (1.0 + jnp.exp(-proj[:, width:2 * width])))
    logf = jnp.log(f_all)
    k_all = 1.0 - f_all

    r = lax.broadcasted_iota(jnp.int32, (tt, tt), 0)
    c = lax.broadcasted_iota(jnp.int32, (tt, tt), 1)
    shift = HG_CHUNK.bit_length() - 1
    tril = (c <= r) & ((r >> shift) == (c >> shift))
    tril_b = jnp.where(tril, 1.0, 0.0).astype(BF16)
    hi, mid, lo = _split3(logf)
    b_all = _dot(tril_b, hi) + _dot(tril_b, mid) + _dot(tril_b, lo)

    for h in range(HG_HEADS):
        sl = slice(h * HG_DK, (h + 1) * HG_DK)
        q = q_all[:, sl]
        k = k_all[:, sl]
        b = b_all[:, sl]
        v = proj[:, 2 * width + h * HG_DK:2 * width + (h + 1) * HG_DK]
        gt = proj[:, 3 * width + h * HG_DK:3 * width + (h + 1) * HG_DK]
        v_b = v.astype(BF16)
        qd = (q * jnp.exp(b)).astype(BF16)
        kd = (k * jnp.exp(-b)).astype(BF16)
        a = jnp.where(tril, _dot_nt(qd, kd), 0.0)
        o = _dot(a.astype(BF16), v_b)

        st = st_ref[h]
        inter = []
        for n in range(nchunk):
            rows = slice(n * HG_CHUNK, (n + 1) * HG_CHUNK)
            b_n = b[rows]
            b_last = b_n[HG_CHUNK - 1:HG_CHUNK]
            inter.append(_dot_nt(qd[rows], st.astype(BF16)))
            kdl = (k[rows] * jnp.exp(b_last - b_n)).astype(BF16)
            st = st * jnp.exp(b_last) + _dot_tn(v_b[rows], kdl)
        st_ref[h] = st
        o = o + jnp.concatenate(inter, axis=0)
        o = _rms(o, og_ref[...]) * _silu(gt)
        o_scr[:, sl] = o.astype(BF16)

    h_new = h_in + _dot(o_scr[...], wo_ref[...])
    h_out[...] = h_new
    hn2 = _rms(h_new, fg_ref[...])
    hn_out[...] = hn2

    x_hi, x_mid, _ = _split3(hn2)
    router = router_ref[...]
    r_hi = router.astype(BF16)
    r_lo = (router - r_hi.astype(F32)).astype(BF16)
    logits = _dot(x_hi, r_hi) + _dot(x_mid, r_hi) + _dot(x_hi, r_lo)
    lane = lax.broadcasted_iota(jnp.int32, logits.shape, 1)
    neg = -jnp.inf
    logits = jnp.where(lane < N_EXPERTS, logits, neg)
    m1 = jnp.max(logits, axis=-1, keepdims=True)
    i1 = jnp.min(jnp.where(logits == m1, lane, LANES), axis=-1, keepdims=True)
    rest = jnp.where(lane == i1, neg, logits)
    m2 = jnp.max(rest, axis=-1, keepdims=True)
    i2 = jnp.min(jnp.where(rest == m2, lane, LANES), axis=-1, keepdims=True)
    e2 = jnp.exp(m2 - m1)
    w1 = 1.0 / (1.0 + e2)
    w2 = e2 / (1.0 + e2)

    @pl.when((pl.program_id(0) == 0) & (pl.program_id(1) == 0))
    def _():
        cnt_ref[...] = jnp.zeros_like(cnt_ref)

    onehot = jnp.where((lane == i1) | (lane == i2), 1.0, 0.0)
    earlier = jnp.where(c < r, 1.0, 0.0).astype(BF16)
    rank = _dot(earlier, onehot.astype(BF16)) + cnt_ref[...]
    cnt = cnt_ref[...] + jnp.sum(onehot, axis=0, keepdims=True)
    cnt_ref[...] = cnt
    cnt_out[...] = jnp.broadcast_to(cnt, cnt_out.shape)
    rank1 = jnp.sum(jnp.where(lane == i1, rank, 0.0), axis=-1, keepdims=True)
    rank2 = jnp.sum(jnp.where(lane == i2, rank, 0.0), axis=-1, keepdims=True)
    fields = (i1.astype(F32), i2.astype(F32), w1, w2, rank1, rank2)
    meta = jnp.zeros(logits.shape, F32)
    for idx, val in enumerate(fields):
        meta = jnp.where(lane == idx, val, meta)
    meta_out[...] = meta


def _hgrn(h2d, g, w_in, lb_raw, og, w_o, fg, router_p, *, batch, seq, tt, layer):
    t, d = h2d.shape
    nt = seq // tt
    width = HG_HEADS * HG_DK
    tok = lambda cols: pl.BlockSpec((tt, cols), lambda b, i: (b * nt + i, 0))
    const = lambda shape: pl.BlockSpec(shape, lambda b, i: (0,) * len(shape))
    return pl.pallas_call(
        functools.partial(_hgrn_kernel, layer=layer),
        grid=(batch, nt),
        in_specs=[tok(d), const((1, d)), const(w_in.shape), const(lb_raw.shape),
                  const((1, HG_DK)), const(w_o.shape), const((1, d)), const(router_p.shape)],
        out_specs=[tok(d), tok(d), tok(LANES), const((SUBLANES, LANES))],
        out_shape=[jax.ShapeDtypeStruct((t, d), F32),
                   jax.ShapeDtypeStruct((t, d), F32),
                   jax.ShapeDtypeStruct((t, LANES), F32),
                   jax.ShapeDtypeStruct((SUBLANES, LANES), F32)],
        scratch_shapes=[pltpu.VMEM((HG_HEADS, HG_DK, HG_DK), F32),
                        pltpu.VMEM((tt, width), BF16),
                        pltpu.VMEM((1, LANES), F32)],
        compiler_params=pltpu.CompilerParams(
            dimension_semantics=("arbitrary", "arbitrary"), vmem_limit_bytes=VMEM_LIMIT),
        name="hgrn",
    )(h2d, g, w_in, lb_raw, og, w_o, fg, router_p)


def _expert_kernel(src_ref, te_ref, tv_ref, x_hbm, wg_ref, wu_ref, wd_ref, y_ref,
                   xbuf, xb_ref, sems):
    del te_ref
    r = pl.program_id(0)
    j = pl.program_id(1)
    n_tiles = pl.num_programs(0)
    nj = pl.num_programs(1)
    tm = xb_ref.shape[0]
    part = tm // nj

    def gather_wait(slot):
        pltpu.make_async_copy(x_hbm.at[pl.ds(0, tm)], xbuf.at[slot], sems.at[slot]).wait()

    def gather_issue(tile, first, count):
        slot = tile % 2

        def issue(i, _):
            row = first + i
            pltpu.make_async_copy(x_hbm.at[pl.ds(src_ref[tile * tm + row], 1)],
                                  xbuf.at[slot, pl.ds(row, 1)], sems.at[slot]).start()
            return 0

        lax.fori_loop(0, count, issue, 0)

    @pl.when((r == 0) & (j == 0))
    def _():
        gather_issue(r, 0, tm)

    @pl.when(r + 1 < n_tiles)
    def _():
        gather_issue(r + 1, j * part, part)

    @pl.when(j == 0)
    def _():
        gather_wait(r % 2)
        xb_ref[...] = xbuf[r % 2].astype(BF16)
        y_ref[...] = jnp.zeros_like(y_ref)

    @pl.when(tv_ref[r] != 0)
    def _():
        x = xb_ref[...]
        a = _silu(_dot(x, wg_ref[...])) * _dot(x, wu_ref[...])
        y_ref[...] += _dot(a.astype(BF16), wd_ref[...])


def _experts(src_rows, tile_expert, tile_valid, x, w_gate, w_up, w_down, *, tm, tf):
    n_rows = src_rows.shape[0]
    d = x.shape[1]
    f = w_gate.shape[2]
    nj = f // tf
    col = lambda r, j, tv: jnp.where(tv[r] != 0, j, nj - 1)
    return pl.pallas_call(
        _expert_kernel,
        grid_spec=pltpu.PrefetchScalarGridSpec(
            num_scalar_prefetch=3,
            grid=(n_rows // tm, nj),
            in_specs=[
                pl.BlockSpec(memory_space=pl.ANY),
                pl.BlockSpec((None, d, tf), lambda r, j, s, te, tv: (te[r], 0, col(r, j, tv))),
                pl.BlockSpec((None, d, tf), lambda r, j, s, te, tv: (te[r], 0, col(r, j, tv))),
                pl.BlockSpec((None, tf, d), lambda r, j, s, te, tv: (te[r], col(r, j, tv), 0)),
            ],
            out_specs=pl.BlockSpec((tm, d), lambda r, j, s, te, tv: (r, 0)),
            scratch_shapes=[pltpu.VMEM((2, tm, d), F32), pltpu.VMEM((tm, d), BF16),
                            pltpu.SemaphoreType.DMA((2,))],
        ),
        out_shape=jax.ShapeDtypeStruct((n_rows, d), F32),
        compiler_params=pltpu.CompilerParams(
            dimension_semantics=("arbitrary", "arbitrary"), vmem_limit_bytes=VMEM_LIMIT),
        name="moe_experts",
    )(src_rows, tile_expert, tile_valid, x, w_gate, w_up, w_down)


def _combine_kernel(d1_ref, d2_ref, y_hbm, h_ref, meta_ref, fg_ref, out_ref, a_ref, b_ref, sem):
    tc = h_ref.shape[0]
    base = pl.program_id(0) * tc

    def issue(r, _):
        pltpu.make_async_copy(y_hbm.at[pl.ds(d1_ref[base + r], 1)], a_ref.at[pl.ds(r, 1)],
                              sem).start()
        pltpu.make_async_copy(y_hbm.at[pl.ds(d2_ref[base + r], 1)], b_ref.at[pl.ds(r, 1)],
                              sem).start()
        return 0

    lax.fori_loop(0, tc, issue, 0)
    pltpu.make_async_copy(y_hbm.at[pl.ds(0, tc)], a_ref, sem).wait()
    pltpu.make_async_copy(y_hbm.at[pl.ds(0, tc)], b_ref, sem).wait()
    meta = meta_ref[...]
    w1 = meta[:, 2:3]
    w2 = meta[:, 3:4]
    out_ref[...] = _rms(h_ref[...] + w1 * a_ref[...] + w2 * b_ref[...], fg_ref[...])


def _combine(dest1, dest2, y, h2d, meta, fg, *, tc):
    t, d = h2d.shape
    return pl.pallas_call(
        _combine_kernel,
        grid_spec=pltpu.PrefetchScalarGridSpec(
            num_scalar_prefetch=2,
            grid=(t // tc,),
            in_specs=[
                pl.BlockSpec(memory_space=pl.ANY),
                pl.BlockSpec((tc, d), lambda i, d1, d2: (i, 0)),
                pl.BlockSpec((tc, LANES), lambda i, d1, d2: (i, 0)),
                pl.BlockSpec((1, d), lambda i, d1, d2: (0, 0)),
            ],
            out_specs=pl.BlockSpec((tc, d), lambda i, d1, d2: (i, 0)),
            scratch_shapes=[pltpu.VMEM((tc, d), F32), pltpu.VMEM((tc, d), F32),
                            pltpu.SemaphoreType.DMA(())],
        ),
        out_shape=jax.ShapeDtypeStruct((t, d), F32),
        compiler_params=pltpu.CompilerParams(
            dimension_semantics=("arbitrary",), vmem_limit_bytes=VMEM_LIMIT),
        name="moe_combine",
    )(dest1, dest2, y, h2d, meta, fg)


def _routing_plan(meta, counts, *, tm, n_tiles):
    e1 = meta[:, 0].astype(jnp.int32)
    e2 = meta[:, 1].astype(jnp.int32)
    cnt = counts[0, :N_EXPERTS].astype(jnp.int32)
    padded = (cnt + tm - 1) // tm * tm
    ends = jnp.cumsum(padded)
    offs = ends - padded
    dest1 = offs[e1] + meta[:, 4].astype(jnp.int32)
    dest2 = offs[e2] + meta[:, 5].astype(jnp.int32)
    start = jnp.arange(n_tiles, dtype=jnp.int32) * tm
    tile_valid = (start < ends[-1]).astype(jnp.int32)
    last = jnp.sum((ends < ends[-1]).astype(jnp.int32))
    passed = jnp.sum((ends[None, :] <= start[:, None]).astype(jnp.int32), axis=1)
    tok = jnp.arange(meta.shape[0], dtype=jnp.int32)
    src_rows = jnp.zeros((n_tiles * tm,), jnp.int32).at[dest1].set(tok).at[dest2].set(tok)
    return dest1, dest2, src_rows, jnp.minimum(passed, last), tile_valid


def _pad_heads(w, head_dim):
    k = w.shape[0]
    w = w.reshape(k, MLA_HEADS, head_dim)
    w = jnp.pad(w, ((0, 0), (0, 0), (0, LANES - head_dim)))
    return w.reshape(k, MLA_HEADS * LANES)


def _pick(n, cap):
    t = min(n, cap)
    while n % t:
        t //= 2
    return t


def kernel(x, positions, mix_norm, ffn_norm, final_norm, mla_w_in, mla_q_norm, mla_w_q_up,
           mla_kv_norm, mla_w_kv_up, mla_w_o, hgrn_w_in, hgrn_lower_bounds, hgrn_out_norm,
           hgrn_w_o, ffn_w_gate, ffn_w_up, ffn_w_down, moe_router, moe_w_gate, moe_w_up,
           moe_w_down):
    batch, seq, d = x.shape
    t = batch * seq
    x2d = x.reshape(t, d)
    row = lambda v: v.reshape(1, -1)

    w_in = mla_w_in[0]
    n_lat = MLA_Q_LORA + MLA_KV_LORA
    w_in_p = w_in[:, :n_lat].astype(BF16)
    wkr_t = jnp.pad(w_in[:, n_lat:].T,
                    ((MLA_NOPE, LANES - MLA_NOPE - MLA_ROPE), (0, 0))).astype(BF16)
    wq_t = _pad_heads(mla_w_q_up[0], MLA_NOPE + MLA_ROPE).T.astype(BF16)
    w_kv = mla_w_kv_up[0].reshape(MLA_KV_LORA, MLA_HEADS, MLA_NOPE + MLA_V)
    wk_p = _pad_heads(w_kv[:, :, :MLA_NOPE].reshape(MLA_KV_LORA, -1), MLA_NOPE).astype(BF16)
    wv_t = w_kv[:, :, MLA_NOPE:].reshape(MLA_KV_LORA, -1).T.astype(BF16)
    inv_freq = ROPE_THETA ** (-jnp.arange(0, MLA_ROPE, 2, dtype=F32) / MLA_ROPE)
    freq = inv_freq.reshape(MLA_ROPE // 2, 1)
    pos = positions.astype(F32).reshape(1, t)

    q_t, k, v_t = _mla_proj(x2d, pos, row(mix_norm[0]), w_in_p, wkr_t, row(mla_q_norm[0]), wq_t,
                            row(mla_kv_norm[0]), wk_p, wv_t, freq,
                            batch=batch, seq=seq, tm=_pick(seq, 512))
    o = _mla_attn(q_t, k, v_t, tq=_pick(seq, 512)).reshape(t, MLA_HEADS * MLA_V)
    h = _attn_ffn(x2d, o, mla_w_o[0].astype(BF16), row(ffn_norm[0]),
                  ffn_w_gate[0].astype(BF16), ffn_w_up[0].astype(BF16),
                  ffn_w_down[0].astype(BF16), tm=_pick(t, 1024), tf=896)

    router_p = jnp.pad(moe_router[0], ((0, 0), (0, LANES - N_EXPERTS)))
    h, hn, meta, counts = _hgrn(h, row(mix_norm[1]), hgrn_w_in[0].astype(BF16),
                                hgrn_lower_bounds, row(hgrn_out_norm[0]),
                                hgrn_w_o[0].astype(BF16), row(ffn_norm[1]), router_p,
                                batch=batch, seq=seq, tt=_pick(seq, 256), layer=1)
    tm = _pick(t, 512)
    n_tiles = 2 * t // tm + N_EXPERTS
    dest1, dest2, src_rows, tile_expert, tile_valid = _routing_plan(
        meta, counts, tm=tm, n_tiles=n_tiles)
    y = _experts(src_rows, tile_expert, tile_valid, hn, moe_w_gate[0].astype(BF16),
                 moe_w_up[0].astype(BF16), moe_w_down[0].astype(BF16), tm=tm, tf=1792)
    out = _combine(dest1, dest2, y, h, meta, row(final_norm), tc=_pick(t, 512))
    return out.reshape(batch, seq, d)
```

```python
import functools

import jax
import jax.numpy as jnp
from jax import lax
from jax.experimental import pallas as pl
from jax.experimental.pallas import tpu as pltpu

EPS = 1e-6
LANES = 128
SUBLANES = 8

MLA_HEADS = 16
MLA_Q_LORA = 384
MLA_KV_LORA = 256
MLA_NOPE = 64
MLA_ROPE = 32
MLA_V = 64
ROPE_THETA = 10000.0

HG_HEADS = 8
HG_DK = 128
HG_CHUNK = 64

N_EXPERTS = 8

VMEM_LIMIT = 56 * 1024 * 1024

BF16 = jnp.bfloat16
F32 = jnp.float32


def _dot(a, b):
    return jnp.dot(a, b, preferred_element_type=F32)


def _dot_nt(a, b):
    return lax.dot_general(a, b, (((1,), (1,)), ((), ())), preferred_element_type=F32)


def _dot_tn(a, b):
    return lax.dot_general(a, b, (((0,), (0,)), ((), ())), preferred_element_type=F32)


def _rms(x, g):
    return x * lax.rsqrt(jnp.mean(x * x, axis=-1, keepdims=True) + EPS) * g


def _silu(x):
    return x * (1.0 / (1.0 + jnp.exp(-x)))


def _split3(x):
    hi = x.astype(BF16)
    r = x - hi.astype(F32)
    mid = r.astype(BF16)
    lo = (r - mid.astype(F32)).astype(BF16)
    return hi, mid, lo


def _mla_proj_kernel(x_ref, pos_ref, g_ref, win_ref, wkr_ref, qg_ref, wq_ref, kvg_ref, wk_ref,
                     wv_ref, freq_ref, q_out, k_out, v_out, *, scale):
    half = MLA_ROPE // 2
    x1_rows = slice(MLA_NOPE, MLA_NOPE + half)
    x2_rows = slice(MLA_NOPE + half, MLA_NOPE + MLA_ROPE)
    hn = _rms(x_ref[...], g_ref[...]).astype(BF16)
    proj = _dot(hn, win_ref[...])
    cqn = _rms(proj[:, :MLA_Q_LORA], qg_ref[...]).astype(BF16)
    ckvn = _rms(proj[:, MLA_Q_LORA:], kvg_ref[...]).astype(BF16)

    ang = freq_ref[...] * pos_ref[...]
    cos = jnp.cos(ang)
    sin = jnp.sin(ang)

    def rope_t(slot, mul):
        x1 = slot[x1_rows]
        x2 = slot[x2_rows]
        return jnp.concatenate([slot[:MLA_NOPE] * mul, (x1 * cos - x2 * sin) * mul,
                                (x2 * cos + x1 * sin) * mul, slot[MLA_NOPE + MLA_ROPE:]], axis=0)

    kr_t = _dot_nt(wkr_ref[...], hn)
    kr_roped = rope_t(kr_t, 1.0).T
    q_t = _dot_nt(wq_ref[...], cqn)
    k_pad = _dot(ckvn, wk_ref[...])
    for h in range(MLA_HEADS):
        sl = slice(h * LANES, (h + 1) * LANES)
        q_out[h] = rope_t(q_t[sl], scale).astype(BF16)
        k_out[h] = (k_pad[:, sl] + kr_roped).astype(BF16)
    v_out[...] = _dot_nt(wv_ref[...], ckvn).astype(BF16)


def _mla_proj(x2d, pos, g, w_in_p, wkr_t, qg, wq_t, kvg, wk_p, wv_t, freq, *, batch, seq, tm):
    t = x2d.shape[0]
    d = x2d.shape[1]
    nt = seq // tm
    const = lambda shape: pl.BlockSpec(shape, lambda i: (0,) * len(shape))
    scale = (MLA_NOPE + MLA_ROPE) ** -0.5 * 1.4426950408889634
    return pl.pallas_call(
        functools.partial(_mla_proj_kernel, scale=scale),
        grid=(t // tm,),
        in_specs=[
            pl.BlockSpec((tm, d), lambda i: (i, 0)),
            pl.BlockSpec((1, tm), lambda i: (0, i)),
            const((1, d)),
            const(w_in_p.shape),
            const(wkr_t.shape),
            const((1, MLA_Q_LORA)),
            const(wq_t.shape),
            const((1, MLA_KV_LORA)),
            const(wk_p.shape),
            const(wv_t.shape),
            const(freq.shape),
        ],
        out_specs=[
            pl.BlockSpec((None, MLA_HEADS, LANES, tm), lambda i: (i // nt, 0, 0, i % nt)),
            pl.BlockSpec((None, MLA_HEADS, tm, LANES), lambda i: (i // nt, 0, i % nt, 0)),
            pl.BlockSpec((None, MLA_HEADS * MLA_V, tm), lambda i: (i // nt, 0, i % nt)),
        ],
        out_shape=[
            jax.ShapeDtypeStruct((batch, MLA_HEADS, LANES, seq), BF16),
            jax.ShapeDtypeStruct((batch, MLA_HEADS, seq, LANES), BF16),
            jax.ShapeDtypeStruct((batch, MLA_HEADS * MLA_V, seq), BF16),
        ],
        compiler_params=pltpu.CompilerParams(
            dimension_semantics=("parallel",), vmem_limit_bytes=VMEM_LIMIT),
        name="mla_proj",
    )(x2d, pos, g, w_in_p, wkr_t, qg, wq_t, kvg, wk_p, wv_t, freq)


def _attn_kernel(q_ref, k_ref, v_ref, o_ref, acc_ref, m_ref, l_ref, sa_ref, sb_ref, *, tq):
    seq = k_ref.shape[1]
    nq = seq // tq
    causal = (lax.broadcasted_iota(jnp.int32, (tq, tq), 0)
              <= lax.broadcasted_iota(jnp.int32, (tq, tq), 1))

    def scores(dst, blk, q0):
        k0 = pl.multiple_of(blk * tq, tq)
        for h in range(2):
            dst[h] = _dot(k_ref[h, pl.ds(k0, tq), :], q_ref[h, :, pl.ds(q0, tq)])

    def softmax_pv(src, blk, masked):
        k0 = pl.multiple_of(blk * tq, tq)
        for h in range(2):
            s = src[h]
            if masked:
                s = jnp.where(causal, s, -jnp.inf)
            m = m_ref[h]
            m_new = jnp.maximum(m, jnp.max(s, axis=0, keepdims=True))
            p = jnp.exp2(s - m_new)
            alpha = jnp.exp2(m - m_new)
            l_ref[h] = alpha * l_ref[h] + jnp.sum(p, axis=0, keepdims=True)
            m_ref[h] = m_new
            v_t = v_ref[h * MLA_V:(h + 1) * MLA_V, pl.ds(k0, tq)]
            acc_ref[h] = alpha * acc_ref[h] + _dot(v_t, p.astype(BF16))

    def q_block(qi, _):
        q0 = pl.multiple_of(qi * tq, tq)
        acc_ref[...] = jnp.zeros_like(acc_ref)
        m_ref[...] = jnp.full_like(m_ref, -jnp.inf)
        l_ref[...] = jnp.zeros_like(l_ref)

        scores(sa_ref, 0, q0)

        def pair(jp, _):
            scores(sb_ref, 2 * jp + 1, q0)
            softmax_pv(sa_ref, 2 * jp, False)
            scores(sa_ref, 2 * jp + 2, q0)
            softmax_pv(sb_ref, 2 * jp + 1, False)
            return 0

        lax.fori_loop(0, qi // 2, pair, 0)

        @pl.when(qi % 2 == 1)
        def _():
            scores(sb_ref, qi, q0)
            softmax_pv(sa_ref, qi - 1, False)
            softmax_pv(sb_ref, qi, True)

        @pl.when(qi % 2 == 0)
        def _():
            softmax_pv(sa_ref, qi, True)

        out = jnp.concatenate([acc_ref[h] / l_ref[h] for h in range(2)], axis=0)
        o_ref[pl.ds(q0, tq), :] = out.T.astype(BF16)
        return 0

    lax.fori_loop(0, nq, q_block, 0)


def _mla_attn(q_t, k, v_t, *, tq):
    batch, heads, seq, _ = k.shape
    return pl.pallas_call(
        functools.partial(_attn_kernel, tq=tq),
        grid=(batch, heads // 2),
        in_specs=[
            pl.BlockSpec((None, 2, LANES, seq), lambda b, hp: (b, hp, 0, 0)),
            pl.BlockSpec((None, 2, seq, LANES), lambda b, hp: (b, hp, 0, 0)),
            pl.BlockSpec((None, 2 * MLA_V, seq), lambda b, hp: (b, hp, 0)),
        ],
        out_specs=pl.BlockSpec((None, seq, 2 * MLA_V), lambda b, hp: (b, 0, hp)),
        out_shape=jax.ShapeDtypeStruct((batch, seq, heads * MLA_V), BF16),
        scratch_shapes=[pltpu.VMEM((2, MLA_V, tq), F32),
                        pltpu.VMEM((2, 1, tq), F32), pltpu.VMEM((2, 1, tq), F32),
                        pltpu.VMEM((2, tq, tq), F32), pltpu.VMEM((2, tq, tq), F32)],
        compiler_params=pltpu.CompilerParams(
            dimension_semantics=("parallel", "parallel"), vmem_limit_bytes=VMEM_LIMIT),
        name="mla_attn",
    )(q_t, k, v_t)


def _attn_ffn_kernel(x_ref, o_ref, wo_ref, g_ref, wg_ref, wu_ref, wd_ref, out_ref, hn_ref):
    j = pl.program_id(1)

    @pl.when(j == 0)
    def _():
        h1 = x_ref[...] + _dot(o_ref[...], wo_ref[...])
        out_ref[...] = h1
        hn_ref[...] = _rms(h1, g_ref[...]).astype(BF16)

    hn = hn_ref[...]
    a = _silu(_dot(hn, wg_ref[...])) * _dot(hn, wu_ref[...])
    out_ref[...] += _dot(a.astype(BF16), wd_ref[...])


def _attn_ffn(x2d, o, w_o, g, w_gate, w_up, w_down, *, tm, tf):
    t, d = x2d.shape
    f = w_gate.shape[1]
    return pl.pallas_call(
        _attn_ffn_kernel,
        grid=(t // tm, f // tf),
        in_specs=[
            pl.BlockSpec((tm, d), lambda i, j: (i, 0)),
            pl.BlockSpec((tm, o.shape[1]), lambda i, j: (i, 0)),
            pl.BlockSpec(w_o.shape, lambda i, j: (0, 0)),
            pl.BlockSpec((1, d), lambda i, j: (0, 0)),
            pl.BlockSpec((d, tf), lambda i, j: (0, j)),
            pl.BlockSpec((d, tf), lambda i, j: (0, j)),
            pl.BlockSpec((tf, d), lambda i, j: (j, 0)),
        ],
        out_specs=pl.BlockSpec((tm, d), lambda i, j: (i, 0)),
        out_shape=jax.ShapeDtypeStruct((t, d), F32),
        scratch_shapes=[pltpu.VMEM((tm, d), BF16)],
        compiler_params=pltpu.CompilerParams(
            dimension_semantics=("parallel", "arbitrary"), vmem_limit_bytes=VMEM_LIMIT),
        name="attn_ffn",
    )(x2d, o, w_o, g, w_gate, w_up, w_down)


def _hgrn_kernel(h_ref, g_ref, win_ref, lbraw_ref, og_ref, wo_ref, fg_ref, router_ref,
                 h_out, hn_out, meta_out, cnt_out, st_ref, o_scr, cnt_ref, *, layer):
    tt = h_ref.shape[0]
    width = HG_HEADS * HG_DK
    nchunk = tt // HG_CHUNK

    @pl.when(pl.program_id(1) == 0)
    def _():
        st_ref[...] = jnp.zeros_like(st_ref)

    lbraw = lbraw_ref[...]
    e = jnp.exp(lbraw - jnp.max(lbraw, axis=0, keepdims=True))
    sm = e / jnp.sum(e, axis=0, keepdims=True)
    lb = jnp.sum(sm[1:layer + 1], axis=0, keepdims=True)

    h_in = h_ref[...]
    hn = _rms(h_in, g_ref[...]).astype(BF16)
    proj = _dot(hn, win_ref[...])
    q_all = proj[:, :width] * (HG_DK ** -0.5)
    f_all = lb + (1.0 - lb) * (1.0 / (1.0 + jnp.exp(-proj[:, width:2 * width])))
    logf = jnp.log(f_all)
    k_all = 1.0 - f_all

    r = lax.broadcasted_iota(jnp.int32, (tt, tt), 0)
    c = lax.broadcasted_iota(jnp.int32, (tt, tt), 1)
    shift = HG_CHUNK.bit_length() - 1
    tril = (c <= r) & ((r >> shift) == (c >> shift))
    tril_b = jnp.where(tril, 1.0, 0.0).astype(BF16)
    hi, mid, lo = _split3(logf)
    b_all = _dot(tril_b, hi) + _dot(tril_b, mid) + _dot(tril_b, lo)

    for h in range(HG_HEADS):
        sl = slice(h * HG_DK, (h + 1) * HG_DK)
        q = q_all[:, sl]
        k = k_all[:, sl]
        b = b_all[:, sl]
        v = proj[:, 2 * width + h * HG_DK:2 * width + (h + 1) * HG_DK]
        gt = proj[:, 3 * width + h * HG_DK:3 * width + (h + 1) * HG_DK]
        v_b = v.astype(BF16)
        qd = (q * jnp.exp(b)).astype(BF16)
        kd = (k * jnp.exp(-b)).astype(BF16)
        a = jnp.where(tril, _dot_nt(qd, kd), 0.0)
        o = _dot(a.astype(BF16), v_b)

        st = st_ref[h]
        inter = []
        for n in range(nchunk):
            rows = slice(n * HG_CHUNK, (n + 1) * HG_CHUNK)
            b_n = b[rows]
            b_last = b_n[HG_CHUNK - 1:HG_CHUNK]
            inter.append(_dot_nt(qd[rows], st.astype(BF16)))
            kdl = (k[rows] * jnp.exp(b_last - b_n)).astype(BF16)
            st = st * jnp.exp(b_last) + _dot_tn(v_b[rows], kdl)
        st_ref[h] = st
        o = o + jnp.concatenate(inter, axis=0)
        o = _rms(o, og_ref[...]) * _silu(gt)
        o_scr[:, sl] = o.astype(BF16)

    h_new = h_in + _dot(o_scr[...], wo_ref[...])
    h_out[...] = h_new
    hn2 = _rms(h_new, fg_ref[...])
    hn_out[...] = hn2

    x_hi, x_mid, _ = _split3(hn2)
    router = router_ref[...]
    r_hi = router.astype(BF16)
    r_lo = (router - r_hi.astype(F32)).astype(BF16)
    logits = _dot(x_hi, r_hi) + _dot(x_mid, r_hi) + _dot(x_hi, r_lo)
    lane = lax.broadcasted_iota(jnp.int32, logits.shape, 1)
    neg = -jnp.inf
    logits = jnp.where(lane < N_EXPERTS, logits, neg)
    m1 = jnp.max(logits, axis=-1, keepdims=True)
    i1 = jnp.min(jnp.where(logits == m1, lane, LANES), axis=-1, keepdims=True)
    rest = jnp.where(lane == i1, neg, logits)
    m2 = jnp.max(rest, axis=-1, keepdims=True)
    i2 = jnp.min(jnp.where(rest == m2, lane, LANES), axis=-1, keepdims=True)
    e2 = jnp.exp(m2 - m1)
    w1 = 1.0 / (1.0 + e2)
    w2 = e2 / (1.0 + e2)

    @pl.when((pl.program_id(0) == 0) & (pl.program_id(1) == 0))
    def _():
        cnt_ref[...] = jnp.zeros_like(cnt_ref)

    onehot = jnp.where((lane == i1) | (lane == i2), 1.0, 0.0)
    earlier = jnp.where(c < r, 1.0, 0.0).astype(BF16)
    rank = _dot(earlier, onehot.astype(BF16)) + cnt_ref[...]
    cnt = cnt_ref[...] + jnp.sum(onehot, axis=0, keepdims=True)
    cnt_ref[...] = cnt
    cnt_out[...] = jnp.broadcast_to(cnt, cnt_out.shape)
    rank1 = jnp.sum(jnp.where(lane == i1, rank, 0.0), axis=-1, keepdims=True)
    rank2 = jnp.sum(jnp.where(lane == i2, rank, 0.0), axis=-1, keepdims=True)
    fields = (i1.astype(F32), i2.astype(F32), w1, w2, rank1, rank2)
    meta = jnp.zeros(logits.shape, F32)
    for idx, val in enumerate(fields):
        meta = jnp.where(lane == idx, val, meta)
    meta_out[...] = meta


def _hgrn(h2d, g, w_in, lb_raw, og, w_o, fg, router_p, *, batch, seq, tt, layer):
    t, d = h2d.shape
    nt = seq // tt
    width = HG_HEADS * HG_DK
    tok = lambda cols: pl.BlockSpec((tt, cols), lambda b, i: (b * nt + i, 0))
    const = lambda shape: pl.BlockSpec(shape, lambda b, i: (0,) * len(shape))
    return pl.pallas_call(
        functools.partial(_hgrn_kernel, layer=layer),
        grid=(batch, nt),
        in_specs=[tok(d), const((1, d)), const(w_in.shape), const(lb_raw.shape),
                  const((1, HG_DK)), const(w_o.shape), const((1, d)), const(router_p.shape)],
        out_specs=[tok(d), tok(d), tok(LANES), const((SUBLANES, LANES))],
        out_shape=[jax.ShapeDtypeStruct((t, d), F32),
                   jax.ShapeDtypeStruct((t, d), F32),
                   jax.ShapeDtypeStruct((t, LANES), F32),
                   jax.ShapeDtypeStruct((SUBLANES, LANES), F32)],
        scratch_shapes=[pltpu.VMEM((HG_HEADS, HG_DK, HG_DK), F32),
                        pltpu.VMEM((tt, width), BF16),
                        pltpu.VMEM((1, LANES), F32)],
        compiler_params=pltpu.CompilerParams(
            dimension_semantics=("arbitrary", "arbitrary"), vmem_limit_bytes=VMEM_LIMIT),
        name="hgrn",
    )(h2d, g, w_in, lb_raw, og, w_o, fg, router_p)


def _expert_kernel(src_ref, te_ref, tv_ref, x_hbm, wg_ref, wu_ref, wd_ref, y_ref,
                   xbuf, xb_ref, sems):
    del te_ref
    r = pl.program_id(0)
    j = pl.program_id(1)
    n_tiles = pl.num_programs(0)
    nj = pl.num_programs(1)
    tm = xb_ref.shape[0]
    part = tm // nj

    def gather_wait(slot):
        pltpu.make_async_copy(x_hbm.at[pl.ds(0, tm)], xbuf.at[slot], sems.at[slot]).wait()

    def gather_issue(tile, first, count):
        slot = tile % 2

        def issue(i, _):
            row = first + i
            pltpu.make_async_copy(x_hbm.at[pl.ds(src_ref[tile * tm + row], 1)],
                                  xbuf.at[slot, pl.ds(row, 1)], sems.at[slot]).start()
            return 0

        lax.fori_loop(0, count, issue, 0)

    @pl.when((r == 0) & (j == 0))
    def _():
        gather_issue(r, 0, tm)

    @pl.when(r + 1 < n_tiles)
    def _():
        gather_issue(r + 1, j * part, part)

    @pl.when(j == 0)
    def _():
        gather_wait(r % 2)
        xb_ref[...] = xbuf[r % 2].astype(BF16)
        y_ref[...] = jnp.zeros_like(y_ref)

    @pl.when(tv_ref[r] != 0)
    def _():
        x = xb_ref[...]
        a = _silu(_dot(x, wg_ref[...])) * _dot(x, wu_ref[...])
        y_ref[...] += _dot(a.astype(BF16), wd_ref[...])


def _experts(src_rows, tile_expert, tile_valid, x, w_gate, w_up, w_down, *, tm, tf):
    n_rows = src_rows.shape[0]
    d = x.shape[1]
    f = w_gate.shape[2]
    nj = f // tf
    col = lambda r, j, tv: jnp.where(tv[r] != 0, j, nj - 1)
    return pl.pallas_call(
        _expert_kernel,
        grid_spec=pltpu.PrefetchScalarGridSpec(
            num_scalar_prefetch=3,
            grid=(n_rows // tm, nj),
            in_specs=[
                pl.BlockSpec(memory_space=pl.ANY),
                pl.BlockSpec((None, d, tf), lambda r, j, s, te, tv: (te[r], 0, col(r, j, tv))),
                pl.BlockSpec((None, d, tf), lambda r, j, s, te, tv: (te[r], 0, col(r, j, tv))),
                pl.BlockSpec((None, tf, d), lambda r, j, s, te, tv: (te[r], col(r, j, tv), 0)),
            ],
            out_specs=pl.BlockSpec((tm, d), lambda r, j, s, te, tv: (r, 0)),
            scratch_shapes=[pltpu.VMEM((2, tm, d), F32), pltpu.VMEM((tm, d), BF16),
                            pltpu.SemaphoreType.DMA((2,))],
        ),
        out_shape=jax.ShapeDtypeStruct((n_rows, d), F32),
        compiler_params=pltpu.CompilerParams(
            dimension_semantics=("arbitrary", "arbitrary"), vmem_limit_bytes=VMEM_LIMIT),
        name="moe_experts",
    )(src_rows, tile_expert, tile_valid, x, w_gate, w_up, w_down)


def _combine_kernel(d1_ref, d2_ref, y_hbm, h_ref, meta_ref, fg_ref, out_ref, a_ref, b_ref, sem):
    tc = h_ref.shape[0]
    base = pl.program_id(0) * tc

    def issue(r, _):
        pltpu.make_async_copy(y_hbm.at[pl.ds(d1_ref[base + r], 1)], a_ref.at[pl.ds(r, 1)],
                              sem).start()
        pltpu.make_async_copy(y_hbm.at[pl.ds(d2_ref[base + r], 1)], b_ref.at[pl.ds(r, 1)],
                              sem).start()
        return 0

    lax.fori_loop(0, tc, issue, 0)
    pltpu.make_async_copy(y_hbm.at[pl.ds(0, tc)], a_ref, sem).wait()
    pltpu.make_async_copy(y_hbm.at[pl.ds(0, tc)], b_ref, sem).wait()
    meta = meta_ref[...]
    w1 = meta[:, 2:3]
    w2 = meta[:, 3:4]
    out_ref[...] = _rms(h_ref[...] + w1 * a_ref[...] + w2 * b_ref[...], fg_ref[...])


def _combine(dest1, dest2, y, h2d, meta, fg, *, tc):
    t, d = h2d.shape
    return pl.pallas_call(
        _combine_kernel,
        grid_spec=pltpu.PrefetchScalarGridSpec(
            num_scalar_prefetch=2,
            grid=(t // tc,),
            in_specs=[
                pl.BlockSpec(memory_space=pl.ANY),
                pl.BlockSpec((tc, d), lambda i, d1, d2: (i, 0)),
                pl.BlockSpec((tc, LANES), lambda i, d1, d2: (i, 0)),
                pl.BlockSpec((1, d), lambda i, d1, d2: (0, 0)),
            ],
            out_specs=pl.BlockSpec((tc, d), lambda i, d1, d2: (i, 0)),
            scratch_shapes=[pltpu.VMEM((tc, d), F32), pltpu.VMEM((tc, d), F32),
                            pltpu.SemaphoreType.DMA(())],
        ),
        out_shape=jax.ShapeDtypeStruct((t, d), F32),
        compiler_params=pltpu.CompilerParams(
            dimension_semantics=("arbitrary",), vmem_limit_bytes=VMEM_LIMIT),
        name="moe_combine",
    )(dest1, dest2, y, h2d, meta, fg)


def _routing_plan(meta, counts, *, tm, n_tiles):
    e1 = meta[:, 0].astype(jnp.int32)
    e2 = meta[:, 1].astype(jnp.int32)
    cnt = counts[0, :N_EXPERTS].astype(jnp.int32)
    padded = (cnt + tm - 1) // tm * tm
    ends = jnp.cumsum(padded)
    offs = ends - padded
    dest1 = offs[e1] + meta[:, 4].astype(jnp.int32)
    dest2 = offs[e2] + meta[:, 5].astype(jnp.int32)
    start = jnp.arange(n_tiles, dtype=jnp.int32) * tm
    tile_valid = (start < ends[-1]).astype(jnp.int32)
    last = jnp.sum((ends < ends[-1]).astype(jnp.int32))
    passed = jnp.sum((ends[None, :] <= start[:, None]).astype(jnp.int32), axis=1)
    tok = jnp.arange(meta.shape[0], dtype=jnp.int32)
    src_rows = jnp.zeros((n_tiles * tm,), jnp.int32).at[dest1].set(tok).at[dest2].set(tok)
    return dest1, dest2, src_rows, jnp.minimum(passed, last), tile_valid


def _pad_heads(w, head_dim):
    k = w.shape[0]
    w = w.reshape(k, MLA_HEADS, head_dim)
    w = jnp.pad(w, ((0, 0), (0, 0), (0, LANES - head_dim)))
    return w.reshape(k, MLA_HEADS * LANES)


def _pick(n, cap):
    t = min(n, cap)
    while n % t:
        t //= 2
    return t


def kernel(x, positions, mix_norm, ffn_norm, final_norm, mla_w_in, mla_q_norm, mla_w_q_up,
           mla_kv_norm, mla_w_kv_up, mla_w_o, hgrn_w_in, hgrn_lower_bounds, hgrn_out_norm,
           hgrn_w_o, ffn_w_gate, ffn_w_up, ffn_w_down, moe_router, moe_w_gate, moe_w_up,
           moe_w_down):
    batch, seq, d = x.shape
    t = batch * seq
    x2d = x.reshape(t, d)
    row = lambda v: v.reshape(1, -1)

    w_in = mla_w_in[0]
    n_lat = MLA_Q_LORA + MLA_KV_LORA
    w_in_p = w_in[:, :n_lat].astype(BF16)
    wkr_t = jnp.pad(w_in[:, n_lat:].T,
                    ((MLA_NOPE, LANES - MLA_NOPE - MLA_ROPE), (0, 0))).astype(BF16)
    wq_t = _pad_heads(mla_w_q_up[0], MLA_NOPE + MLA_ROPE).T.astype(BF16)
    w_kv = mla_w_kv_up[0].reshape(MLA_KV_LORA, MLA_HEADS, MLA_NOPE + MLA_V)
    wk_p = _pad_heads(w_kv[:, :, :MLA_NOPE].reshape(MLA_KV_LORA, -1), MLA_NOPE).astype(BF16)
    wv_t = w_kv[:, :, MLA_NOPE:].reshape(MLA_KV_LORA, -1).T.astype(BF16)
    inv_freq = ROPE_THETA ** (-jnp.arange(0, MLA_ROPE, 2, dtype=F32) / MLA_ROPE)
    freq = inv_freq.reshape(MLA_ROPE // 2, 1)
    pos = positions.astype(F32).reshape(1, t)

    q_t, k, v_t = _mla_proj(x2d, pos, row(mix_norm[0]), w_in_p, wkr_t, row(mla_q_norm[0]), wq_t,
                            row(mla_kv_norm[0]), wk_p, wv_t, freq,
                            batch=batch, seq=seq, tm=_pick(seq, 512))
    o = _mla_attn(q_t, k, v_t, tq=_pick(seq, 512)).reshape(t, MLA_HEADS * MLA_V)
    h = _attn_ffn(x2d, o, mla_w_o[0].astype(BF16), row(ffn_norm[0]),
                  ffn_w_gate[0].astype(BF16), ffn_w_up[0].astype(BF16),
                  ffn_w_down[0].astype(BF16), tm=_pick(t, 1024), tf=896)

    router_p = jnp.pad(moe_router[0], ((0, 0), (0, LANES - N_EXPERTS)))
    h, hn, meta, counts = _hgrn(h, row(mix_norm[1]), hgrn_w_in[0].astype(BF16),
                                hgrn_lower_bounds, row(hgrn_out_norm[0]),
                                hgrn_w_o[0].astype(BF16), row(ffn_norm[1]), router_p,
                                batch=batch, seq=seq, tt=_pick(seq, 256), layer=1)
    tm = _pick(t, 512)
    n_tiles = 2 * t // tm + N_EXPERTS
    dest1, dest2, src_rows, tile_expert, tile_valid = _routing_plan(
        meta, counts, tm=tm, n_tiles=n_tiles)
    y = _experts(src_rows, tile_expert, tile_valid, hn, moe_w_gate[0].astype(BF16),
                 moe_w_up[0].astype(BF16), moe_w_down[0].astype(BF16), tm=tm, tf=1792)
    out = _combine(dest1, dest2, y, h, meta, row(final_norm), tc=_pick(t, 512))
    return out.reshape(batch, seq, d)
```

```python
import functools

import jax
import jax.numpy as jnp
from jax import lax
from jax.experimental import pallas as pl
from jax.experimental.pallas import tpu as pltpu

EPS = 1e-6
LANES = 128
SUBLANES = 8

MLA_HEADS = 16
MLA_Q_LORA = 384
MLA_KV_LORA = 256
MLA_NOPE = 64
MLA_ROPE = 32
MLA_V = 64
ROPE_THETA = 10000.0

HG_HEADS = 8
HG_DK = 128
HG_CHUNK = 64

N_EXPERTS = 8

VMEM_LIMIT = 56 * 1024 * 1024

BF16 = jnp.bfloat16
F32 = jnp.float32


def _dot(a, b):
    return jnp.dot(a, b, preferred_element_type=F32)


def _dot_nt(a, b):
    return lax.dot_general(a, b, (((1,), (1,)), ((), ())), preferred_element_type=F32)


def _dot_tn(a, b):
    return lax.dot_general(a, b, (((0,), (0,)), ((), ())), preferred_element_type=F32)


def _rms(x, g):
    return x * lax.rsqrt(jnp.mean(x * x, axis=-1, keepdims=True) + EPS) * g


def _silu(x):
    return x * (1.0 / (1.0 + jnp.exp(-x)))


def _split3(x):
    hi = x.astype(BF16)
    r = x - hi.astype(F32)
    mid = r.astype(BF16)
    lo = (r - mid.astype(F32)).astype(BF16)
    return hi, mid, lo


def _mla_proj_kernel(x_ref, pos_ref, g_ref, win_ref, wkr_ref, qg_ref, wq_ref, kvg_ref, wk_ref,
                     wv_ref, freq_ref, q_out, k_out, v_out, *, scale):
    half = MLA_ROPE // 2
    x1_rows = slice(MLA_NOPE, MLA_NOPE + half)
    x2_rows = slice(MLA_NOPE + half, MLA_NOPE + MLA_ROPE)
    hn = _rms(x_ref[...], g_ref[...]).astype(BF16)
    proj = _dot(hn, win_ref[...])
    cqn = _rms(proj[:, :MLA_Q_LORA], qg_ref[...]).astype(BF16)
    ckvn = _rms(proj[:, MLA_Q_LORA:], kvg_ref[...]).astype(BF16)

    ang = freq_ref[...] * pos_ref[...]
    cos = jnp.cos(ang)
    sin = jnp.sin(ang)

    def rope_t(slot, mul):
        x1 = slot[x1_rows]
        x2 = slot[x2_rows]
        return jnp.concatenate([slot[:MLA_NOPE] * mul, (x1 * cos - x2 * sin) * mul,
                                (x2 * cos + x1 * sin) * mul, slot[MLA_NOPE + MLA_ROPE:]], axis=0)

    kr_t = _dot_nt(wkr_ref[...], hn)
    kr_roped = rope_t(kr_t, 1.0).T
    q_t = _dot_nt(wq_ref[...], cqn)
    k_pad = _dot(ckvn, wk_ref[...])
    for h in range(MLA_HEADS):
        sl = slice(h * LANES, (h + 1) * LANES)
        q_out[h] = rope_t(q_t[sl], scale).astype(BF16)
        k_out[h] = (k_pad[:, sl] + kr_roped).astype(BF16)
    v_out[...] = _dot_nt(wv_ref[...], ckvn).astype(BF16)


def _mla_proj(x2d, pos, g, w_in_p, wkr_t, qg, wq_t, kvg, wk_p, wv_t, freq, *, batch, seq, tm):
    t = x2d.shape[0]
    d = x2d.shape[1]
    nt = seq // tm
    const = lambda shape: pl.BlockSpec(shape, lambda i: (0,) * len(shape))
    scale = (MLA_NOPE + MLA_ROPE) ** -0.5 * 1.4426950408889634
    return pl.pallas_call(
        functools.partial(_mla_proj_kernel, scale=scale),
        grid=(t // tm,),
        in_specs=[
            pl.BlockSpec((tm, d), lambda i: (i, 0)),
            pl.BlockSpec((1, tm), lambda i: (0, i)),
            const((1, d)),
            const(w_in_p.shape),
            const(wkr_t.shape),
            const((1, MLA_Q_LORA)),
            const(wq_t.shape),
            const((1, MLA_KV_LORA)),
            const(wk_p.shape),
            const(wv_t.shape),
            const(freq.shape),
        ],
        out_specs=[
            pl.BlockSpec((None, MLA_HEADS, LANES, tm), lambda i: (i // nt, 0, 0, i % nt)),
            pl.BlockSpec((None, MLA_HEADS, tm, LANES), lambda i: (i // nt, 0, i % nt, 0)),
            pl.BlockSpec((None, MLA_HEADS * MLA_V, tm), lambda i: (i // nt, 0, i % nt)),
        ],
        out_shape=[
            jax.ShapeDtypeStruct((batch, MLA_HEADS, LANES, seq), BF16),
            jax.ShapeDtypeStruct((batch, MLA_HEADS, seq, LANES), BF16),
            jax.ShapeDtypeStruct((batch, MLA_HEADS * MLA_V, seq), BF16),
        ],
        compiler_params=pltpu.CompilerParams(
            dimension_semantics=("parallel",), vmem_limit_bytes=VMEM_LIMIT),
        name="mla_proj",
    )(x2d, pos, g, w_in_p, wkr_t, qg, wq_t, kvg, wk_p, wv_t, freq)


def _attn_kernel(q_ref, k_ref, v_ref, o_ref, acc_ref, m_ref, l_ref, sa_ref, sb_ref, *, tq):
    seq = k_ref.shape[1]
    nq = seq // tq
    causal = (lax.broadcasted_iota(jnp.int32, (tq, tq), 0)
              <= lax.broadcasted_iota(jnp.int32, (tq, tq), 1))

    def scores(dst, blk, q0):
        k0 = pl.multiple_of(blk * tq, tq)
        for h in range(2):
            dst[h] = _dot(k_ref[h, pl.ds(k0, tq), :], q_ref[h, :, pl.ds(q0, tq)])

    def softmax_pv(src, blk, masked):
        k0 = pl.multiple_of(blk * tq, tq)
        for h in range(2):
            s = src[h]
            if masked:
                s = jnp.where(causal, s, -jnp.inf)
            m = m_ref[h]
            m_new = jnp.maximum(m, jnp.max(s, axis=0, keepdims=True))
            p = jnp.exp2(s - m_new)
            alpha = jnp.exp2(m - m_new)
            l_ref[h] = alpha * l_ref[h] + jnp.sum(p, axis=0, keepdims=True)
            m_ref[h] = m_new
            v_t = v_ref[h * MLA_V:(h + 1) * MLA_V, pl.ds(k0, tq)]
            acc_ref[h] = alpha * acc_ref[h] + _dot(v_t, p.astype(BF16))

    def q_block(qi, _):
        q0 = pl.multiple_of(qi * tq, tq)
        acc_ref[...] = jnp.zeros_like(acc_ref)
        m_ref[...] = jnp.full_like(m_ref, -jnp.inf)
        l_ref[...] = jnp.zeros_like(l_ref)

        scores(sa_ref, 0, q0)

        def pair(jp, _):
            scores(sb_ref, 2 * jp + 1, q0)
            softmax_pv(sa_ref, 2 * jp, False)
            scores(sa_ref, 2 * jp + 2, q0)
            softmax_pv(sb_ref, 2 * jp + 1, False)
            return 0

        lax.fori_loop(0, qi // 2, pair, 0)

        @pl.when(qi % 2 == 1)
        def _():
            scores(sb_ref, qi, q0)
            softmax_pv(sa_ref, qi - 1, False)
            softmax_pv(sb_ref, qi, True)

        @pl.when(qi % 2 == 0)
        def _():
            softmax_pv(sa_ref, qi, True)

        out = jnp.concatenate([acc_ref[h] / l_ref[h] for h in range(2)], axis=0)
        o_ref[pl.ds(q0, tq), :] = out.T.astype(BF16)
        return 0

    lax.fori_loop(0, nq, q_block, 0)


def _mla_attn(q_t, k, v_t, *, tq):
    batch, heads, seq, _ = k.shape
    return pl.pallas_call(
        functools.partial(_attn_kernel, tq=tq),
        grid=(batch, heads // 2),
        in_specs=[
            pl.BlockSpec((None, 2, LANES, seq), lambda b, hp: (b, hp, 0, 0)),
            pl.BlockSpec((None, 2, seq, LANES), lambda b, hp: (b, hp, 0, 0)),
            pl.BlockSpec((None, 2 * MLA_V, seq), lambda b, hp: (b, hp, 0)),
        ],
        out_specs=pl.BlockSpec((None, seq, 2 * MLA_V), lambda b, hp: (b, 0, hp)),
        out_shape=jax.ShapeDtypeStruct((batch, seq, heads * MLA_V), BF16),
        scratch_shapes=[pltpu.VMEM((2, MLA_V, tq), F32),
                        pltpu.VMEM((2, 1, tq), F32), pltpu.VMEM((2, 1, tq), F32),
                        pltpu.VMEM((2, tq, tq), F32), pltpu.VMEM((2, tq, tq), F32)],
        compiler_params=pltpu.CompilerParams(
            dimension_semantics=("parallel", "parallel"), vmem_limit_bytes=VMEM_LIMIT),
        name="mla_attn",
    )(q_t, k, v_t)


def _attn_ffn_kernel(x_ref, o_ref, wo_ref, g_ref, wg_ref, wu_ref, wd_ref, out_ref, hn_ref):
    j = pl.program_id(1)

    @pl.when(j == 0)
    def _():
        h1 = x_ref[...] + _dot(o_ref[...], wo_ref[...])
        out_ref[...] = h1
        hn_ref[...] = _rms(h1, g_ref[...]).astype(BF16)

    hn = hn_ref[...]
    a = _silu(_dot(hn, wg_ref[...])) * _dot(hn, wu_ref[...])
    out_ref[...] += _dot(a.astype(BF16), wd_ref[...])


def _attn_ffn(x2d, o, w_o, g, w_gate, w_up, w_down, *, tm, tf):
    t, d = x2d.shape
    f = w_gate.shape[1]
    return pl.pallas_call(
        _attn_ffn_kernel,
        grid=(t // tm, f // tf),
        in_specs=[
            pl.BlockSpec((tm, d), lambda i, j: (i, 0)),
            pl.BlockSpec((tm, o.shape[1]), lambda i, j: (i, 0)),
            pl.BlockSpec(w_o.shape, lambda i, j: (0, 0)),
            pl.BlockSpec((1, d), lambda i, j: (0, 0)),
            pl.BlockSpec((d, tf), lambda i, j: (0, j)),
            pl.BlockSpec((d, tf), lambda i, j: (0, j)),
            pl.BlockSpec((tf, d), lambda i, j: (j, 0)),
        ],
        out_specs=pl.BlockSpec((tm, d), lambda i, j: (i, 0)),
        out_shape=jax.ShapeDtypeStruct((t, d), F32),
        scratch_shapes=[pltpu.VMEM((tm, d), BF16)],
        compiler_params=pltpu.CompilerParams(
            dimension_semantics=("parallel", "arbitrary"), vmem_limit_bytes=VMEM_LIMIT),
        name="attn_ffn",
    )(x2d, o, w_o, g, w_gate, w_up, w_down)


def _hgrn_kernel(h_ref, g_ref, win_ref, lbraw_ref, og_ref, wo_ref, fg_ref, router_ref,
                 h_out, hn_out, meta_out, cnt_out, st_ref, o_scr, cnt_ref, *, layer):
    tt = h_ref.shape[0]
    width = HG_HEADS * HG_DK
    nchunk = tt // HG_CHUNK

    @pl.when(pl.program_id(1) == 0)
    def _():
        st_ref[...] = jnp.zeros_like(st_ref)

    lbraw = lbraw_ref[...]
    e = jnp.exp(lbraw - jnp.max(lbraw, axis=0, keepdims=True))
    sm = e / jnp.sum(e, axis=0, keepdims=True)
    lb = jnp.sum(sm[1:layer + 1], axis=0, keepdims=True)

    h_in = h_ref[...]
    hn = _rms(h_in, g_ref[...]).astype(BF16)
    proj = _dot(hn, win_ref[...])
    q_all = proj[:, :width] * (HG_DK ** -0.5)
    f_all = lb + (1.0 - lb) * (1.0 / (1.0 + jnp.exp(-proj[:, width:2 * width])))
    logf = jnp.log(f_all)
    k_all = 1.0 - f_all

    r = lax.broadcasted_iota(jnp.int32, (tt, tt), 0)
    c = lax.broadcasted_iota(jnp.int32, (tt, tt), 1)
    shift = HG_CHUNK.bit_length() - 1
    tril = (c <= r) & ((r >> shift) == (c >> shift))
    tril_b = jnp.where(tril, 1.0, 0.0).astype(BF16)
    hi, mid, lo = _split3(logf)
    b_all = _dot(tril_b, hi) + _dot(tril_b, mid) + _dot(tril_b, lo)

    for h in range(HG_HEADS):
        sl = slice(h * HG_DK, (h + 1) * HG_DK)
        q = q_all[:, sl]
        k = k_all[:, sl]
        b = b_all[:, sl]
        v = proj[:, 2 * width + h * HG_DK:2 * width + (h + 1) * HG_DK]
        gt = proj[:, 3 * width + h * HG_DK:3 * width + (h + 1) * HG_DK]
        v_b = v.astype(BF16)
        qd = (q * jnp.exp(b)).astype(BF16)
        kd = (k * jnp.exp(-b)).astype(BF16)
        a = jnp.where(tril, _dot_nt(qd, kd), 0.0)
        o = _dot(a.astype(BF16), v_b)

        st = st_ref[h]
        inter = []
        for n in range(nchunk):
            rows = slice(n * HG_CHUNK, (n + 1) * HG_CHUNK)
            b_n = b[rows]
            b_last = b_n[HG_CHUNK - 1:HG_CHUNK]
            inter.append(_dot_nt(qd[rows], st.astype(BF16)))
            kdl = (k[rows] * jnp.exp(b_last - b_n)).astype(BF16)
            st = st * jnp.exp(b_last) + _dot_tn(v_b[rows], kdl)
        st_ref[h] = st
        o = o + jnp.concatenate(inter, axis=0)
        o = _rms(o, og_ref[...]) * _silu(gt)
        o_scr[:, sl] = o.astype(BF16)

    h_new = h_in + _dot(o_scr[...], wo_ref[...])
    h_out[...] = h_new
    hn2 = _rms(h_new, fg_ref[...])
    hn_out[...] = hn2

    x_hi, x_mid, _ = _split3(hn2)
    router = router_ref[...]
    r_hi = router.astype(BF16)
    r_lo = (router - r_hi.astype(F32)).astype(BF16)
    logits = _dot(x_hi, r_hi) + _dot(x_mid, r_hi) + _dot(x_hi, r_lo)
    lane = lax.broadcasted_iota(jnp.int32, logits.shape, 1)
    neg = -jnp.inf
    logits = jnp.where(lane < N_EXPERTS, logits, neg)
    m1 = jnp.max(logits, axis=-1, keepdims=True)
    i1 = jnp.min(jnp.where(logits == m1, lane, LANES), axis=-1, keepdims=True)
    rest = jnp.where(lane == i1, neg, logits)
    m2 = jnp.max(rest, axis=-1, keepdims=True)
    i2 = jnp.min(jnp.where(rest == m2, lane, LANES), axis=-1, keepdims=True)
    e2 = jnp.exp(m2 - m1)
    w1 = 1.0 / (1.0 + e2)
    w2 = e2 / (1.0 + e2)

    @pl.when((pl.program_id(0) == 0) & (pl.program_id(1) == 0))
    def _():
        cnt_ref[...] = jnp.zeros_like(cnt_ref)

    onehot = jnp.where((lane == i1) | (lane == i2), 1.0, 0.0)
    earlier = jnp.where(c < r, 1.0, 0.0).astype(BF16)
    rank = _dot(earlier, onehot.astype(BF16)) + cnt_ref[...]
    cnt = cnt_ref[...] + jnp.sum(onehot, axis=0, keepdims=True)
    cnt_ref[...] = cnt
    cnt_out[...] = jnp.broadcast_to(cnt, cnt_out.shape)
    rank1 = jnp.sum(jnp.where(lane == i1, rank, 0.0), axis=-1, keepdims=True)
    rank2 = jnp.sum(jnp.where(lane == i2, rank, 0.0), axis=-1, keepdims=True)
    fields = (i1.astype(F32), i2.astype(F32), w1, w2, rank1, rank2)
    meta = jnp.zeros(logits.shape, F32)
    for idx, val in enumerate(fields):
        meta = jnp.where(lane == idx, val, meta)
    meta_out[...] = meta


def _hgrn(h2d, g, w_in, lb_raw, og, w_o, fg, router_p, *, batch, seq, tt, layer):
    t, d = h2d.shape
    nt = seq // tt
    width = HG_HEADS * HG_DK
    tok = lambda cols: pl.BlockSpec((tt, cols), lambda b, i: (b * nt + i, 0))
    const = lambda shape: pl.BlockSpec(shape, lambda b, i: (0,) * len(shape))
    return pl.pallas_call(
        functools.partial(_hgrn_kernel, layer=layer),
        grid=(batch, nt),
        in_specs=[tok(d), const((1, d)), const(w_in.shape), const(lb_raw.shape),
                  const((1, HG_DK)), const(w_o.shape), const((1, d)), const(router_p.shape)],
        out_specs=[tok(d), tok(d), tok(LANES), const((SUBLANES, LANES))],
        out_shape=[jax.ShapeDtypeStruct((t, d), F32),
                   jax.ShapeDtypeStruct((t, d), F32),
                   jax.ShapeDtypeStruct((t, LANES), F32),
                   jax.ShapeDtypeStruct((SUBLANES, LANES), F32)],
        scratch_shapes=[pltpu.VMEM((HG_HEADS, HG_DK, HG_DK), F32),
                        pltpu.VMEM((tt, width), BF16),
                        pltpu.VMEM((1, LANES), F32)],
        compiler_params=pltpu.CompilerParams(
            dimension_semantics=("arbitrary", "arbitrary"), vmem_limit_bytes=VMEM_LIMIT),
        name="hgrn",
    )(h2d, g, w_in, lb_raw, og, w_o, fg, router_p)


def _expert_kernel(src_ref, te_ref, tv_ref, x_hbm, wg_ref, wu_ref, wd_ref, y_ref,
                   xbuf, xb_ref, sems):
    del te_ref
    r = pl.program_id(0)
    j = pl.program_id(1)
    n_tiles = pl.num_programs(0)
    nj = pl.num_programs(1)
    tm = xb_ref.shape[0]
    part = tm // nj
    valid = tv_ref[r] != 0
    has_next = r + 1 < n_tiles

    def row_copy(tile, row):
        slot = tile % 2
        return pltpu.make_async_copy(x_hbm.at[pl.ds(src_ref[tile * tm + row], 1)],
                                     xbuf.at[slot, pl.ds(row, 1)], sems.at[slot])

    def swiglu_step():
        x = xb_ref[...]
        a = _silu(_dot(x, wg_ref[...])) * _dot(x, wu_ref[...])
        y_ref[...] += _dot(a.astype(BF16), wd_ref[...])

    @pl.when((r == 0) & (j == 0))
    def _():
        lax.fori_loop(0, tm, lambda i, c: (row_copy(r, i).start(), c)[1], 0)

    @pl.when((j == 0) & ((r == 0) | (tv_ref[jnp.maximum(r - 1, 0)] != 0)))
    def _():
        pltpu.make_async_copy(x_hbm.at[pl.ds(0, tm)], xbuf.at[r % 2], sems.at[r % 2]).wait()
        xb_ref[...] = xbuf[r % 2].astype(BF16)

    @pl.when(j == 0)
    def _():
        y_ref[...] = jnp.zeros_like(y_ref)

    @pl.when(valid & has_next)
    def _():
        for i in range(part):
            row_copy(r + 1, j * part + i).start()
        swiglu_step()

    @pl.when(valid & jnp.logical_not(has_next))
    def _():
        swiglu_step()


def _experts(src_rows, tile_expert, tile_valid, x, w_gate, w_up, w_down, *, tm, tf):
    n_rows = src_rows.shape[0]
    d = x.shape[1]
    f = w_gate.shape[2]
    nj = f // tf
    col = lambda r, j, tv: jnp.where(tv[r] != 0, j, nj - 1)
    return pl.pallas_call(
        _expert_kernel,
        grid_spec=pltpu.PrefetchScalarGridSpec(
            num_scalar_prefetch=3,
            grid=(n_rows // tm, nj),
            in_specs=[
                pl.BlockSpec(memory_space=pl.ANY),
                pl.BlockSpec((None, d, tf), lambda r, j, s, te, tv: (te[r], 0, col(r, j, tv))),
                pl.BlockSpec((None, d, tf), lambda r, j, s, te, tv: (te[r], 0, col(r, j, tv))),
                pl.BlockSpec((None, tf, d), lambda r, j, s, te, tv: (te[r], col(r, j, tv), 0)),
            ],
            out_specs=pl.BlockSpec((tm, d), lambda r, j, s, te, tv: (r, 0)),
            scratch_shapes=[pltpu.VMEM((2, tm, d), F32), pltpu.VMEM((tm, d), BF16),
                            pltpu.SemaphoreType.DMA((2,))],
        ),
        out_shape=jax.ShapeDtypeStruct((n_rows, d), F32),
        compiler_params=pltpu.CompilerParams(
            dimension_semantics=("arbitrary", "arbitrary"), vmem_limit_bytes=VMEM_LIMIT),
        name="moe_experts",
    )(src_rows, tile_expert, tile_valid, x, w_gate, w_up, w_down)


def _combine_kernel(d1_ref, d2_ref, y_hbm, h_ref, meta_ref, fg_ref, out_ref, a_ref, b_ref, sem):
    tc = h_ref.shape[0]
    base = pl.program_id(0) * tc

    def issue(r, _):
        pltpu.make_async_copy(y_hbm.at[pl.ds(d1_ref[base + r], 1)], a_ref.at[pl.ds(r, 1)],
                              sem).start()
        pltpu.make_async_copy(y_hbm.at[pl.ds(d2_ref[base + r], 1)], b_ref.at[pl.ds(r, 1)],
                              sem).start()
        return 0

    lax.fori_loop(0, tc, issue, 0, unroll=16)
    pltpu.make_async_copy(y_hbm.at[pl.ds(0, tc)], a_ref, sem).wait()
    pltpu.make_async_copy(y_hbm.at[pl.ds(0, tc)], b_ref, sem).wait()
    meta = meta_ref[...]
    w1 = meta[:, 2:3]
    w2 = meta[:, 3:4]
    out_ref[...] = _rms(h_ref[...] + w1 * a_ref[...] + w2 * b_ref[...], fg_ref[...])


def _combine(dest1, dest2, y, h2d, meta, fg, *, tc):
    t, d = h2d.shape
    return pl.pallas_call(
        _combine_kernel,
        grid_spec=pltpu.PrefetchScalarGridSpec(
            num_scalar_prefetch=2,
            grid=(t // tc,),
            in_specs=[
                pl.BlockSpec(memory_space=pl.ANY),
                pl.BlockSpec((tc, d), lambda i, d1, d2: (i, 0)),
                pl.BlockSpec((tc, LANES), lambda i, d1, d2: (i, 0)),
                pl.BlockSpec((1, d), lambda i, d1, d2: (0, 0)),
            ],
            out_specs=pl.BlockSpec((tc, d), lambda i, d1, d2: (i, 0)),
            scratch_shapes=[pltpu.VMEM((tc, d), F32), pltpu.VMEM((tc, d), F32),
                            pltpu.SemaphoreType.DMA(())],
        ),
        out_shape=jax.ShapeDtypeStruct((t, d), F32),
        compiler_params=pltpu.CompilerParams(
            dimension_semantics=("arbitrary",), vmem_limit_bytes=VMEM_LIMIT),
        name="moe_combine",
    )(dest1, dest2, y, h2d, meta, fg)


def _routing_plan(meta, counts, *, tm, n_tiles):
    e1 = meta[:, 0].astype(jnp.int32)
    e2 = meta[:, 1].astype(jnp.int32)
    cnt = counts[0, :N_EXPERTS].astype(jnp.int32)
    padded = (cnt + tm - 1) // tm * tm
    ends = jnp.cumsum(padded)
    offs = ends - padded
    dest1 = offs[e1] + meta[:, 4].astype(jnp.int32)
    dest2 = offs[e2] + meta[:, 5].astype(jnp.int32)
    start = jnp.arange(n_tiles, dtype=jnp.int32) * tm
    tile_valid = (start < ends[-1]).astype(jnp.int32)
    last = jnp.sum((ends < ends[-1]).astype(jnp.int32))
    passed = jnp.sum((ends[None, :] <= start[:, None]).astype(jnp.int32), axis=1)
    tok = jnp.arange(meta.shape[0], dtype=jnp.int32)
    src_rows = jnp.zeros((n_tiles * tm,), jnp.int32).at[dest1].set(tok).at[dest2].set(tok)
    return dest1, dest2, src_rows, jnp.minimum(passed, last), tile_valid


def _pad_heads(w, head_dim):
    k = w.shape[0]
    w = w.reshape(k, MLA_HEADS, head_dim)
    w = jnp.pad(w, ((0, 0), (0, 0), (0, LANES - head_dim)))
    return w.reshape(k, MLA_HEADS * LANES)


def _pick(n, cap):
    t = min(n, cap)
    while n % t:
        t //= 2
    return t


def kernel(x, positions, mix_norm, ffn_norm, final_norm, mla_w_in, mla_q_norm, mla_w_q_up,
           mla_kv_norm, mla_w_kv_up, mla_w_o, hgrn_w_in, hgrn_lower_bounds, hgrn_out_norm,
           hgrn_w_o, ffn_w_gate, ffn_w_up, ffn_w_down, moe_router, moe_w_gate, moe_w_up,
           moe_w_down):
    batch, seq, d = x.shape
    t = batch * seq
    x2d = x.reshape(t, d)
    row = lambda v: v.reshape(1, -1)

    w_in = mla_w_in[0]
    n_lat = MLA_Q_LORA + MLA_KV_LORA
    w_in_p = w_in[:, :n_lat].astype(BF16)
    wkr_t = jnp.pad(w_in[:, n_lat:].T,
                    ((MLA_NOPE, LANES - MLA_NOPE - MLA_ROPE), (0, 0))).astype(BF16)
    wq_t = _pad_heads(mla_w_q_up[0], MLA_NOPE + MLA_ROPE).T.astype(BF16)
    w_kv = mla_w_kv_up[0].reshape(MLA_KV_LORA, MLA_HEADS, MLA_NOPE + MLA_V)
    wk_p = _pad_heads(w_kv[:, :, :MLA_NOPE].reshape(MLA_KV_LORA, -1), MLA_NOPE).astype(BF16)
    wv_t = w_kv[:, :, MLA_NOPE:].reshape(MLA_KV_LORA, -1).T.astype(BF16)
    inv_freq = ROPE_THETA ** (-jnp.arange(0, MLA_ROPE, 2, dtype=F32) / MLA_ROPE)
    freq = inv_freq.reshape(MLA_ROPE // 2, 1)
    pos = positions.astype(F32).reshape(1, t)

    q_t, k, v_t = _mla_proj(x2d, pos, row(mix_norm[0]), w_in_p, wkr_t, row(mla_q_norm[0]), wq_t,
                            row(mla_kv_norm[0]), wk_p, wv_t, freq,
                            batch=batch, seq=seq, tm=_pick(seq, 512))
    o = _mla_attn(q_t, k, v_t, tq=_pick(seq, 512)).reshape(t, MLA_HEADS * MLA_V)
    h = _attn_ffn(x2d, o, mla_w_o[0].astype(BF16), row(ffn_norm[0]),
                  ffn_w_gate[0].astype(BF16), ffn_w_up[0].astype(BF16),
                  ffn_w_down[0].astype(BF16), tm=_pick(t, 1024), tf=896)

    router_p = jnp.pad(moe_router[0], ((0, 0), (0, LANES - N_EXPERTS)))
    h, hn, meta, counts = _hgrn(h, row(mix_norm[1]), hgrn_w_in[0].astype(BF16),
                                hgrn_lower_bounds, row(hgrn_out_norm[0]),
                                hgrn_w_o[0].astype(BF16), row(ffn_norm[1]), router_p,
                                batch=batch, seq=seq, tt=_pick(seq, 256), layer=1)
    tm = _pick(t, 512)
    n_tiles = 2 * t // tm + N_EXPERTS
    dest1, dest2, src_rows, tile_expert, tile_valid = _routing_plan(
        meta, counts, tm=tm, n_tiles=n_tiles)
    y = _experts(src_rows, tile_expert, tile_valid, hn, moe_w_gate[0].astype(BF16),
                 moe_w_up[0].astype(BF16), moe_w_down[0].astype(BF16), tm=tm, tf=1792)
    out = _combine(dest1, dest2, y, h, meta, row(final_norm), tc=_pick(t, 512))
    return out.reshape(batch, seq, d)
```

```python
import functools

import jax
import jax.numpy as jnp
from jax import lax
from jax.experimental import pallas as pl
from jax.experimental.pallas import tpu as pltpu

EPS = 1e-6
LANES = 128
SUBLANES = 8

MLA_HEADS = 16
MLA_Q_LORA = 384
MLA_KV_LORA = 256
MLA_NOPE = 64
MLA_ROPE = 32
MLA_V = 64
ROPE_THETA = 10000.0

HG_HEADS = 8
HG_DK = 128
HG_CHUNK = 64

N_EXPERTS = 8

VMEM_LIMIT = 56 * 1024 * 1024

BF16 = jnp.bfloat16
F32 = jnp.float32


def _dot(a, b):
    return jnp.dot(a, b, preferred_element_type=F32)


def _dot_nt(a, b):
    return lax.dot_general(a, b, (((1,), (1,)), ((), ())), preferred_element_type=F32)


def _dot_tn(a, b):
    return lax.dot_general(a, b, (((0,), (0,)), ((), ())), preferred_element_type=F32)


def _rms(x, g):
    return x * lax.rsqrt(jnp.mean(x * x, axis=-1, keepdims=True) + EPS) * g


def _silu(x):
    return x * (1.0 / (1.0 + jnp.exp(-x)))


def _store_rows_as_tiles(ref, x):
    n, d = x.shape
    nc = d // LANES
    for c in range(nc):
        ref[pl.ds(c, n, stride=nc), :] = x[:, c * LANES:(c + 1) * LANES]


def _load_rows_from_tiles(ref, nc):
    n = ref.shape[0] // nc
    return jnp.concatenate([ref[pl.ds(c, n, stride=nc), :] for c in range(nc)], axis=1)


def _split3(x):
    hi = x.astype(BF16)
    r = x - hi.astype(F32)
    mid = r.astype(BF16)
    lo = (r - mid.astype(F32)).astype(BF16)
    return hi, mid, lo


def _mla_proj_kernel(x_ref, pos_ref, g_ref, win_ref, wkr_ref, qg_ref, wq_ref, kvg_ref, wk_ref,
                     wv_ref, freq_ref, q_out, k_out, v_out, *, scale):
    half = MLA_ROPE // 2
    x1_rows = slice(MLA_NOPE, MLA_NOPE + half)
    x2_rows = slice(MLA_NOPE + half, MLA_NOPE + MLA_ROPE)
    hn = _rms(x_ref[...], g_ref[...]).astype(BF16)
    proj = _dot(hn, win_ref[...])
    cqn = _rms(proj[:, :MLA_Q_LORA], qg_ref[...]).astype(BF16)
    ckvn = _rms(proj[:, MLA_Q_LORA:], kvg_ref[...]).astype(BF16)

    ang = freq_ref[...] * pos_ref[...]
    cos = jnp.cos(ang)
    sin = jnp.sin(ang)

    def rope_t(slot, mul):
        x1 = slot[x1_rows]
        x2 = slot[x2_rows]
        return jnp.concatenate([slot[:MLA_NOPE] * mul, (x1 * cos - x2 * sin) * mul,
                                (x2 * cos + x1 * sin) * mul, slot[MLA_NOPE + MLA_ROPE:]], axis=0)

    kr_t = _dot_nt(wkr_ref[...], hn)
    kr_roped = rope_t(kr_t, 1.0).T
    q_t = _dot_nt(wq_ref[...], cqn)
    k_pad = _dot(ckvn, wk_ref[...])
    for h in range(MLA_HEADS):
        sl = slice(h * LANES, (h + 1) * LANES)
        q_out[h] = rope_t(q_t[sl], scale).astype(BF16)
        k_out[h] = (k_pad[:, sl] + kr_roped).astype(BF16)
    v_out[...] = _dot_nt(wv_ref[...], ckvn).astype(BF16)


def _mla_proj(x2d, pos, g, w_in_p, wkr_t, qg, wq_t, kvg, wk_p, wv_t, freq, *, batch, seq, tm):
    t = x2d.shape[0]
    d = x2d.shape[1]
    nt = seq // tm
    const = lambda shape: pl.BlockSpec(shape, lambda i: (0,) * len(shape))
    scale = (MLA_NOPE + MLA_ROPE) ** -0.5 * 1.4426950408889634
    return pl.pallas_call(
        functools.partial(_mla_proj_kernel, scale=scale),
        grid=(t // tm,),
        in_specs=[
            pl.BlockSpec((tm, d), lambda i: (i, 0)),
            pl.BlockSpec((1, tm), lambda i: (0, i)),
            const((1, d)),
            const(w_in_p.shape),
            const(wkr_t.shape),
            const((1, MLA_Q_LORA)),
            const(wq_t.shape),
            const((1, MLA_KV_LORA)),
            const(wk_p.shape),
            const(wv_t.shape),
            const(freq.shape),
        ],
        out_specs=[
            pl.BlockSpec((None, MLA_HEADS, LANES, tm), lambda i: (i // nt, 0, 0, i % nt)),
            pl.BlockSpec((None, MLA_HEADS, tm, LANES), lambda i: (i // nt, 0, i % nt, 0)),
            pl.BlockSpec((None, MLA_HEADS * MLA_V, tm), lambda i: (i // nt, 0, i % nt)),
        ],
        out_shape=[
            jax.ShapeDtypeStruct((batch, MLA_HEADS, LANES, seq), BF16),
            jax.ShapeDtypeStruct((batch, MLA_HEADS, seq, LANES), BF16),
            jax.ShapeDtypeStruct((batch, MLA_HEADS * MLA_V, seq), BF16),
        ],
        compiler_params=pltpu.CompilerParams(
            dimension_semantics=("parallel",), vmem_limit_bytes=VMEM_LIMIT),
        name="mla_proj",
    )(x2d, pos, g, w_in_p, wkr_t, qg, wq_t, kvg, wk_p, wv_t, freq)


def _attn_kernel(q_ref, k_ref, v_ref, o_ref, acc_ref, m_ref, l_ref, sa_ref, sb_ref, *, tq):
    seq = k_ref.shape[1]
    nq = seq // tq
    causal = (lax.broadcasted_iota(jnp.int32, (tq, tq), 0)
              <= lax.broadcasted_iota(jnp.int32, (tq, tq), 1))

    def scores(dst, blk, q0):
        k0 = pl.multiple_of(blk * tq, tq)
        for h in range(2):
            dst[h] = _dot(k_ref[h, pl.ds(k0, tq), :], q_ref[h, :, pl.ds(q0, tq)])

    def softmax_pv(src, blk, masked):
        k0 = pl.multiple_of(blk * tq, tq)
        for h in range(2):
            s = src[h]
            if masked:
                s = jnp.where(causal, s, -jnp.inf)
            m = m_ref[h]
            m_new = jnp.maximum(m, jnp.max(s, axis=0, keepdims=True))
            p = jnp.exp2(s - m_new)
            alpha = jnp.exp2(m - m_new)
            l_ref[h] = alpha * l_ref[h] + jnp.sum(p, axis=0, keepdims=True)
            m_ref[h] = m_new
            v_t = v_ref[h * MLA_V:(h + 1) * MLA_V, pl.ds(k0, tq)]
            acc_ref[h] = alpha * acc_ref[h] + _dot(v_t, p.astype(BF16))

    def q_block(qi, _):
        q0 = pl.multiple_of(qi * tq, tq)
        acc_ref[...] = jnp.zeros_like(acc_ref)
        m_ref[...] = jnp.full_like(m_ref, -jnp.inf)
        l_ref[...] = jnp.zeros_like(l_ref)

        scores(sa_ref, 0, q0)

        def pair(jp, _):
            scores(sb_ref, 2 * jp + 1, q0)
            softmax_pv(sa_ref, 2 * jp, False)
            scores(sa_ref, 2 * jp + 2, q0)
            softmax_pv(sb_ref, 2 * jp + 1, False)
            return 0

        lax.fori_loop(0, qi // 2, pair, 0)

        @pl.when(qi % 2 == 1)
        def _():
            scores(sb_ref, qi, q0)
            softmax_pv(sa_ref, qi - 1, False)
            softmax_pv(sb_ref, qi, True)

        @pl.when(qi % 2 == 0)
        def _():
            softmax_pv(sa_ref, qi, True)

        out = jnp.concatenate([acc_ref[h] / l_ref[h] for h in range(2)], axis=0)
        o_ref[pl.ds(q0, tq), :] = out.T.astype(BF16)
        return 0

    lax.fori_loop(0, nq, q_block, 0)


def _mla_attn(q_t, k, v_t, *, tq):
    batch, heads, seq, _ = k.shape
    return pl.pallas_call(
        functools.partial(_attn_kernel, tq=tq),
        grid=(batch, heads // 2),
        in_specs=[
            pl.BlockSpec((None, 2, LANES, seq), lambda b, hp: (b, hp, 0, 0)),
            pl.BlockSpec((None, 2, seq, LANES), lambda b, hp: (b, hp, 0, 0)),
            pl.BlockSpec((None, 2 * MLA_V, seq), lambda b, hp: (b, hp, 0)),
        ],
        out_specs=pl.BlockSpec((None, seq, 2 * MLA_V), lambda b, hp: (b, 0, hp)),
        out_shape=jax.ShapeDtypeStruct((batch, seq, heads * MLA_V), BF16),
        scratch_shapes=[pltpu.VMEM((2, MLA_V, tq), F32),
                        pltpu.VMEM((2, 1, tq), F32), pltpu.VMEM((2, 1, tq), F32),
                        pltpu.VMEM((2, tq, tq), F32), pltpu.VMEM((2, tq, tq), F32)],
        compiler_params=pltpu.CompilerParams(
            dimension_semantics=("parallel", "parallel"), vmem_limit_bytes=VMEM_LIMIT),
        name="mla_attn",
    )(q_t, k, v_t)


def _attn_ffn_kernel(x_ref, o_ref, wo_ref, g_ref, wg_ref, wu_ref, wd_ref, out_ref, hn_ref):
    j = pl.program_id(1)

    @pl.when(j == 0)
    def _():
        h1 = x_ref[...] + _dot(o_ref[...], wo_ref[...])
        out_ref[...] = h1
        hn_ref[...] = _rms(h1, g_ref[...]).astype(BF16)

    hn = hn_ref[...]
    a = _silu(_dot(hn, wg_ref[...])) * _dot(hn, wu_ref[...])
    out_ref[...] += _dot(a.astype(BF16), wd_ref[...])


def _attn_ffn(x2d, o, w_o, g, w_gate, w_up, w_down, *, tm, tf):
    t, d = x2d.shape
    f = w_gate.shape[1]
    return pl.pallas_call(
        _attn_ffn_kernel,
        grid=(t // tm, f // tf),
        in_specs=[
            pl.BlockSpec((tm, d), lambda i, j: (i, 0)),
            pl.BlockSpec((tm, o.shape[1]), lambda i, j: (i, 0)),
            pl.BlockSpec(w_o.shape, lambda i, j: (0, 0)),
            pl.BlockSpec((1, d), lambda i, j: (0, 0)),
            pl.BlockSpec((d, tf), lambda i, j: (0, j)),
            pl.BlockSpec((d, tf), lambda i, j: (0, j)),
            pl.BlockSpec((tf, d), lambda i, j: (j, 0)),
        ],
        out_specs=pl.BlockSpec((tm, d), lambda i, j: (i, 0)),
        out_shape=jax.ShapeDtypeStruct((t, d), F32),
        scratch_shapes=[pltpu.VMEM((tm, d), BF16)],
        compiler_params=pltpu.CompilerParams(
            dimension_semantics=("parallel", "arbitrary"), vmem_limit_bytes=VMEM_LIMIT),
        name="attn_ffn",
    )(x2d, o, w_o, g, w_gate, w_up, w_down)


def _hgrn_kernel(h_ref, g_ref, win_ref, lbraw_ref, og_ref, wo_ref, fg_ref, router_ref,
                 h_out, hn_out, meta_out, cnt_out, st_ref, o_scr, cnt_ref, *, layer):
    tt = h_ref.shape[0]
    width = HG_HEADS * HG_DK
    nchunk = tt // HG_CHUNK

    @pl.when(pl.program_id(1) == 0)
    def _():
        st_ref[...] = jnp.zeros_like(st_ref)

    lbraw = lbraw_ref[...]
    e = jnp.exp(lbraw - jnp.max(lbraw, axis=0, keepdims=True))
    sm = e / jnp.sum(e, axis=0, keepdims=True)
    lb = jnp.sum(sm[1:layer + 1], axis=0, keepdims=True)

    h_in = h_ref[...]
    hn = _rms(h_in, g_ref[...]).astype(BF16)
    proj = _dot(hn, win_ref[...])
    q_all = proj[:, :width] * (HG_DK ** -0.5)
    f_all = lb + (1.0 - lb) * (1.0 / (1.0 + jnp.exp(-proj[:, width:2 * width])))
    logf = jnp.log(f_all)
    k_all = 1.0 - f_all

    r = lax.broadcasted_iota(jnp.int32, (tt, tt), 0)
    c = lax.broadcasted_iota(jnp.int32, (tt, tt), 1)
    shift = HG_CHUNK.bit_length() - 1
    tril = (c <= r) & ((r >> shift) == (c >> shift))
    tril_b = jnp.where(tril, 1.0, 0.0).astype(BF16)
    hi, mid, lo = _split3(logf)
    b_all = _dot(tril_b, hi) + _dot(tril_b, mid) + _dot(tril_b, lo)

    for h in range(HG_HEADS):
        sl = slice(h * HG_DK, (h + 1) * HG_DK)
        q = q_all[:, sl]
        k = k_all[:, sl]
        b = b_all[:, sl]
        v = proj[:, 2 * width + h * HG_DK:2 * width + (h + 1) * HG_DK]
        gt = proj[:, 3 * width + h * HG_DK:3 * width + (h + 1) * HG_DK]
        v_b = v.astype(BF16)
        qd = (q * jnp.exp(b)).astype(BF16)
        kd = (k * jnp.exp(-b)).astype(BF16)
        a = jnp.where(tril, _dot_nt(qd, kd), 0.0)
        o = _dot(a.astype(BF16), v_b)

        st = st_ref[h]
        inter = []
        for n in range(nchunk):
            rows = slice(n * HG_CHUNK, (n + 1) * HG_CHUNK)
            b_n = b[rows]
            b_last = b_n[HG_CHUNK - 1:HG_CHUNK]
            inter.append(_dot_nt(qd[rows], st.astype(BF16)))
            kdl = (k[rows] * jnp.exp(b_last - b_n)).astype(BF16)
            st = st * jnp.exp(b_last) + _dot_tn(v_b[rows], kdl)
        st_ref[h] = st
        o = o + jnp.concatenate(inter, axis=0)
        o = _rms(o, og_ref[...]) * _silu(gt)
        o_scr[:, sl] = o.astype(BF16)

    h_new = h_in + _dot(o_scr[...], wo_ref[...])
    h_out[...] = h_new
    hn2 = _rms(h_new, fg_ref[...])
    _store_rows_as_tiles(hn_out, hn2)

    x_hi, x_mid, _ = _split3(hn2)
    router = router_ref[...]
    r_hi = router.astype(BF16)
    r_lo = (router - r_hi.astype(F32)).astype(BF16)
    logits = _dot(x_hi, r_hi) + _dot(x_mid, r_hi) + _dot(x_hi, r_lo)
    lane = lax.broadcasted_iota(jnp.int32, logits.shape, 1)
    neg = -jnp.inf
    logits = jnp.where(lane < N_EXPERTS, logits, neg)
    m1 = jnp.max(logits, axis=-1, keepdims=True)
    i1 = jnp.min(jnp.where(logits == m1, lane, LANES), axis=-1, keepdims=True)
    rest = jnp.where(lane == i1, neg, logits)
    m2 = jnp.max(rest, axis=-1, keepdims=True)
    i2 = jnp.min(jnp.where(rest == m2, lane, LANES), axis=-1, keepdims=True)
    e2 = jnp.exp(m2 - m1)
    w1 = 1.0 / (1.0 + e2)
    w2 = e2 / (1.0 + e2)

    @pl.when((pl.program_id(0) == 0) & (pl.program_id(1) == 0))
    def _():
        cnt_ref[...] = jnp.zeros_like(cnt_ref)

    onehot = jnp.where((lane == i1) | (lane == i2), 1.0, 0.0)
    earlier = jnp.where(c < r, 1.0, 0.0).astype(BF16)
    rank = _dot(earlier, onehot.astype(BF16)) + cnt_ref[...]
    cnt = cnt_ref[...] + jnp.sum(onehot, axis=0, keepdims=True)
    cnt_ref[...] = cnt
    cnt_out[...] = jnp.broadcast_to(cnt, cnt_out.shape)
    rank1 = jnp.sum(jnp.where(lane == i1, rank, 0.0), axis=-1, keepdims=True)
    rank2 = jnp.sum(jnp.where(lane == i2, rank, 0.0), axis=-1, keepdims=True)
    fields = (i1.astype(F32), i2.astype(F32), w1, w2, rank1, rank2)
    meta = jnp.zeros(logits.shape, F32)
    for idx, val in enumerate(fields):
        meta = jnp.where(lane == idx, val, meta)
    meta_out[...] = meta


def _hgrn(h2d, g, w_in, lb_raw, og, w_o, fg, router_p, *, batch, seq, tt, layer):
    t, d = h2d.shape
    nt = seq // tt
    width = HG_HEADS * HG_DK
    tok = lambda cols: pl.BlockSpec((tt, cols), lambda b, i: (b * nt + i, 0))
    const = lambda shape: pl.BlockSpec(shape, lambda b, i: (0,) * len(shape))
    return pl.pallas_call(
        functools.partial(_hgrn_kernel, layer=layer),
        grid=(batch, nt),
        in_specs=[tok(d), const((1, d)), const(w_in.shape), const(lb_raw.shape),
                  const((1, HG_DK)), const(w_o.shape), const((1, d)), const(router_p.shape)],
        out_specs=[tok(d),
                   pl.BlockSpec((tt * (d // LANES), LANES), lambda b, i: (b * nt + i, 0)),
                   tok(LANES), const((SUBLANES, LANES))],
        out_shape=[jax.ShapeDtypeStruct((t, d), F32),
                   jax.ShapeDtypeStruct((t * (d // LANES), LANES), F32),
                   jax.ShapeDtypeStruct((t, LANES), F32),
                   jax.ShapeDtypeStruct((SUBLANES, LANES), F32)],
        scratch_shapes=[pltpu.VMEM((HG_HEADS, HG_DK, HG_DK), F32),
                        pltpu.VMEM((tt, width), BF16),
                        pltpu.VMEM((1, LANES), F32)],
        compiler_params=pltpu.CompilerParams(
            dimension_semantics=("arbitrary", "arbitrary"), vmem_limit_bytes=VMEM_LIMIT),
        name="hgrn",
    )(h2d, g, w_in, lb_raw, og, w_o, fg, router_p)


def _expert_kernel(src_ref, te_ref, tv_ref, x_hbm, wg_ref, wu_ref, wd_ref, y_ref,
                   xbuf, xb_ref, acc_ref, sems):
    del te_ref
    r = pl.program_id(0)
    j = pl.program_id(1)
    n_tiles = pl.num_programs(0)
    nj = pl.num_programs(1)
    tm = xb_ref.shape[0]
    nc = xb_ref.shape[1] // LANES
    part = tm // nj
    valid = tv_ref[r] != 0
    has_next = r + 1 < n_tiles

    def row_copy(tile, row):
        slot = tile % 2
        src = pl.multiple_of(src_ref[tile * tm + row] * nc, nc)
        return pltpu.make_async_copy(x_hbm.at[pl.ds(src, nc)],
                                     xbuf.at[slot, pl.ds(row * nc, nc)], sems.at[slot])

    def swiglu_step():
        x = xb_ref[...]
        a = _silu(_dot(x, wg_ref[...])) * _dot(x, wu_ref[...])
        acc_ref[...] += _dot(a.astype(BF16), wd_ref[...])

    @pl.when((r == 0) & (j == 0))
    def _():
        lax.fori_loop(0, tm, lambda i, c: (row_copy(r, i).start(), c)[1], 0)

    @pl.when((j == 0) & ((r == 0) | (tv_ref[jnp.maximum(r - 1, 0)] != 0)))
    def _():
        pltpu.make_async_copy(x_hbm.at[pl.ds(0, tm * nc)], xbuf.at[r % 2],
                              sems.at[r % 2]).wait()
        xb_ref[...] = _load_rows_from_tiles(xbuf.at[r % 2], nc).astype(BF16)

    @pl.when(j == 0)
    def _():
        acc_ref[...] = jnp.zeros_like(acc_ref)

    @pl.when(valid & has_next)
    def _():
        for i in range(part):
            row_copy(r + 1, j * part + i).start()
        swiglu_step()

    @pl.when(valid & jnp.logical_not(has_next))
    def _():
        swiglu_step()

    @pl.when(j == nj - 1)
    def _():
        _store_rows_as_tiles(y_ref, acc_ref[...])


def _experts(src_rows, tile_expert, tile_valid, x, w_gate, w_up, w_down, *, tm, tf):
    n_rows = src_rows.shape[0]
    d = w_gate.shape[1]
    nc = d // LANES
    f = w_gate.shape[2]
    nj = f // tf
    col = lambda r, j, tv: jnp.where(tv[r] != 0, j, nj - 1)
    return pl.pallas_call(
        _expert_kernel,
        grid_spec=pltpu.PrefetchScalarGridSpec(
            num_scalar_prefetch=3,
            grid=(n_rows // tm, nj),
            in_specs=[
                pl.BlockSpec(memory_space=pl.ANY),
                pl.BlockSpec((None, d, tf), lambda r, j, s, te, tv: (te[r], 0, col(r, j, tv))),
                pl.BlockSpec((None, d, tf), lambda r, j, s, te, tv: (te[r], 0, col(r, j, tv))),
                pl.BlockSpec((None, tf, d), lambda r, j, s, te, tv: (te[r], col(r, j, tv), 0)),
            ],
            out_specs=pl.BlockSpec((tm * nc, LANES), lambda r, j, s, te, tv: (r, 0)),
            scratch_shapes=[pltpu.VMEM((2, tm * nc, LANES), F32), pltpu.VMEM((tm, d), BF16),
                            pltpu.VMEM((tm, d), F32), pltpu.SemaphoreType.DMA((2,))],
        ),
        out_shape=jax.ShapeDtypeStruct((n_rows * nc, LANES), F32),
        compiler_params=pltpu.CompilerParams(
            dimension_semantics=("arbitrary", "arbitrary"), vmem_limit_bytes=VMEM_LIMIT),
        name="moe_experts",
    )(src_rows, tile_expert, tile_valid, x, w_gate, w_up, w_down)


def _combine_kernel(d1_ref, d2_ref, y_hbm, h_ref, meta_ref, fg_ref, out_ref, a_ref, b_ref, sem):
    tc = h_ref.shape[0]
    base = pl.program_id(0) * tc

    nc = h_ref.shape[1] // LANES

    def issue(r, _):
        for d_ref, buf in ((d1_ref, a_ref), (d2_ref, b_ref)):
            src = pl.multiple_of(d_ref[base + r] * nc, nc)
            pltpu.make_async_copy(y_hbm.at[pl.ds(src, nc)], buf.at[pl.ds(r * nc, nc)],
                                  sem).start()
        return 0

    lax.fori_loop(0, tc, issue, 0, unroll=16)
    pltpu.make_async_copy(y_hbm.at[pl.ds(0, tc * nc)], a_ref, sem).wait()
    pltpu.make_async_copy(y_hbm.at[pl.ds(0, tc * nc)], b_ref, sem).wait()
    meta = meta_ref[...]
    w1 = meta[:, 2:3]
    w2 = meta[:, 3:4]
    moe = w1 * _load_rows_from_tiles(a_ref, nc) + w2 * _load_rows_from_tiles(b_ref, nc)
    out_ref[...] = _rms(h_ref[...] + moe, fg_ref[...])


def _combine(dest1, dest2, y, h2d, meta, fg, *, tc):
    t, d = h2d.shape
    return pl.pallas_call(
        _combine_kernel,
        grid_spec=pltpu.PrefetchScalarGridSpec(
            num_scalar_prefetch=2,
            grid=(t // tc,),
            in_specs=[
                pl.BlockSpec(memory_space=pl.ANY),
                pl.BlockSpec((tc, d), lambda i, d1, d2: (i, 0)),
                pl.BlockSpec((tc, LANES), lambda i, d1, d2: (i, 0)),
                pl.BlockSpec((1, d), lambda i, d1, d2: (0, 0)),
            ],
            out_specs=pl.BlockSpec((tc, d), lambda i, d1, d2: (i, 0)),
            scratch_shapes=[pltpu.VMEM((tc * d // LANES, LANES), F32),
                            pltpu.VMEM((tc * d // LANES, LANES), F32),
                            pltpu.SemaphoreType.DMA(())],
        ),
        out_shape=jax.ShapeDtypeStruct((t, d), F32),
        compiler_params=pltpu.CompilerParams(
            dimension_semantics=("arbitrary",), vmem_limit_bytes=VMEM_LIMIT),
        name="moe_combine",
    )(dest1, dest2, y, h2d, meta, fg)


def _routing_plan(meta, counts, *, tm, n_tiles):
    e1 = meta[:, 0].astype(jnp.int32)
    e2 = meta[:, 1].astype(jnp.int32)
    cnt = counts[0, :N_EXPERTS].astype(jnp.int32)
    padded = (cnt + tm - 1) // tm * tm
    ends = jnp.cumsum(padded)
    offs = ends - padded
    dest1 = offs[e1] + meta[:, 4].astype(jnp.int32)
    dest2 = offs[e2] + meta[:, 5].astype(jnp.int32)
    start = jnp.arange(n_tiles, dtype=jnp.int32) * tm
    tile_valid = (start < ends[-1]).astype(jnp.int32)
    last = jnp.sum((ends < ends[-1]).astype(jnp.int32))
    passed = jnp.sum((ends[None, :] <= start[:, None]).astype(jnp.int32), axis=1)
    tok = jnp.arange(meta.shape[0], dtype=jnp.int32)
    src_rows = jnp.zeros((n_tiles * tm,), jnp.int32).at[jnp.concatenate([dest1, dest2])].set(
        jnp.concatenate([tok, tok]), unique_indices=True)
    return dest1, dest2, src_rows, jnp.minimum(passed, last), tile_valid


def _pad_heads(w, head_dim):
    k = w.shape[0]
    w = w.reshape(k, MLA_HEADS, head_dim)
    w = jnp.pad(w, ((0, 0), (0, 0), (0, LANES - head_dim)))
    return w.reshape(k, MLA_HEADS * LANES)


def _pick(n, cap):
    t = min(n, cap)
    while n % t:
        t //= 2
    return t


def kernel(x, positions, mix_norm, ffn_norm, final_norm, mla_w_in, mla_q_norm, mla_w_q_up,
           mla_kv_norm, mla_w_kv_up, mla_w_o, hgrn_w_in, hgrn_lower_bounds, hgrn_out_norm,
           hgrn_w_o, ffn_w_gate, ffn_w_up, ffn_w_down, moe_router, moe_w_gate, moe_w_up,
           moe_w_down):
    batch, seq, d = x.shape
    t = batch * seq
    x2d = x.reshape(t, d)
    row = lambda v: v.reshape(1, -1)

    w_in = mla_w_in[0]
    n_lat = MLA_Q_LORA + MLA_KV_LORA
    w_in_p = w_in[:, :n_lat].astype(BF16)
    wkr_t = jnp.pad(w_in[:, n_lat:].T,
                    ((MLA_NOPE, LANES - MLA_NOPE - MLA_ROPE), (0, 0))).astype(BF16)
    wq_t = _pad_heads(mla_w_q_up[0], MLA_NOPE + MLA_ROPE).T.astype(BF16)
    w_kv = mla_w_kv_up[0].reshape(MLA_KV_LORA, MLA_HEADS, MLA_NOPE + MLA_V)
    wk_p = _pad_heads(w_kv[:, :, :MLA_NOPE].reshape(MLA_KV_LORA, -1), MLA_NOPE).astype(BF16)
    wv_t = w_kv[:, :, MLA_NOPE:].reshape(MLA_KV_LORA, -1).T.astype(BF16)
    inv_freq = ROPE_THETA ** (-jnp.arange(0, MLA_ROPE, 2, dtype=F32) / MLA_ROPE)
    freq = inv_freq.reshape(MLA_ROPE // 2, 1)
    pos = positions.astype(F32).reshape(1, t)

    q_t, k, v_t = _mla_proj(x2d, pos, row(mix_norm[0]), w_in_p, wkr_t, row(mla_q_norm[0]), wq_t,
                            row(mla_kv_norm[0]), wk_p, wv_t, freq,
                            batch=batch, seq=seq, tm=_pick(seq, 512))
    o = _mla_attn(q_t, k, v_t, tq=_pick(seq, 512)).reshape(t, MLA_HEADS * MLA_V)
    h = _attn_ffn(x2d, o, mla_w_o[0].astype(BF16), row(ffn_norm[0]),
                  ffn_w_gate[0].astype(BF16), ffn_w_up[0].astype(BF16),
                  ffn_w_down[0].astype(BF16), tm=_pick(t, 1024), tf=896)

    router_p = jnp.pad(moe_router[0], ((0, 0), (0, LANES - N_EXPERTS)))
    h, hn, meta, counts = _hgrn(h, row(mix_norm[1]), hgrn_w_in[0].astype(BF16),
                                hgrn_lower_bounds, row(hgrn_out_norm[0]),
                                hgrn_w_o[0].astype(BF16), row(ffn_norm[1]), router_p,
                                batch=batch, seq=seq, tt=_pick(seq, 256), layer=1)
    tm = _pick(t, 512)
    n_tiles = 2 * t // tm + N_EXPERTS
    dest1, dest2, src_rows, tile_expert, tile_valid = _routing_plan(
        meta, counts, tm=tm, n_tiles=n_tiles)
    y = _experts(src_rows, tile_expert, tile_valid, hn, moe_w_gate[0].astype(BF16),
                 moe_w_up[0].astype(BF16), moe_w_down[0].astype(BF16), tm=tm, tf=1792)
    out = _combine(dest1, dest2, y, h, meta, row(final_norm), tc=_pick(t, 512))
    return out.reshape(batch, seq, d)
```

```python
import functools

import jax
import jax.numpy as jnp
from jax import lax
from jax.experimental import pallas as pl
from jax.experimental.pallas import tpu as pltpu

EPS = 1e-6
LANES = 128
SUBLANES = 8

MLA_HEADS = 16
MLA_Q_LORA = 384
MLA_KV_LORA = 256
MLA_NOPE = 64
MLA_ROPE = 32
MLA_V = 64
ROPE_THETA = 10000.0

HG_HEADS = 8
HG_DK = 128
HG_CHUNK = 64

N_EXPERTS = 8

VMEM_LIMIT = 56 * 1024 * 1024

BF16 = jnp.bfloat16
F32 = jnp.float32


def _dot(a, b):
    return jnp.dot(a, b, preferred_element_type=F32)


def _dot_nt(a, b):
    return lax.dot_general(a, b, (((1,), (1,)), ((), ())), preferred_element_type=F32)


def _dot_tn(a, b):
    return lax.dot_general(a, b, (((0,), (0,)), ((), ())), preferred_element_type=F32)


def _rms(x, g):
    return x * lax.rsqrt(jnp.mean(x * x, axis=-1, keepdims=True) + EPS) * g


def _silu(x):
    return x * (1.0 / (1.0 + jnp.exp(-x)))


def _store_rows_as_tiles(ref, x):
    n, d = x.shape
    nc = d // LANES
    for c in range(nc):
        ref[pl.ds(c, n, stride=nc), :] = x[:, c * LANES:(c + 1) * LANES]


def _load_rows_from_tiles(ref, nc):
    n = ref.shape[0] // nc
    return jnp.concatenate([ref[pl.ds(c, n, stride=nc), :] for c in range(nc)], axis=1)


def _split3(x):
    hi = x.astype(BF16)
    r = x - hi.astype(F32)
    mid = r.astype(BF16)
    lo = (r - mid.astype(F32)).astype(BF16)
    return hi, mid, lo


def _mla_proj_kernel(x_ref, pos_ref, g_ref, win_ref, wkr_ref, qg_ref, wq_ref, kvg_ref, wk_ref,
                     wv_ref, freq_ref, q_out, k_out, v_out, *, scale):
    half = MLA_ROPE // 2
    x1_rows = slice(MLA_NOPE, MLA_NOPE + half)
    x2_rows = slice(MLA_NOPE + half, MLA_NOPE + MLA_ROPE)
    hn = _rms(x_ref[...], g_ref[...]).astype(BF16)
    proj = _dot(hn, win_ref[...])
    cqn = _rms(proj[:, :MLA_Q_LORA], qg_ref[...]).astype(BF16)
    ckvn = _rms(proj[:, MLA_Q_LORA:], kvg_ref[...]).astype(BF16)

    ang = freq_ref[...] * pos_ref[...]
    cos = jnp.cos(ang)
    sin = jnp.sin(ang)

    def rope_t(slot, mul):
        x1 = slot[x1_rows]
        x2 = slot[x2_rows]
        return jnp.concatenate([slot[:MLA_NOPE] * mul, (x1 * cos - x2 * sin) * mul,
                                (x2 * cos + x1 * sin) * mul, slot[MLA_NOPE + MLA_ROPE:]], axis=0)

    kr_t = _dot_nt(wkr_ref[...], hn)
    kr_roped = rope_t(kr_t, 1.0).T
    q_t = _dot_nt(wq_ref[...], cqn)
    k_pad = _dot(ckvn, wk_ref[...])
    for h in range(MLA_HEADS):
        sl = slice(h * LANES, (h + 1) * LANES)
        q_out[h] = rope_t(q_t[sl], scale).astype(BF16)
        k_out[h] = (k_pad[:, sl] + kr_roped).astype(BF16)
    v_out[...] = _dot_nt(wv_ref[...], ckvn).astype(BF16)


def _mla_proj(x2d, pos, g, w_in_p, wkr_t, qg, wq_t, kvg, wk_p, wv_t, freq, *, batch, seq, tm):
    t = x2d.shape[0]
    d = x2d.shape[1]
    nt = seq // tm
    const = lambda shape: pl.BlockSpec(shape, lambda i: (0,) * len(shape))
    scale = (MLA_NOPE + MLA_ROPE) ** -0.5 * 1.4426950408889634
    return pl.pallas_call(
        functools.partial(_mla_proj_kernel, scale=scale),
        grid=(t // tm,),
        in_specs=[
            pl.BlockSpec((tm, d), lambda i: (i, 0)),
            pl.BlockSpec((1, tm), lambda i: (0, i)),
            const((1, d)),
            const(w_in_p.shape),
            const(wkr_t.shape),
            const((1, MLA_Q_LORA)),
            const(wq_t.shape),
            const((1, MLA_KV_LORA)),
            const(wk_p.shape),
            const(wv_t.shape),
            const(freq.shape),
        ],
        out_specs=[
            pl.BlockSpec((None, MLA_HEADS, LANES, tm), lambda i: (i // nt, 0, 0, i % nt)),
            pl.BlockSpec((None, MLA_HEADS, tm, LANES), lambda i: (i // nt, 0, i % nt, 0)),
            pl.BlockSpec((None, MLA_HEADS * MLA_V, tm), lambda i: (i // nt, 0, i % nt)),
        ],
        out_shape=[
            jax.ShapeDtypeStruct((batch, MLA_HEADS, LANES, seq), BF16),
            jax.ShapeDtypeStruct((batch, MLA_HEADS, seq, LANES), BF16),
            jax.ShapeDtypeStruct((batch, MLA_HEADS * MLA_V, seq), BF16),
        ],
        compiler_params=pltpu.CompilerParams(
            dimension_semantics=("parallel",), vmem_limit_bytes=VMEM_LIMIT),
        name="mla_proj",
    )(x2d, pos, g, w_in_p, wkr_t, qg, wq_t, kvg, wk_p, wv_t, freq)


def _attn_kernel(q_ref, k_ref, v_ref, o_ref, acc_ref, m_ref, vx_ref, sa_ref, sb_ref, *, tq):
    seq = k_ref.shape[1]
    nq = seq // tq
    for h in range(2):
        vx_ref[h, :MLA_V] = v_ref[h * MLA_V:(h + 1) * MLA_V]
        vx_ref[h, MLA_V:] = jnp.ones((vx_ref.shape[1] - MLA_V, seq), BF16)
    causal = (lax.broadcasted_iota(jnp.int32, (tq, tq), 0)
              <= lax.broadcasted_iota(jnp.int32, (tq, tq), 1))

    def scores(dst, blk, q0):
        k0 = pl.multiple_of(blk * tq, tq)
        for h in range(2):
            dst[h] = _dot(k_ref[h, pl.ds(k0, tq), :], q_ref[h, :, pl.ds(q0, tq)])

    def softmax_pv(src, blk, masked):
        k0 = pl.multiple_of(blk * tq, tq)
        for h in range(2):
            s = src[h]
            if masked:
                s = jnp.where(causal, s, -jnp.inf)
            m = m_ref[h]
            m_new = jnp.maximum(m, jnp.max(s, axis=0, keepdims=True))
            p = jnp.exp2(s - m_new)
            alpha = jnp.exp2(m - m_new)
            m_ref[h] = m_new
            acc_ref[h] = alpha * acc_ref[h] + _dot(vx_ref[h, :, pl.ds(k0, tq)], p.astype(BF16))

    def q_block(qi, _):
        q0 = pl.multiple_of(qi * tq, tq)
        acc_ref[...] = jnp.zeros_like(acc_ref)
        m_ref[...] = jnp.full_like(m_ref, -jnp.inf)

        scores(sa_ref, 0, q0)

        def pair(jp, _):
            scores(sb_ref, 2 * jp + 1, q0)
            softmax_pv(sa_ref, 2 * jp, False)
            scores(sa_ref, 2 * jp + 2, q0)
            softmax_pv(sb_ref, 2 * jp + 1, False)
            return 0

        lax.fori_loop(0, qi // 2, pair, 0)

        @pl.when(qi % 2 == 1)
        def _():
            scores(sb_ref, qi, q0)
            softmax_pv(sa_ref, qi - 1, False)
            softmax_pv(sb_ref, qi, True)

        @pl.when(qi % 2 == 0)
        def _():
            softmax_pv(sa_ref, qi, True)

        out = jnp.concatenate([acc_ref[h, :MLA_V] / acc_ref[h, MLA_V:MLA_V + 1]
                               for h in range(2)], axis=0)
        o_ref[pl.ds(q0, tq), :] = out.T.astype(BF16)
        return 0

    lax.fori_loop(0, nq, q_block, 0)


def _mla_attn(q_t, k, v_t, *, tq):
    batch, heads, seq, _ = k.shape
    return pl.pallas_call(
        functools.partial(_attn_kernel, tq=tq),
        grid=(batch, heads // 2),
        in_specs=[
            pl.BlockSpec((None, 2, LANES, seq), lambda b, hp: (b, hp, 0, 0)),
            pl.BlockSpec((None, 2, seq, LANES), lambda b, hp: (b, hp, 0, 0)),
            pl.BlockSpec((None, 2 * MLA_V, seq), lambda b, hp: (b, hp, 0)),
        ],
        out_specs=pl.BlockSpec((None, seq, 2 * MLA_V), lambda b, hp: (b, 0, hp)),
        out_shape=jax.ShapeDtypeStruct((batch, seq, heads * MLA_V), BF16),
        scratch_shapes=[pltpu.VMEM((2, MLA_V + 16, tq), F32),
                        pltpu.VMEM((2, 1, tq), F32), pltpu.VMEM((2, MLA_V + 16, seq), BF16),
                        pltpu.VMEM((2, tq, tq), F32), pltpu.VMEM((2, tq, tq), F32)],
        compiler_params=pltpu.CompilerParams(
            dimension_semantics=("parallel", "parallel"), vmem_limit_bytes=VMEM_LIMIT),
        name="mla_attn",
    )(q_t, k, v_t)


def _attn_ffn_kernel(x_ref, o_ref, wo_ref, g_ref, wg_ref, wu_ref, wd_ref, out_ref, hn_ref):
    j = pl.program_id(1)

    @pl.when(j == 0)
    def _():
        h1 = x_ref[...] + _dot(o_ref[...], wo_ref[...])
        out_ref[...] = h1
        hn_ref[...] = _rms(h1, g_ref[...]).astype(BF16)

    hn = hn_ref[...]
    a = _silu(_dot(hn, wg_ref[...])) * _dot(hn, wu_ref[...])
    out_ref[...] += _dot(a.astype(BF16), wd_ref[...])


def _attn_ffn(x2d, o, w_o, g, w_gate, w_up, w_down, *, tm, tf):
    t, d = x2d.shape
    f = w_gate.shape[1]
    return pl.pallas_call(
        _attn_ffn_kernel,
        grid=(t // tm, f // tf),
        in_specs=[
            pl.BlockSpec((tm, d), lambda i, j: (i, 0)),
            pl.BlockSpec((tm, o.shape[1]), lambda i, j: (i, 0)),
            pl.BlockSpec(w_o.shape, lambda i, j: (0, 0)),
            pl.BlockSpec((1, d), lambda i, j: (0, 0)),
            pl.BlockSpec((d, tf), lambda i, j: (0, j)),
            pl.BlockSpec((d, tf), lambda i, j: (0, j)),
            pl.BlockSpec((tf, d), lambda i, j: (j, 0)),
        ],
        out_specs=pl.BlockSpec((tm, d), lambda i, j: (i, 0)),
        out_shape=jax.ShapeDtypeStruct((t, d), F32),
        scratch_shapes=[pltpu.VMEM((tm, d), BF16)],
        compiler_params=pltpu.CompilerParams(
            dimension_semantics=("parallel", "arbitrary"), vmem_limit_bytes=VMEM_LIMIT),
        name="attn_ffn",
    )(x2d, o, w_o, g, w_gate, w_up, w_down)


def _hgrn_kernel(h_ref, g_ref, win_ref, lbraw_ref, og_ref, wo_ref, fg_ref, router_ref,
                 h_out, hn_out, meta_out, cnt_out, st_ref, o_scr, cnt_ref, *, layer):
    tt = h_ref.shape[0]
    width = HG_HEADS * HG_DK
    nchunk = tt // HG_CHUNK

    @pl.when(pl.program_id(1) == 0)
    def _():
        st_ref[...] = jnp.zeros_like(st_ref)

    lbraw = lbraw_ref[...]
    e = jnp.exp(lbraw - jnp.max(lbraw, axis=0, keepdims=True))
    sm = e / jnp.sum(e, axis=0, keepdims=True)
    lb = jnp.sum(sm[1:layer + 1], axis=0, keepdims=True)

    h_in = h_ref[...]
    hn = _rms(h_in, g_ref[...]).astype(BF16)
    proj = _dot(hn, win_ref[...])
    q_all = proj[:, :width] * (HG_DK ** -0.5)
    f_all = lb + (1.0 - lb) * (1.0 / (1.0 + jnp.exp(-proj[:, width:2 * width])))
    logf = jnp.log(f_all)
    k_all = 1.0 - f_all

    r = lax.broadcasted_iota(jnp.int32, (tt, tt), 0)
    c = lax.broadcasted_iota(jnp.int32, (tt, tt), 1)
    shift = HG_CHUNK.bit_length() - 1
    tril = (c <= r) & ((r >> shift) == (c >> shift))
    tril_b = jnp.where(tril, 1.0, 0.0).astype(BF16)
    hi, mid, lo = _split3(logf)
    b_all = _dot(tril_b, hi) + _dot(tril_b, mid) + _dot(tril_b, lo)

    for h in range(HG_HEADS):
        sl = slice(h * HG_DK, (h + 1) * HG_DK)
        q = q_all[:, sl]
        k = k_all[:, sl]
        b = b_all[:, sl]
        v = proj[:, 2 * width + h * HG_DK:2 * width + (h + 1) * HG_DK]
        gt = proj[:, 3 * width + h * HG_DK:3 * width + (h + 1) * HG_DK]
        v_b = v.astype(BF16)
        qd = (q * jnp.exp(b)).astype(BF16)
        kd = (k * jnp.exp(-b)).astype(BF16)
        a = jnp.where(tril, _dot_nt(qd, kd), 0.0)
        o = _dot(a.astype(BF16), v_b)

        st = st_ref[h]
        inter = []
        for n in range(nchunk):
            rows = slice(n * HG_CHUNK, (n + 1) * HG_CHUNK)
            b_n = b[rows]
            b_last = b_n[HG_CHUNK - 1:HG_CHUNK]
            inter.append(_dot_nt(qd[rows], st.astype(BF16)))
            kdl = (k[rows] * jnp.exp(b_last - b_n)).astype(BF16)
            st = st * jnp.exp(b_last) + _dot_tn(v_b[rows], kdl)
        st_ref[h] = st
        o = o + jnp.concatenate(inter, axis=0)
        o = _rms(o, og_ref[...]) * _silu(gt)
        o_scr[:, sl] = o.astype(BF16)

    h_new = h_in + _dot(o_scr[...], wo_ref[...])
    h_out[...] = h_new
    hn2 = _rms(h_new, fg_ref[...])
    _store_rows_as_tiles(hn_out, hn2)

    x_hi, x_mid, _ = _split3(hn2)
    router = router_ref[...]
    r_hi = router.astype(BF16)
    r_lo = (router - r_hi.astype(F32)).astype(BF16)
    hi_terms = _dot(x_hi, jnp.concatenate([r_hi, r_lo], axis=1))
    logits = hi_terms[:, :LANES] + _dot(x_mid, r_hi) + hi_terms[:, LANES:]
    lane = lax.broadcasted_iota(jnp.int32, logits.shape, 1)
    neg = -jnp.inf
    logits = jnp.where(lane < N_EXPERTS, logits, neg)
    m1 = jnp.max(logits, axis=-1, keepdims=True)
    i1 = jnp.min(jnp.where(logits == m1, lane, LANES), axis=-1, keepdims=True)
    rest = jnp.where(lane == i1, neg, logits)
    m2 = jnp.max(rest, axis=-1, keepdims=True)
    i2 = jnp.min(jnp.where(rest == m2, lane, LANES), axis=-1, keepdims=True)
    e2 = jnp.exp(m2 - m1)
    w1 = 1.0 / (1.0 + e2)
    w2 = e2 / (1.0 + e2)

    @pl.when((pl.program_id(0) == 0) & (pl.program_id(1) == 0))
    def _():
        cnt_ref[...] = jnp.zeros_like(cnt_ref)

    onehot = jnp.where((lane == i1) | (lane == i2), 1.0, 0.0)
    earlier = jnp.where(c < r, 1.0, 0.0).astype(BF16)
    rank = _dot(earlier, onehot.astype(BF16)) + cnt_ref[...]
    cnt = cnt_ref[...] + jnp.sum(onehot, axis=0, keepdims=True)
    cnt_ref[...] = cnt
    cnt_out[...] = jnp.broadcast_to(cnt, cnt_out.shape)
    rank1 = jnp.sum(jnp.where(lane == i1, rank, 0.0), axis=-1, keepdims=True)
    rank2 = jnp.sum(jnp.where(lane == i2, rank, 0.0), axis=-1, keepdims=True)
    fields = (i1.astype(F32), i2.astype(F32), w1, w2, rank1, rank2)
    meta = jnp.zeros(logits.shape, F32)
    for idx, val in enumerate(fields):
        meta = jnp.where(lane == idx, val, meta)
    meta_out[...] = meta


def _hgrn(h2d, g, w_in, lb_raw, og, w_o, fg, router_p, *, batch, seq, tt, layer):
    t, d = h2d.shape
    nt = seq // tt
    width = HG_HEADS * HG_DK
    tok = lambda cols: pl.BlockSpec((tt, cols), lambda b, i: (b * nt + i, 0))
    const = lambda shape: pl.BlockSpec(shape, lambda b, i: (0,) * len(shape))
    return pl.pallas_call(
        functools.partial(_hgrn_kernel, layer=layer),
        grid=(batch, nt),
        in_specs=[tok(d), const((1, d)), const(w_in.shape), const(lb_raw.shape),
                  const((1, HG_DK)), const(w_o.shape), const((1, d)), const(router_p.shape)],
        out_specs=[tok(d),
                   pl.BlockSpec((tt * (d // LANES), LANES), lambda b, i: (b * nt + i, 0)),
                   tok(LANES), const((SUBLANES, LANES))],
        out_shape=[jax.ShapeDtypeStruct((t, d), F32),
                   jax.ShapeDtypeStruct((t * (d // LANES), LANES), F32),
                   jax.ShapeDtypeStruct((t, LANES), F32),
                   jax.ShapeDtypeStruct((SUBLANES, LANES), F32)],
        scratch_shapes=[pltpu.VMEM((HG_HEADS, HG_DK, HG_DK), F32),
                        pltpu.VMEM((tt, width), BF16),
                        pltpu.VMEM((1, LANES), F32)],
        compiler_params=pltpu.CompilerParams(
            dimension_semantics=("arbitrary", "arbitrary"), vmem_limit_bytes=VMEM_LIMIT),
        name="hgrn",
    )(h2d, g, w_in, lb_raw, og, w_o, fg, router_p)


def _expert_kernel(src_ref, te_ref, tv_ref, x_hbm, wg_ref, wu_ref, wd_ref, y_ref,
                   xbuf, xb_ref, acc_ref, sems):
    del te_ref
    r = pl.program_id(0)
    j = pl.program_id(1)
    n_tiles = pl.num_programs(0)
    nj = pl.num_programs(1)
    tm = xb_ref.shape[0]
    nc = xb_ref.shape[1] // LANES
    part = tm // nj
    valid = tv_ref[r] != 0
    has_next = r + 1 < n_tiles

    def row_copy(tile, row):
        slot = tile % 2
        src = pl.multiple_of(src_ref[tile * tm + row] * nc, nc)
        return pltpu.make_async_copy(x_hbm.at[pl.ds(src, nc)],
                                     xbuf.at[slot, pl.ds(row * nc, nc)], sems.at[slot])

    def swiglu_step():
        x = xb_ref[...]
        a = _silu(_dot(x, wg_ref[...])) * _dot(x, wu_ref[...])
        acc_ref[...] += _dot(a.astype(BF16), wd_ref[...])

    @pl.when((r == 0) & (j == 0))
    def _():
        lax.fori_loop(0, tm, lambda i, c: (row_copy(r, i).start(), c)[1], 0)

    @pl.when((j == 0) & ((r == 0) | (tv_ref[jnp.maximum(r - 1, 0)] != 0)))
    def _():
        pltpu.make_async_copy(x_hbm.at[pl.ds(0, tm * nc)], xbuf.at[r % 2],
                              sems.at[r % 2]).wait()
        xb_ref[...] = _load_rows_from_tiles(xbuf.at[r % 2], nc).astype(BF16)

    @pl.when(j == 0)
    def _():
        acc_ref[...] = jnp.zeros_like(acc_ref)

    @pl.when(valid & has_next)
    def _():
        for i in range(part):
            row_copy(r + 1, j * part + i).start()
        swiglu_step()

    @pl.when(valid & jnp.logical_not(has_next))
    def _():
        swiglu_step()

    @pl.when(j == nj - 1)
    def _():
        _store_rows_as_tiles(y_ref, acc_ref[...])


def _experts(src_rows, tile_expert, tile_valid, x, w_gate, w_up, w_down, *, tm, tf):
    n_rows = src_rows.shape[0]
    d = w_gate.shape[1]
    nc = d // LANES
    f = w_gate.shape[2]
    nj = f // tf
    col = lambda r, j, tv: jnp.where(tv[r] != 0, j, nj - 1)
    return pl.pallas_call(
        _expert_kernel,
        grid_spec=pltpu.PrefetchScalarGridSpec(
            num_scalar_prefetch=3,
            grid=(n_rows // tm, nj),
            in_specs=[
                pl.BlockSpec(memory_space=pl.ANY),
                pl.BlockSpec((None, d, tf), lambda r, j, s, te, tv: (te[r], 0, col(r, j, tv))),
                pl.BlockSpec((None, d, tf), lambda r, j, s, te, tv: (te[r], 0, col(r, j, tv))),
                pl.BlockSpec((None, tf, d), lambda r, j, s, te, tv: (te[r], col(r, j, tv), 0)),
            ],
            out_specs=pl.BlockSpec((tm * nc, LANES), lambda r, j, s, te, tv: (r, 0)),
            scratch_shapes=[pltpu.VMEM((2, tm * nc, LANES), F32), pltpu.VMEM((tm, d), BF16),
                            pltpu.VMEM((tm, d), F32), pltpu.SemaphoreType.DMA((2,))],
        ),
        out_shape=jax.ShapeDtypeStruct((n_rows * nc, LANES), F32),
        compiler_params=pltpu.CompilerParams(
            dimension_semantics=("arbitrary", "arbitrary"), vmem_limit_bytes=VMEM_LIMIT),
        name="moe_experts",
    )(src_rows, tile_expert, tile_valid, x, w_gate, w_up, w_down)


def _combine_kernel(d1_ref, d2_ref, y_hbm, h_ref, meta_ref, fg_ref, out_ref, a_ref, b_ref, sem):
    tc = h_ref.shape[0]
    base = pl.program_id(0) * tc

    nc = h_ref.shape[1] // LANES

    def issue(r, _):
        for d_ref, buf in ((d1_ref, a_ref), (d2_ref, b_ref)):
            src = pl.multiple_of(d_ref[base + r] * nc, nc)
            pltpu.make_async_copy(y_hbm.at[pl.ds(src, nc)], buf.at[pl.ds(r * nc, nc)],
                                  sem).start()
        return 0

    lax.fori_loop(0, tc, issue, 0, unroll=16)
    pltpu.make_async_copy(y_hbm.at[pl.ds(0, tc * nc)], a_ref, sem).wait()
    pltpu.make_async_copy(y_hbm.at[pl.ds(0, tc * nc)], b_ref, sem).wait()
    meta = meta_ref[...]
    w1 = meta[:, 2:3]
    w2 = meta[:, 3:4]
    moe = w1 * _load_rows_from_tiles(a_ref, nc) + w2 * _load_rows_from_tiles(b_ref, nc)
    out_ref[...] = _rms(h_ref[...] + moe, fg_ref[...])


def _combine(dest1, dest2, y, h2d, meta, fg, *, tc):
    t, d = h2d.shape
    return pl.pallas_call(
        _combine_kernel,
        grid_spec=pltpu.PrefetchScalarGridSpec(
            num_scalar_prefetch=2,
            grid=(t // tc,),
            in_specs=[
                pl.BlockSpec(memory_space=pl.ANY),
                pl.BlockSpec((tc, d), lambda i, d1, d2: (i, 0)),
                pl.BlockSpec((tc, LANES), lambda i, d1, d2: (i, 0)),
                pl.BlockSpec((1, d), lambda i, d1, d2: (0, 0)),
            ],
            out_specs=pl.BlockSpec((tc, d), lambda i, d1, d2: (i, 0)),
            scratch_shapes=[pltpu.VMEM((tc * d // LANES, LANES), F32),
                            pltpu.VMEM((tc * d // LANES, LANES), F32),
                            pltpu.SemaphoreType.DMA(())],
        ),
        out_shape=jax.ShapeDtypeStruct((t, d), F32),
        compiler_params=pltpu.CompilerParams(
            dimension_semantics=("arbitrary",), vmem_limit_bytes=VMEM_LIMIT),
        name="moe_combine",
    )(dest1, dest2, y, h2d, meta, fg)


def _routing_plan(meta, counts, *, tm, n_tiles):
    e1 = meta[:, 0].astype(jnp.int32)
    e2 = meta[:, 1].astype(jnp.int32)
    cnt = counts[0, :N_EXPERTS].astype(jnp.int32)
    padded = (cnt + tm - 1) // tm * tm
    ends = jnp.cumsum(padded)
    offs = ends - padded
    dest1 = offs[e1] + meta[:, 4].astype(jnp.int32)
    dest2 = offs[e2] + meta[:, 5].astype(jnp.int32)
    start = jnp.arange(n_tiles, dtype=jnp.int32) * tm
    tile_valid = (start < ends[-1]).astype(jnp.int32)
    last = jnp.sum((ends < ends[-1]).astype(jnp.int32))
    passed = jnp.sum((ends[None, :] <= start[:, None]).astype(jnp.int32), axis=1)
    src_rows = _source_rows(dest1, dest2, n_tiles * tm)
    return dest1, dest2, src_rows, jnp.minimum(passed, last), tile_valid


def _source_rows_kernel(d1_ref, d2_ref, src_ref):
    def clear(i, _):
        src_ref[i] = 0
        return 0

    def place(t, _):
        src_ref[d1_ref[t]] = t
        src_ref[d2_ref[t]] = t
        return 0

    lax.fori_loop(0, src_ref.shape[0], clear, 0, unroll=16)
    lax.fori_loop(0, d1_ref.shape[0], place, 0, unroll=8)


def _source_rows(dest1, dest2, n_rows):
    return pl.pallas_call(
        _source_rows_kernel,
        in_specs=[pl.BlockSpec(memory_space=pltpu.SMEM), pl.BlockSpec(memory_space=pltpu.SMEM)],
        out_specs=pl.BlockSpec(memory_space=pltpu.SMEM),
        out_shape=jax.ShapeDtypeStruct((n_rows,), jnp.int32),
        name="moe_source_rows",
    )(dest1, dest2)


def _pad_heads(w, head_dim):
    k = w.shape[0]
    w = w.reshape(k, MLA_HEADS, head_dim)
    w = jnp.pad(w, ((0, 0), (0, 0), (0, LANES - head_dim)))
    return w.reshape(k, MLA_HEADS * LANES)


def _pick(n, cap):
    t = min(n, cap)
    while n % t:
        t //= 2
    return t


def kernel(x, positions, mix_norm, ffn_norm, final_norm, mla_w_in, mla_q_norm, mla_w_q_up,
           mla_kv_norm, mla_w_kv_up, mla_w_o, hgrn_w_in, hgrn_lower_bounds, hgrn_out_norm,
           hgrn_w_o, ffn_w_gate, ffn_w_up, ffn_w_down, moe_router, moe_w_gate, moe_w_up,
           moe_w_down):
    batch, seq, d = x.shape
    t = batch * seq
    x2d = x.reshape(t, d)
    row = lambda v: v.reshape(1, -1)

    w_in = mla_w_in[0]
    n_lat = MLA_Q_LORA + MLA_KV_LORA
    w_in_p = w_in[:, :n_lat].astype(BF16)
    wkr_t = jnp.pad(w_in[:, n_lat:].T,
                    ((MLA_NOPE, LANES - MLA_NOPE - MLA_ROPE), (0, 0))).astype(BF16)
    wq_t = _pad_heads(mla_w_q_up[0], MLA_NOPE + MLA_ROPE).T.astype(BF16)
    w_kv = mla_w_kv_up[0].reshape(MLA_KV_LORA, MLA_HEADS, MLA_NOPE + MLA_V)
    wk_p = _pad_heads(w_kv[:, :, :MLA_NOPE].reshape(MLA_KV_LORA, -1), MLA_NOPE).astype(BF16)
    wv_t = w_kv[:, :, MLA_NOPE:].reshape(MLA_KV_LORA, -1).T.astype(BF16)
    inv_freq = ROPE_THETA ** (-jnp.arange(0, MLA_ROPE, 2, dtype=F32) / MLA_ROPE)
    freq = inv_freq.reshape(MLA_ROPE // 2, 1)
    pos = positions.astype(F32).reshape(1, t)

    q_t, k, v_t = _mla_proj(x2d, pos, row(mix_norm[0]), w_in_p, wkr_t, row(mla_q_norm[0]), wq_t,
                            row(mla_kv_norm[0]), wk_p, wv_t, freq,
                            batch=batch, seq=seq, tm=_pick(seq, 512))
    o = _mla_attn(q_t, k, v_t, tq=_pick(seq, 512)).reshape(t, MLA_HEADS * MLA_V)
    h = _attn_ffn(x2d, o, mla_w_o[0].astype(BF16), row(ffn_norm[0]),
                  ffn_w_gate[0].astype(BF16), ffn_w_up[0].astype(BF16),
                  ffn_w_down[0].astype(BF16), tm=_pick(t, 1024), tf=896)

    router_p = jnp.pad(moe_router[0], ((0, 0), (0, LANES - N_EXPERTS)))
    h, hn, meta, counts = _hgrn(h, row(mix_norm[1]), hgrn_w_in[0].astype(BF16),
                                hgrn_lower_bounds, row(hgrn_out_norm[0]),
                                hgrn_w_o[0].astype(BF16), row(ffn_norm[1]), router_p,
                                batch=batch, seq=seq, tt=_pick(seq, 256), layer=1)
    tm = _pick(t, 512)
    n_tiles = 2 * t // tm + N_EXPERTS
    dest1, dest2, src_rows, tile_expert, tile_valid = _routing_plan(
        meta, counts, tm=tm, n_tiles=n_tiles)
    y = _experts(src_rows, tile_expert, tile_valid, hn, moe_w_gate[0].astype(BF16),
                 moe_w_up[0].astype(BF16), moe_w_down[0].astype(BF16), tm=tm, tf=1792)
    out = _combine(dest1, dest2, y, h, meta, row(final_norm), tc=_pick(t, 512))
    return out.reshape(batch, seq, d)
```

```python
import functools

import jax
import jax.numpy as jnp
from jax import lax
from jax.experimental import pallas as pl
from jax.experimental.pallas import tpu as pltpu

EPS = 1e-6
LANES = 128
SUBLANES = 8

MLA_HEADS = 16
MLA_Q_LORA = 384
MLA_KV_LORA = 256
MLA_NOPE = 64
MLA_ROPE = 32
MLA_V = 64
ROPE_THETA = 10000.0

HG_HEADS = 8
HG_DK = 128
HG_CHUNK = 64

N_EXPERTS = 8

VMEM_LIMIT = 56 * 1024 * 1024
DMA_THREADS = 2

BF16 = jnp.bfloat16
F32 = jnp.float32


def _dot(a, b):
    return jnp.dot(a, b, preferred_element_type=F32)


def _dot_nt(a, b):
    return lax.dot_general(a, b, (((1,), (1,)), ((), ())), preferred_element_type=F32)


def _dot_tn(a, b):
    return lax.dot_general(a, b, (((0,), (0,)), ((), ())), preferred_element_type=F32)


def _rms(x, g):
    return x * lax.rsqrt(jnp.mean(x * x, axis=-1, keepdims=True) + EPS) * g


def _silu(x):
    return x * (1.0 / (1.0 + jnp.exp(-x)))


def _store_rows_as_tiles(ref, x):
    n, d = x.shape
    nc = d // LANES
    for c in range(nc):
        ref[pl.ds(c, n, stride=nc), :] = x[:, c * LANES:(c + 1) * LANES]


def _load_rows_from_tiles(ref, nc):
    n = ref.shape[0] // nc
    return jnp.concatenate([ref[pl.ds(c, n, stride=nc), :] for c in range(nc)], axis=1)


def _split3(x):
    hi = x.astype(BF16)
    r = x - hi.astype(F32)
    mid = r.astype(BF16)
    lo = (r - mid.astype(F32)).astype(BF16)
    return hi, mid, lo


def _mla_proj_kernel(x_ref, pos_ref, g_ref, win_ref, wkr_ref, qg_ref, wq_ref, kvg_ref, wk_ref,
                     wv_ref, freq_ref, q_out, k_out, v_out, *, scale):
    half = MLA_ROPE // 2
    x1_rows = slice(MLA_NOPE, MLA_NOPE + half)
    x2_rows = slice(MLA_NOPE + half, MLA_NOPE + MLA_ROPE)
    hn = _rms(x_ref[...], g_ref[...]).astype(BF16)
    proj = _dot(hn, win_ref[...])
    cqn = _rms(proj[:, :MLA_Q_LORA], qg_ref[...]).astype(BF16)
    ckvn = _rms(proj[:, MLA_Q_LORA:], kvg_ref[...]).astype(BF16)

    ang = freq_ref[...] * pos_ref[...]
    cos = jnp.cos(ang)
    sin = jnp.sin(ang)

    def rope_t(slot, mul):
        x1 = slot[x1_rows]
        x2 = slot[x2_rows]
        return jnp.concatenate([slot[:MLA_NOPE] * mul, (x1 * cos - x2 * sin) * mul,
                                (x2 * cos + x1 * sin) * mul, slot[MLA_NOPE + MLA_ROPE:]], axis=0)

    kr_t = _dot_nt(wkr_ref[...], hn)
    kr_roped = rope_t(kr_t, 1.0).T
    q_t = _dot_nt(wq_ref[...], cqn)
    k_pad = _dot(ckvn, wk_ref[...])
    for h in range(MLA_HEADS):
        sl = slice(h * LANES, (h + 1) * LANES)
        q_out[h] = rope_t(q_t[sl], scale).astype(BF16)
        k_out[h] = (k_pad[:, sl] + kr_roped).astype(BF16)
    v_out[...] = _dot_nt(wv_ref[...], ckvn).astype(BF16)


def _mla_proj(x2d, pos, g, w_in_p, wkr_t, qg, wq_t, kvg, wk_p, wv_t, freq, *, batch, seq, tm):
    t = x2d.shape[0]
    d = x2d.shape[1]
    nt = seq // tm
    const = lambda shape: pl.BlockSpec(shape, lambda i: (0,) * len(shape))
    scale = (MLA_NOPE + MLA_ROPE) ** -0.5 * 1.4426950408889634
    return pl.pallas_call(
        functools.partial(_mla_proj_kernel, scale=scale),
        grid=(t // tm,),
        in_specs=[
            pl.BlockSpec((tm, d), lambda i: (i, 0)),
            pl.BlockSpec((1, tm), lambda i: (0, i)),
            const((1, d)),
            const(w_in_p.shape),
            const(wkr_t.shape),
            const((1, MLA_Q_LORA)),
            const(wq_t.shape),
            const((1, MLA_KV_LORA)),
            const(wk_p.shape),
            const(wv_t.shape),
            const(freq.shape),
        ],
        out_specs=[
            pl.BlockSpec((None, MLA_HEADS, LANES, tm), lambda i: (i // nt, 0, 0, i % nt)),
            pl.BlockSpec((None, MLA_HEADS, tm, LANES), lambda i: (i // nt, 0, i % nt, 0)),
            pl.BlockSpec((None, MLA_HEADS * MLA_V, tm), lambda i: (i // nt, 0, i % nt)),
        ],
        out_shape=[
            jax.ShapeDtypeStruct((batch, MLA_HEADS, LANES, seq), BF16),
            jax.ShapeDtypeStruct((batch, MLA_HEADS, seq, LANES), BF16),
            jax.ShapeDtypeStruct((batch, MLA_HEADS * MLA_V, seq), BF16),
        ],
        compiler_params=pltpu.CompilerParams(
            dimension_semantics=("parallel",), vmem_limit_bytes=VMEM_LIMIT),
        name="mla_proj",
    )(x2d, pos, g, w_in_p, wkr_t, qg, wq_t, kvg, wk_p, wv_t, freq)


def _attn_kernel(q_ref, k_ref, v_ref, o_ref, acc_ref, m_ref, vx_ref, sa_ref, sb_ref, *, tq):
    seq = k_ref.shape[1]
    nq = seq // tq
    for h in range(2):
        vx_ref[h, :MLA_V] = v_ref[h * MLA_V:(h + 1) * MLA_V]
        vx_ref[h, MLA_V:] = jnp.ones((vx_ref.shape[1] - MLA_V, seq), BF16)
    causal = (lax.broadcasted_iota(jnp.int32, (tq, tq), 0)
              <= lax.broadcasted_iota(jnp.int32, (tq, tq), 1))

    def scores(dst, blk, q0):
        k0 = pl.multiple_of(blk * tq, tq)
        for h in range(2):
            dst[h] = _dot(k_ref[h, pl.ds(k0, tq), :], q_ref[h, :, pl.ds(q0, tq)])

    def softmax_pv(src, blk, masked):
        k0 = pl.multiple_of(blk * tq, tq)
        for h in range(2):
            s = src[h]
            if masked:
                s = jnp.where(causal, s, -jnp.inf)
            m = m_ref[h]
            m_new = jnp.maximum(m, jnp.max(s, axis=0, keepdims=True))
            p = jnp.exp2(s - m_new)
            alpha = jnp.exp2(m - m_new)
            m_ref[h] = m_new
            acc_ref[h] = alpha * acc_ref[h] + _dot(vx_ref[h, :, pl.ds(k0, tq)], p.astype(BF16))

    def q_block(qi, _):
        q0 = pl.multiple_of(qi * tq, tq)
        acc_ref[...] = jnp.zeros_like(acc_ref)
        m_ref[...] = jnp.full_like(m_ref, -jnp.inf)

        scores(sa_ref, 0, q0)

        def pair(jp, _):
            scores(sb_ref, 2 * jp + 1, q0)
            softmax_pv(sa_ref, 2 * jp, False)
            scores(sa_ref, 2 * jp + 2, q0)
            softmax_pv(sb_ref, 2 * jp + 1, False)
            return 0

        lax.fori_loop(0, qi // 2, pair, 0)

        @pl.when(qi % 2 == 1)
        def _():
            scores(sb_ref, qi, q0)
            softmax_pv(sa_ref, qi - 1, False)
            softmax_pv(sb_ref, qi, True)

        @pl.when(qi % 2 == 0)
        def _():
            softmax_pv(sa_ref, qi, True)

        out = jnp.concatenate([acc_ref[h, :MLA_V] / acc_ref[h, MLA_V:MLA_V + 1]
                               for h in range(2)], axis=0)
        o_ref[pl.ds(q0, tq), :] = out.T.astype(BF16)
        return 0

    lax.fori_loop(0, nq, q_block, 0)


def _mla_attn(q_t, k, v_t, *, tq):
    batch, heads, seq, _ = k.shape
    return pl.pallas_call(
        functools.partial(_attn_kernel, tq=tq),
        grid=(batch, heads // 2),
        in_specs=[
            pl.BlockSpec((None, 2, LANES, seq), lambda b, hp: (b, hp, 0, 0)),
            pl.BlockSpec((None, 2, seq, LANES), lambda b, hp: (b, hp, 0, 0)),
            pl.BlockSpec((None, 2 * MLA_V, seq), lambda b, hp: (b, hp, 0)),
        ],
        out_specs=pl.BlockSpec((None, seq, 2 * MLA_V), lambda b, hp: (b, 0, hp)),
        out_shape=jax.ShapeDtypeStruct((batch, seq, heads * MLA_V), BF16),
        scratch_shapes=[pltpu.VMEM((2, MLA_V + 16, tq), F32),
                        pltpu.VMEM((2, 1, tq), F32), pltpu.VMEM((2, MLA_V + 16, seq), BF16),
                        pltpu.VMEM((2, tq, tq), F32), pltpu.VMEM((2, tq, tq), F32)],
        compiler_params=pltpu.CompilerParams(
            dimension_semantics=("parallel", "parallel"), vmem_limit_bytes=VMEM_LIMIT),
        name="mla_attn",
    )(q_t, k, v_t)


def _attn_ffn_kernel(x_ref, o_ref, wo_ref, g_ref, wg_ref, wu_ref, wd_ref, out_ref, hn_ref):
    j = pl.program_id(1)

    @pl.when(j == 0)
    def _():
        h1 = x_ref[...] + _dot(o_ref[...], wo_ref[...])
        out_ref[...] = h1
        hn_ref[...] = _rms(h1, g_ref[...]).astype(BF16)

    hn = hn_ref[...]
    a = _silu(_dot(hn, wg_ref[...])) * _dot(hn, wu_ref[...])
    out_ref[...] += _dot(a.astype(BF16), wd_ref[...])


def _attn_ffn(x2d, o, w_o, g, w_gate, w_up, w_down, *, tm, tf):
    t, d = x2d.shape
    f = w_gate.shape[1]
    return pl.pallas_call(
        _attn_ffn_kernel,
        grid=(t // tm, f // tf),
        in_specs=[
            pl.BlockSpec((tm, d), lambda i, j: (i, 0)),
            pl.BlockSpec((tm, o.shape[1]), lambda i, j: (i, 0)),
            pl.BlockSpec(w_o.shape, lambda i, j: (0, 0)),
            pl.BlockSpec((1, d), lambda i, j: (0, 0)),
            pl.BlockSpec((d, tf), lambda i, j: (0, j)),
            pl.BlockSpec((d, tf), lambda i, j: (0, j)),
            pl.BlockSpec((tf, d), lambda i, j: (j, 0)),
        ],
        out_specs=pl.BlockSpec((tm, d), lambda i, j: (i, 0)),
        out_shape=jax.ShapeDtypeStruct((t, d), F32),
        scratch_shapes=[pltpu.VMEM((tm, d), BF16)],
        compiler_params=pltpu.CompilerParams(
            dimension_semantics=("parallel", "arbitrary"), vmem_limit_bytes=VMEM_LIMIT),
        name="attn_ffn",
    )(x2d, o, w_o, g, w_gate, w_up, w_down)


def _hgrn_kernel(h_ref, g_ref, win_ref, lbraw_ref, og_ref, wo_ref, fg_ref, router_ref,
                 h_out, hn_out, meta_out, cnt_out, st_ref, o_scr, cnt_ref, *, layer):
    tt = h_ref.shape[0]
    width = HG_HEADS * HG_DK
    nchunk = tt // HG_CHUNK

    @pl.when(pl.program_id(1) == 0)
    def _():
        st_ref[...] = jnp.zeros_like(st_ref)

    lbraw = lbraw_ref[...]
    e = jnp.exp(lbraw - jnp.max(lbraw, axis=0, keepdims=True))
    sm = e / jnp.sum(e, axis=0, keepdims=True)
    lb = jnp.sum(sm[1:layer + 1], axis=0, keepdims=True)

    h_in = h_ref[...]
    hn = _rms(h_in, g_ref[...]).astype(BF16)
    proj = _dot(hn, win_ref[...])
    q_all = proj[:, :width] * (HG_DK ** -0.5)
    f_all = lb + (1.0 - lb) * (1.0 / (1.0 + jnp.exp(-proj[:, width:2 * width])))
    logf = jnp.log(f_all)
    k_all = 1.0 - f_all

    r = lax.broadcasted_iota(jnp.int32, (tt, tt), 0)
    c = lax.broadcasted_iota(jnp.int32, (tt, tt), 1)
    shift = HG_CHUNK.bit_length() - 1
    tril = (c <= r) & ((r >> shift) == (c >> shift))
    tril_b = jnp.where(tril, 1.0, 0.0).astype(BF16)
    hi, mid, _ = _split3(logf)
    b_all = _dot(tril_b, hi) + _dot(tril_b, mid)

    for h in range(HG_HEADS):
        sl = slice(h * HG_DK, (h + 1) * HG_DK)
        q = q_all[:, sl]
        k = k_all[:, sl]
        b = b_all[:, sl]
        v = proj[:, 2 * width + h * HG_DK:2 * width + (h + 1) * HG_DK]
        gt = proj[:, 3 * width + h * HG_DK:3 * width + (h + 1) * HG_DK]
        v_b = v.astype(BF16)
        qd = (q * jnp.exp(b)).astype(BF16)
        kd = (k * jnp.exp(-b)).astype(BF16)
        a = jnp.where(tril, _dot_nt(qd, kd), 0.0)
        o = _dot(a.astype(BF16), v_b)

        st = st_ref[h]
        inter = []
        for n in range(nchunk):
            rows = slice(n * HG_CHUNK, (n + 1) * HG_CHUNK)
            b_n = b[rows]
            b_last = b_n[HG_CHUNK - 1:HG_CHUNK]
            inter.append(_dot_nt(qd[rows], st.astype(BF16)))
            kdl = (k[rows] * jnp.exp(b_last - b_n)).astype(BF16)
            st = st * jnp.exp(b_last) + _dot_tn(v_b[rows], kdl)
        st_ref[h] = st
        o = o + jnp.concatenate(inter, axis=0)
        o = _rms(o, og_ref[...]) * _silu(gt)
        o_scr[:, sl] = o.astype(BF16)

    h_new = h_in + _dot(o_scr[...], wo_ref[...])
    h_out[...] = h_new
    hn2 = _rms(h_new, fg_ref[...])
    _store_rows_as_tiles(hn_out, hn2)

    x_hi, x_mid, _ = _split3(hn2)
    router = router_ref[...]
    r_hi = router.astype(BF16)
    r_lo = (router - r_hi.astype(F32)).astype(BF16)
    hi_terms = _dot(x_hi, jnp.concatenate([r_hi, r_lo], axis=1))
    logits = hi_terms[:, :LANES] + _dot(x_mid, r_hi) + hi_terms[:, LANES:]
    lane = lax.broadcasted_iota(jnp.int32, logits.shape, 1)
    neg = -jnp.inf
    logits = jnp.where(lane < N_EXPERTS, logits, neg)
    m1 = jnp.max(logits, axis=-1, keepdims=True)
    i1 = jnp.min(jnp.where(logits == m1, lane, LANES), axis=-1, keepdims=True)
    rest = jnp.where(lane == i1, neg, logits)
    m2 = jnp.max(rest, axis=-1, keepdims=True)
    i2 = jnp.min(jnp.where(rest == m2, lane, LANES), axis=-1, keepdims=True)
    e2 = jnp.exp(m2 - m1)
    w1 = 1.0 / (1.0 + e2)
    w2 = e2 / (1.0 + e2)

    @pl.when((pl.program_id(0) == 0) & (pl.program_id(1) == 0))
    def _():
        cnt_ref[...] = jnp.zeros_like(cnt_ref)

    onehot = jnp.where((lane == i1) | (lane == i2), 1.0, 0.0)
    earlier = jnp.where(c < r, 1.0, 0.0).astype(BF16)
    rank = _dot(earlier, onehot.astype(BF16)) + cnt_ref[...]
    cnt = cnt_ref[...] + jnp.sum(onehot, axis=0, keepdims=True)
    cnt_ref[...] = cnt
    cnt_out[...] = jnp.broadcast_to(cnt, cnt_out.shape)
    rank1 = jnp.sum(jnp.where(lane == i1, rank, 0.0), axis=-1, keepdims=True)
    rank2 = jnp.sum(jnp.where(lane == i2, rank, 0.0), axis=-1, keepdims=True)
    fields = (i1.astype(F32), i2.astype(F32), w1, w2, rank1, rank2)
    meta = jnp.zeros(logits.shape, F32)
    for idx, val in enumerate(fields):
        meta = jnp.where(lane == idx, val, meta)
    meta_out[...] = meta


def _hgrn(h2d, g, w_in, lb_raw, og, w_o, fg, router_p, *, batch, seq, tt, layer):
    t, d = h2d.shape
    nt = seq // tt
    width = HG_HEADS * HG_DK
    tok = lambda cols: pl.BlockSpec((tt, cols), lambda b, i: (b * nt + i, 0))
    const = lambda shape: pl.BlockSpec(shape, lambda b, i: (0,) * len(shape))
    return pl.pallas_call(
        functools.partial(_hgrn_kernel, layer=layer),
        grid=(batch, nt),
        in_specs=[tok(d), const((1, d)), const(w_in.shape), const(lb_raw.shape),
                  const((1, HG_DK)), const(w_o.shape), const((1, d)), const(router_p.shape)],
        out_specs=[tok(d),
                   pl.BlockSpec((tt * (d // LANES), LANES), lambda b, i: (b * nt + i, 0)),
                   tok(LANES), const((SUBLANES, LANES))],
        out_shape=[jax.ShapeDtypeStruct((t, d), F32),
                   jax.ShapeDtypeStruct((t * (d // LANES), LANES), F32),
                   jax.ShapeDtypeStruct((t, LANES), F32),
                   jax.ShapeDtypeStruct((SUBLANES, LANES), F32)],
        scratch_shapes=[pltpu.VMEM((HG_HEADS, HG_DK, HG_DK), F32),
                        pltpu.VMEM((tt, width), BF16),
                        pltpu.VMEM((1, LANES), F32)],
        compiler_params=pltpu.CompilerParams(
            dimension_semantics=("arbitrary", "arbitrary"), vmem_limit_bytes=VMEM_LIMIT),
        name="hgrn",
    )(h2d, g, w_in, lb_raw, og, w_o, fg, router_p)


def _expert_kernel(src_ref, te_ref, tv_ref, x_hbm, wg_ref, wu_ref, wd_ref, y_ref,
                   xbuf, xb_ref, acc_ref, sems):
    del te_ref
    r = pl.program_id(0)
    j = pl.program_id(1)
    n_tiles = pl.num_programs(0)
    nj = pl.num_programs(1)
    tm = xb_ref.shape[0]
    nc = xb_ref.shape[1] // LANES
    part = tm // nj
    valid = tv_ref[r] != 0
    has_next = r + 1 < n_tiles

    def row_copy(tile, row):
        slot = tile % 2
        src = pl.multiple_of(src_ref[tile * tm + row] * nc, nc)
        return pltpu.make_async_copy(x_hbm.at[pl.ds(src, nc)],
                                     xbuf.at[slot, pl.ds(row * nc, nc)], sems.at[slot])

    def swiglu_step():
        x = xb_ref[...]
        a = _silu(_dot(x, wg_ref[...])) * _dot(x, wu_ref[...])
        acc_ref[...] += _dot(a.astype(BF16), wd_ref[...])

    @pl.when((r == 0) & (j == 0))
    def _():
        lax.fori_loop(0, tm, lambda i, c: (row_copy(r, i).start(), c)[1], 0)

    @pl.when((j == 0) & ((r == 0) | (tv_ref[jnp.maximum(r - 1, 0)] != 0)))
    def _():
        pltpu.make_async_copy(x_hbm.at[pl.ds(0, tm * nc)], xbuf.at[r % 2],
                              sems.at[r % 2]).wait()
        xb_ref[...] = _load_rows_from_tiles(xbuf.at[r % 2], nc).astype(BF16)

    @pl.when(j == 0)
    def _():
        acc_ref[...] = jnp.zeros_like(acc_ref)

    @pl.when(valid & has_next)
    def _():
        for i in range(part):
            row_copy(r + 1, j * part + i).start(priority=i % DMA_THREADS)
        swiglu_step()

    @pl.when(valid & jnp.logical_not(has_next))
    def _():
        swiglu_step()

    @pl.when(j == nj - 1)
    def _():
        _store_rows_as_tiles(y_ref, acc_ref[...])


def _experts(src_rows, tile_expert, tile_valid, x, w_gate, w_up, w_down, *, tm, tf):
    n_rows = src_rows.shape[0]
    d = w_gate.shape[1]
    nc = d // LANES
    f = w_gate.shape[2]
    nj = f // tf
    col = lambda r, j, tv: jnp.where(tv[r] != 0, j, nj - 1)
    return pl.pallas_call(
        _expert_kernel,
        grid_spec=pltpu.PrefetchScalarGridSpec(
            num_scalar_prefetch=3,
            grid=(n_rows // tm, nj),
            in_specs=[
                pl.BlockSpec(memory_space=pl.ANY),
                pl.BlockSpec((None, d, tf), lambda r, j, s, te, tv: (te[r], 0, col(r, j, tv))),
                pl.BlockSpec((None, d, tf), lambda r, j, s, te, tv: (te[r], 0, col(r, j, tv))),
                pl.BlockSpec((None, tf, d), lambda r, j, s, te, tv: (te[r], col(r, j, tv), 0)),
            ],
            out_specs=pl.BlockSpec((tm * nc, LANES), lambda r, j, s, te, tv: (r, 0)),
            scratch_shapes=[pltpu.VMEM((2, tm * nc, LANES), F32), pltpu.VMEM((tm, d), BF16),
                            pltpu.VMEM((tm, d), F32), pltpu.SemaphoreType.DMA((2,))],
        ),
        out_shape=jax.ShapeDtypeStruct((n_rows * nc, LANES), F32),
        compiler_params=pltpu.CompilerParams(
            dimension_semantics=("arbitrary", "arbitrary"), vmem_limit_bytes=VMEM_LIMIT),
        name="moe_experts",
    )(src_rows, tile_expert, tile_valid, x, w_gate, w_up, w_down)


def _combine_kernel(d1_ref, d2_ref, y_hbm, h_ref, meta_ref, fg_ref, out_ref, a_ref, b_ref, sem):
    tc = h_ref.shape[0]
    base = pl.program_id(0) * tc

    nc = h_ref.shape[1] // LANES

    def issue(r, _):
        for thread, (d_ref, buf) in enumerate(((d1_ref, a_ref), (d2_ref, b_ref))):
            src = pl.multiple_of(d_ref[base + r] * nc, nc)
            pltpu.make_async_copy(y_hbm.at[pl.ds(src, nc)], buf.at[pl.ds(r * nc, nc)],
                                  sem).start(priority=thread % DMA_THREADS)
        return 0

    lax.fori_loop(0, tc, issue, 0, unroll=16)
    pltpu.make_async_copy(y_hbm.at[pl.ds(0, tc * nc)], a_ref, sem).wait()
    pltpu.make_async_copy(y_hbm.at[pl.ds(0, tc * nc)], b_ref, sem).wait()
    meta = meta_ref[...]
    w1 = meta[:, 2:3]
    w2 = meta[:, 3:4]
    moe = w1 * _load_rows_from_tiles(a_ref, nc) + w2 * _load_rows_from_tiles(b_ref, nc)
    out_ref[...] = _rms(h_ref[...] + moe, fg_ref[...])


def _combine(dest1, dest2, y, h2d, meta, fg, *, tc):
    t, d = h2d.shape
    return pl.pallas_call(
        _combine_kernel,
        grid_spec=pltpu.PrefetchScalarGridSpec(
            num_scalar_prefetch=2,
            grid=(t // tc,),
            in_specs=[
                pl.BlockSpec(memory_space=pl.ANY),
                pl.BlockSpec((tc, d), lambda i, d1, d2: (i, 0)),
                pl.BlockSpec((tc, LANES), lambda i, d1, d2: (i, 0)),
                pl.BlockSpec((1, d), lambda i, d1, d2: (0, 0)),
            ],
            out_specs=pl.BlockSpec((tc, d), lambda i, d1, d2: (i, 0)),
            scratch_shapes=[pltpu.VMEM((tc * d // LANES, LANES), F32),
                            pltpu.VMEM((tc * d // LANES, LANES), F32),
                            pltpu.SemaphoreType.DMA(())],
        ),
        out_shape=jax.ShapeDtypeStruct((t, d), F32),
        compiler_params=pltpu.CompilerParams(
            dimension_semantics=("arbitrary",), vmem_limit_bytes=VMEM_LIMIT),
        name="moe_combine",
    )(dest1, dest2, y, h2d, meta, fg)


def _routing_plan(meta, counts, *, tm, n_tiles):
    e1 = meta[:, 0].astype(jnp.int32)
    e2 = meta[:, 1].astype(jnp.int32)
    cnt = counts[0, :N_EXPERTS].astype(jnp.int32)
    padded = (cnt + tm - 1) // tm * tm
    ends = jnp.cumsum(padded)
    offs = ends - padded
    dest1 = offs[e1] + meta[:, 4].astype(jnp.int32)
    dest2 = offs[e2] + meta[:, 5].astype(jnp.int32)
    start = jnp.arange(n_tiles, dtype=jnp.int32) * tm
    tile_valid = (start < ends[-1]).astype(jnp.int32)
    last = jnp.sum((ends < ends[-1]).astype(jnp.int32))
    passed = jnp.sum((ends[None, :] <= start[:, None]).astype(jnp.int32), axis=1)
    src_rows = _source_rows(dest1, dest2, n_tiles * tm)
    return dest1, dest2, src_rows, jnp.minimum(passed, last), tile_valid


def _source_rows_kernel(d1_ref, d2_ref, src_ref):
    def clear(i, _):
        src_ref[i] = 0
        return 0

    def place(t, _):
        src_ref[d1_ref[t]] = t
        src_ref[d2_ref[t]] = t
        return 0

    lax.fori_loop(0, src_ref.shape[0], clear, 0, unroll=16)
    lax.fori_loop(0, d1_ref.shape[0], place, 0, unroll=8)


def _source_rows(dest1, dest2, n_rows):
    return pl.pallas_call(
        _source_rows_kernel,
        in_specs=[pl.BlockSpec(memory_space=pltpu.SMEM), pl.BlockSpec(memory_space=pltpu.SMEM)],
        out_specs=pl.BlockSpec(memory_space=pltpu.SMEM),
        out_shape=jax.ShapeDtypeStruct((n_rows,), jnp.int32),
        name="moe_source_rows",
    )(dest1, dest2)


def _pad_heads(w, head_dim):
    k = w.shape[0]
    w = w.reshape(k, MLA_HEADS, head_dim)
    w = jnp.pad(w, ((0, 0), (0, 0), (0, LANES - head_dim)))
    return w.reshape(k, MLA_HEADS * LANES)


def _pick(n, cap):
    t = min(n, cap)
    while n % t:
        t //= 2
    return t


def kernel(x, positions, mix_norm, ffn_norm, final_norm, mla_w_in, mla_q_norm, mla_w_q_up,
           mla_kv_norm, mla_w_kv_up, mla_w_o, hgrn_w_in, hgrn_lower_bounds, hgrn_out_norm,
           hgrn_w_o, ffn_w_gate, ffn_w_up, ffn_w_down, moe_router, moe_w_gate, moe_w_up,
           moe_w_down):
    batch, seq, d = x.shape
    t = batch * seq
    x2d = x.reshape(t, d)
    row = lambda v: v.reshape(1, -1)

    w_in = mla_w_in[0]
    n_lat = MLA_Q_LORA + MLA_KV_LORA
    w_in_p = w_in[:, :n_lat].astype(BF16)
    wkr_t = jnp.pad(w_in[:, n_lat:].T,
                    ((MLA_NOPE, LANES - MLA_NOPE - MLA_ROPE), (0, 0))).astype(BF16)
    wq_t = _pad_heads(mla_w_q_up[0], MLA_NOPE + MLA_ROPE).T.astype(BF16)
    w_kv = mla_w_kv_up[0].reshape(MLA_KV_LORA, MLA_HEADS, MLA_NOPE + MLA_V)
    wk_p = _pad_heads(w_kv[:, :, :MLA_NOPE].reshape(MLA_KV_LORA, -1), MLA_NOPE).astype(BF16)
    wv_t = w_kv[:, :, MLA_NOPE:].reshape(MLA_KV_LORA, -1).T.astype(BF16)
    inv_freq = ROPE_THETA ** (-jnp.arange(0, MLA_ROPE, 2, dtype=F32) / MLA_ROPE)
    freq = inv_freq.reshape(MLA_ROPE // 2, 1)
    pos = positions.astype(F32).reshape(1, t)

    q_t, k, v_t = _mla_proj(x2d, pos, row(mix_norm[0]), w_in_p, wkr_t, row(mla_q_norm[0]), wq_t,
                            row(mla_kv_norm[0]), wk_p, wv_t, freq,
                            batch=batch, seq=seq, tm=_pick(seq, 512))
    o = _mla_attn(q_t, k, v_t, tq=_pick(seq, 512)).reshape(t, MLA_HEADS * MLA_V)
    h = _attn_ffn(x2d, o, mla_w_o[0].astype(BF16), row(ffn_norm[0]),
                  ffn_w_gate[0].astype(BF16), ffn_w_up[0].astype(BF16),
                  ffn_w_down[0].astype(BF16), tm=_pick(t, 1024), tf=896)

    router_p = jnp.pad(moe_router[0], ((0, 0), (0, LANES - N_EXPERTS)))
    h, hn, meta, counts = _hgrn(h, row(mix_norm[1]), hgrn_w_in[0].astype(BF16),
                                hgrn_lower_bounds, row(hgrn_out_norm[0]),
                                hgrn_w_o[0].astype(BF16), row(ffn_norm[1]), router_p,
                                batch=batch, seq=seq, tt=_pick(seq, 256), layer=1)
    tm = _pick(t, 512)
    n_tiles = 2 * t // tm + N_EXPERTS
    dest1, dest2, src_rows, tile_expert, tile_valid = _routing_plan(
        meta, counts, tm=tm, n_tiles=n_tiles)
    y = _experts(src_rows, tile_expert, tile_valid, hn, moe_w_gate[0].astype(BF16),
                 moe_w_up[0].astype(BF16), moe_w_down[0].astype(BF16), tm=tm, tf=1792)
    out = _combine(dest1, dest2, y, h, meta, row(final_norm), tc=_pick(t, 512))
    return out.reshape(batch, seq, d)
```

```python
import functools

import jax
import jax.numpy as jnp
from jax import lax
from jax.experimental import pallas as pl
from jax.experimental.pallas import tpu as pltpu

EPS = 1e-6
LANES = 128
SUBLANES = 8

MLA_HEADS = 16
MLA_Q_LORA = 384
MLA_KV_LORA = 256
MLA_NOPE = 64
MLA_ROPE = 32
MLA_V = 64
ROPE_THETA = 10000.0

HG_HEADS = 8
HG_DK = 128
HG_CHUNK = 64

N_EXPERTS = 8

VMEM_LIMIT = 56 * 1024 * 1024
DMA_THREADS = 2

BF16 = jnp.bfloat16
F32 = jnp.float32


def _dot(a, b):
    return jnp.dot(a, b, preferred_element_type=F32)


def _dot_nt(a, b):
    return lax.dot_general(a, b, (((1,), (1,)), ((), ())), preferred_element_type=F32)


def _dot_tn(a, b):
    return lax.dot_general(a, b, (((0,), (0,)), ((), ())), preferred_element_type=F32)


def _rms(x, g):
    return x * lax.rsqrt(jnp.mean(x * x, axis=-1, keepdims=True) + EPS) * g


def _silu(x):
    return x * (1.0 / (1.0 + jnp.exp(-x)))


def _store_rows_as_tiles(ref, x):
    n, d = x.shape
    nc = d // LANES
    for c in range(nc):
        ref[pl.ds(c, n, stride=nc), :] = x[:, c * LANES:(c + 1) * LANES]


def _load_rows_from_tiles(ref, nc):
    n = ref.shape[0] // nc
    return jnp.concatenate([ref[pl.ds(c, n, stride=nc), :] for c in range(nc)], axis=1)


def _split3(x):
    hi = x.astype(BF16)
    r = x - hi.astype(F32)
    mid = r.astype(BF16)
    lo = (r - mid.astype(F32)).astype(BF16)
    return hi, mid, lo


def _mla_proj_kernel(x_ref, pos_ref, g_ref, win_ref, wkr_ref, qg_ref, wq_ref, kvg_ref, wk_ref,
                     wv_ref, freq_ref, q_out, k_out, v_out, *, scale):
    half = MLA_ROPE // 2
    x1_rows = slice(MLA_NOPE, MLA_NOPE + half)
    x2_rows = slice(MLA_NOPE + half, MLA_NOPE + MLA_ROPE)
    hn = _rms(x_ref[...], g_ref[...]).astype(BF16)
    proj = _dot(hn, win_ref[...])
    cqn = _rms(proj[:, :MLA_Q_LORA], qg_ref[...]).astype(BF16)
    ckvn = _rms(proj[:, MLA_Q_LORA:], kvg_ref[...]).astype(BF16)

    ang = freq_ref[...] * pos_ref[...]
    cos = jnp.cos(ang)
    sin = jnp.sin(ang)

    def rope_t(slot, mul):
        x1 = slot[x1_rows]
        x2 = slot[x2_rows]
        return jnp.concatenate([slot[:MLA_NOPE] * mul, (x1 * cos - x2 * sin) * mul,
                                (x2 * cos + x1 * sin) * mul, slot[MLA_NOPE + MLA_ROPE:]], axis=0)

    kr_t = _dot_nt(wkr_ref[...], hn)
    kr_roped = rope_t(kr_t, 1.0).T
    q_t = _dot_nt(wq_ref[...], cqn)
    k_pad = _dot(ckvn, wk_ref[...])
    for h in range(MLA_HEADS):
        sl = slice(h * LANES, (h + 1) * LANES)
        q_out[h] = rope_t(q_t[sl], scale).astype(BF16)
        k_out[h] = (k_pad[:, sl] + kr_roped).astype(BF16)
    v_out[...] = _dot_nt(wv_ref[...], ckvn).astype(BF16)


def _mla_proj(x2d, pos, g, w_in_p, wkr_t, qg, wq_t, kvg, wk_p, wv_t, freq, *, batch, seq, tm):
    t = x2d.shape[0]
    d = x2d.shape[1]
    nt = seq // tm
    const = lambda shape: pl.BlockSpec(shape, lambda i: (0,) * len(shape))
    scale = (MLA_NOPE + MLA_ROPE) ** -0.5 * 1.4426950408889634
    return pl.pallas_call(
        functools.partial(_mla_proj_kernel, scale=scale),
        grid=(t // tm,),
        in_specs=[
            pl.BlockSpec((tm, d), lambda i: (i, 0)),
            pl.BlockSpec((1, tm), lambda i: (0, i)),
            const((1, d)),
            const(w_in_p.shape),
            const(wkr_t.shape),
            const((1, MLA_Q_LORA)),
            const(wq_t.shape),
            const((1, MLA_KV_LORA)),
            const(wk_p.shape),
            const(wv_t.shape),
            const(freq.shape),
        ],
        out_specs=[
            pl.BlockSpec((None, MLA_HEADS, LANES, tm), lambda i: (i // nt, 0, 0, i % nt)),
            pl.BlockSpec((None, MLA_HEADS, tm, LANES), lambda i: (i // nt, 0, i % nt, 0)),
            pl.BlockSpec((None, MLA_HEADS * MLA_V, tm), lambda i: (i // nt, 0, i % nt)),
        ],
        out_shape=[
            jax.ShapeDtypeStruct((batch, MLA_HEADS, LANES, seq), BF16),
            jax.ShapeDtypeStruct((batch, MLA_HEADS, seq, LANES), BF16),
            jax.ShapeDtypeStruct((batch, MLA_HEADS * MLA_V, seq), BF16),
        ],
        compiler_params=pltpu.CompilerParams(
            dimension_semantics=("parallel",), vmem_limit_bytes=VMEM_LIMIT),
        name="mla_proj",
    )(x2d, pos, g, w_in_p, wkr_t, qg, wq_t, kvg, wk_p, wv_t, freq)


def _attn_kernel(q_ref, k_ref, v_ref, o_ref, acc_ref, m_ref, vx_ref, sa_ref, sb_ref, *, tq):
    seq = k_ref.shape[1]
    nq = seq // tq
    hp = k_ref.shape[0]
    for h in range(hp):
        vx_ref[h, :MLA_V] = v_ref[h * MLA_V:(h + 1) * MLA_V]
        vx_ref[h, MLA_V:] = jnp.ones((vx_ref.shape[1] - MLA_V, seq), BF16)
    causal = (lax.broadcasted_iota(jnp.int32, (tq, tq), 0)
              <= lax.broadcasted_iota(jnp.int32, (tq, tq), 1))

    def scores(dst, blk, q0):
        k0 = pl.multiple_of(blk * tq, tq)
        for h in range(hp):
            dst[h] = _dot(k_ref[h, pl.ds(k0, tq), :], q_ref[h, :, pl.ds(q0, tq)])

    def softmax_pv(src, blk, masked):
        k0 = pl.multiple_of(blk * tq, tq)
        for h in range(hp):
            s = src[h]
            if masked:
                s = jnp.where(causal, s, -jnp.inf)
            m = m_ref[h]
            m_new = jnp.maximum(m, jnp.max(s, axis=0, keepdims=True))
            p = jnp.exp2(s - m_new)
            alpha = jnp.exp2(m - m_new)
            m_ref[h] = m_new
            acc_ref[h] = alpha * acc_ref[h] + _dot(vx_ref[h, :, pl.ds(k0, tq)], p.astype(BF16))

    def q_block(qi, _):
        q0 = pl.multiple_of(qi * tq, tq)
        acc_ref[...] = jnp.zeros_like(acc_ref)
        m_ref[...] = jnp.full_like(m_ref, -jnp.inf)

        scores(sa_ref, 0, q0)

        def pair(jp, _):
            scores(sb_ref, 2 * jp + 1, q0)
            softmax_pv(sa_ref, 2 * jp, False)
            scores(sa_ref, 2 * jp + 2, q0)
            softmax_pv(sb_ref, 2 * jp + 1, False)
            return 0

        lax.fori_loop(0, qi // 2, pair, 0)

        @pl.when(qi % 2 == 1)
        def _():
            scores(sb_ref, qi, q0)
            softmax_pv(sa_ref, qi - 1, False)
            softmax_pv(sb_ref, qi, True)

        @pl.when(qi % 2 == 0)
        def _():
            softmax_pv(sa_ref, qi, True)

        out = jnp.concatenate([acc_ref[h, :MLA_V] / acc_ref[h, MLA_V:MLA_V + 1]
                               for h in range(hp)], axis=0)
        o_ref[pl.ds(q0, tq), :] = out.T.astype(BF16)
        return 0

    lax.fori_loop(0, nq, q_block, 0)


def _mla_attn(q_t, k, v_t, *, tq, hp):
    batch, heads, seq, _ = k.shape
    return pl.pallas_call(
        functools.partial(_attn_kernel, tq=tq),
        grid=(batch, heads // hp),
        in_specs=[
            pl.BlockSpec((None, hp, LANES, seq), lambda b, g: (b, g, 0, 0)),
            pl.BlockSpec((None, hp, seq, LANES), lambda b, g: (b, g, 0, 0)),
            pl.BlockSpec((None, hp * MLA_V, seq), lambda b, g: (b, g, 0)),
        ],
        out_specs=pl.BlockSpec((None, seq, hp * MLA_V), lambda b, g: (b, 0, g)),
        out_shape=jax.ShapeDtypeStruct((batch, seq, heads * MLA_V), BF16),
        scratch_shapes=[pltpu.VMEM((hp, MLA_V + 16, tq), F32),
                        pltpu.VMEM((hp, 1, tq), F32), pltpu.VMEM((hp, MLA_V + 16, seq), BF16),
                        pltpu.VMEM((hp, tq, tq), F32), pltpu.VMEM((hp, tq, tq), F32)],
        compiler_params=pltpu.CompilerParams(
            dimension_semantics=("parallel", "parallel"), vmem_limit_bytes=VMEM_LIMIT),
        name="mla_attn",
    )(q_t, k, v_t)


def _attn_ffn_kernel(x_ref, o_ref, wo_ref, g_ref, wg_ref, wu_ref, wd_ref, out_ref, hn_ref):
    j = pl.program_id(1)

    @pl.when(j == 0)
    def _():
        h1 = x_ref[...] + _dot(o_ref[...], wo_ref[...])
        out_ref[...] = h1
        hn_ref[...] = _rms(h1, g_ref[...]).astype(BF16)

    hn = hn_ref[...]
    a = _silu(_dot(hn, wg_ref[...])) * _dot(hn, wu_ref[...])
    out_ref[...] += _dot(a.astype(BF16), wd_ref[...])


def _attn_ffn(x2d, o, w_o, g, w_gate, w_up, w_down, *, tm, tf):
    t, d = x2d.shape
    f = w_gate.shape[1]
    return pl.pallas_call(
        _attn_ffn_kernel,
        grid=(t // tm, f // tf),
        in_specs=[
            pl.BlockSpec((tm, d), lambda i, j: (i, 0)),
            pl.BlockSpec((tm, o.shape[1]), lambda i, j: (i, 0)),
            pl.BlockSpec(w_o.shape, lambda i, j: (0, 0)),
            pl.BlockSpec((1, d), lambda i, j: (0, 0)),
            pl.BlockSpec((d, tf), lambda i, j: (0, j)),
            pl.BlockSpec((d, tf), lambda i, j: (0, j)),
            pl.BlockSpec((tf, d), lambda i, j: (j, 0)),
        ],
        out_specs=pl.BlockSpec((tm, d), lambda i, j: (i, 0)),
        out_shape=jax.ShapeDtypeStruct((t, d), F32),
        scratch_shapes=[pltpu.VMEM((tm, d), BF16)],
        compiler_params=pltpu.CompilerParams(
            dimension_semantics=("parallel", "arbitrary"), vmem_limit_bytes=VMEM_LIMIT),
        name="attn_ffn",
    )(x2d, o, w_o, g, w_gate, w_up, w_down)


def _hgrn_kernel(h_ref, g_ref, win_ref, lbraw_ref, og_ref, wo_ref, fg_ref, router_ref,
                 h_out, hn_out, meta_out, cnt_out, st_ref, o_scr, cnt_ref, *, layer):
    tt = h_ref.shape[0]
    width = HG_HEADS * HG_DK
    nchunk = tt // HG_CHUNK

    @pl.when(pl.program_id(1) == 0)
    def _():
        st_ref[...] = jnp.zeros_like(st_ref)

    lbraw = lbraw_ref[...]
    e = jnp.exp(lbraw - jnp.max(lbraw, axis=0, keepdims=True))
    sm = e / jnp.sum(e, axis=0, keepdims=True)
    lb = jnp.sum(sm[1:layer + 1], axis=0, keepdims=True)

    h_in = h_ref[...]
    hn = _rms(h_in, g_ref[...]).astype(BF16)
    proj = _dot(hn, win_ref[...])
    q_all = proj[:, :width] * (HG_DK ** -0.5)
    f_all = lb + (1.0 - lb) * (1.0 / (1.0 + jnp.exp(-proj[:, width:2 * width])))
    logf = jnp.log(f_all)
    k_all = 1.0 - f_all

    r = lax.broadcasted_iota(jnp.int32, (tt, tt), 0)
    c = lax.broadcasted_iota(jnp.int32, (tt, tt), 1)
    shift = HG_CHUNK.bit_length() - 1
    tril = (c <= r) & ((r >> shift) == (c >> shift))
    tril_b = jnp.where(tril, 1.0, 0.0).astype(BF16)
    hi, mid, _ = _split3(logf)
    b_all = _dot(tril_b, hi) + _dot(tril_b, mid)

    for h in range(HG_HEADS):
        sl = slice(h * HG_DK, (h + 1) * HG_DK)
        q = q_all[:, sl]
        k = k_all[:, sl]
        b = b_all[:, sl]
        v = proj[:, 2 * width + h * HG_DK:2 * width + (h + 1) * HG_DK]
        gt = proj[:, 3 * width + h * HG_DK:3 * width + (h + 1) * HG_DK]
        v_b = v.astype(BF16)
        qd = (q * jnp.exp(b)).astype(BF16)
        kd = (k * jnp.exp(-b)).astype(BF16)
        a = jnp.where(tril, _dot_nt(qd, kd), 0.0)
        o = _dot(a.astype(BF16), v_b)

        st = st_ref[h]
        inter = []
        for n in range(nchunk):
            rows = slice(n * HG_CHUNK, (n + 1) * HG_CHUNK)
            b_n = b[rows]
            b_last = b_n[HG_CHUNK - 1:HG_CHUNK]
            inter.append(_dot_nt(qd[rows], st.astype(BF16)))
            kdl = (k[rows] * jnp.exp(b_last - b_n)).astype(BF16)
            st = st * jnp.exp(b_last) + _dot_tn(v_b[rows], kdl)
        st_ref[h] = st
        o = o + jnp.concatenate(inter, axis=0)
        o = _rms(o, og_ref[...]) * _silu(gt)
        o_scr[:, sl] = o.astype(BF16)

    h_new = h_in + _dot(o_scr[...], wo_ref[...])
    h_out[...] = h_new
    hn2 = _rms(h_new, fg_ref[...])
    _store_rows_as_tiles(hn_out, hn2)

    x_hi, x_mid, _ = _split3(hn2)
    router = router_ref[...]
    r_hi = router.astype(BF16)
    r_lo = (router - r_hi.astype(F32)).astype(BF16)
    hi_terms = _dot(x_hi, jnp.concatenate([r_hi, r_lo], axis=1))
    logits = hi_terms[:, :LANES] + _dot(x_mid, r_hi) + hi_terms[:, LANES:]
    lane = lax.broadcasted_iota(jnp.int32, logits.shape, 1)
    neg = -jnp.inf
    logits = jnp.where(lane < N_EXPERTS, logits, neg)
    m1 = jnp.max(logits, axis=-1, keepdims=True)
    i1 = jnp.min(jnp.where(logits == m1, lane, LANES), axis=-1, keepdims=True)
    rest = jnp.where(lane == i1, neg, logits)
    m2 = jnp.max(rest, axis=-1, keepdims=True)
    i2 = jnp.min(jnp.where(rest == m2, lane, LANES), axis=-1, keepdims=True)
    e2 = jnp.exp(m2 - m1)
    w1 = 1.0 / (1.0 + e2)
    w2 = e2 / (1.0 + e2)

    @pl.when((pl.program_id(0) == 0) & (pl.program_id(1) == 0))
    def _():
        cnt_ref[...] = jnp.zeros_like(cnt_ref)

    onehot = jnp.where((lane == i1) | (lane == i2), 1.0, 0.0)
    earlier = jnp.where(c < r, 1.0, 0.0).astype(BF16)
    rank = _dot(earlier, onehot.astype(BF16)) + cnt_ref[...]
    cnt = cnt_ref[...] + jnp.sum(onehot, axis=0, keepdims=True)
    cnt_ref[...] = cnt
    cnt_out[...] = jnp.broadcast_to(cnt, cnt_out.shape)
    rank1 = jnp.sum(jnp.where(lane == i1, rank, 0.0), axis=-1, keepdims=True)
    rank2 = jnp.sum(jnp.where(lane == i2, rank, 0.0), axis=-1, keepdims=True)
    fields = (i1.astype(F32), i2.astype(F32), w1, w2, rank1, rank2)
    meta = jnp.zeros(logits.shape, F32)
    for idx, val in enumerate(fields):
        meta = jnp.where(lane == idx, val, meta)
    meta_out[...] = meta


def _hgrn(h2d, g, w_in, lb_raw, og, w_o, fg, router_p, *, batch, seq, tt, layer):
    t, d = h2d.shape
    nt = seq // tt
    width = HG_HEADS * HG_DK
    tok = lambda cols: pl.BlockSpec((tt, cols), lambda b, i: (b * nt + i, 0))
    const = lambda shape: pl.BlockSpec(shape, lambda b, i: (0,) * len(shape))
    return pl.pallas_call(
        functools.partial(_hgrn_kernel, layer=layer),
        grid=(batch, nt),
        in_specs=[tok(d), const((1, d)), const(w_in.shape), const(lb_raw.shape),
                  const((1, HG_DK)), const(w_o.shape), const((1, d)), const(router_p.shape)],
        out_specs=[tok(d),
                   pl.BlockSpec((tt * (d // LANES), LANES), lambda b, i: (b * nt + i, 0)),
                   tok(LANES), const((SUBLANES, LANES))],
        out_shape=[jax.ShapeDtypeStruct((t, d), F32),
                   jax.ShapeDtypeStruct((t * (d // LANES), LANES), F32),
                   jax.ShapeDtypeStruct((t, LANES), F32),
                   jax.ShapeDtypeStruct((SUBLANES, LANES), F32)],
        scratch_shapes=[pltpu.VMEM((HG_HEADS, HG_DK, HG_DK), F32),
                        pltpu.VMEM((tt, width), BF16),
                        pltpu.VMEM((1, LANES), F32)],
        compiler_params=pltpu.CompilerParams(
            dimension_semantics=("arbitrary", "arbitrary"), vmem_limit_bytes=VMEM_LIMIT),
        name="hgrn",
    )(h2d, g, w_in, lb_raw, og, w_o, fg, router_p)


def _expert_kernel(src_ref, te_ref, tv_ref, x_hbm, wg_ref, wu_ref, wd_ref, y_ref,
                   xbuf, xb_ref, acc_ref, sems):
    del te_ref
    r = pl.program_id(0)
    j = pl.program_id(1)
    n_tiles = pl.num_programs(0)
    nj = pl.num_programs(1)
    tm = xb_ref.shape[0]
    nc = xb_ref.shape[1] // LANES
    part = tm // nj
    valid = tv_ref[r] != 0
    has_next = r + 1 < n_tiles

    def row_copy(tile, row):
        slot = tile % 2
        src = pl.multiple_of(src_ref[tile * tm + row] * nc, nc)
        return pltpu.make_async_copy(x_hbm.at[pl.ds(src, nc)],
                                     xbuf.at[slot, pl.ds(row * nc, nc)], sems.at[slot])

    def swiglu_step():
        x = xb_ref[...]
        a = _silu(_dot(x, wg_ref[...])) * _dot(x, wu_ref[...])
        acc_ref[...] += _dot(a.astype(BF16), wd_ref[...])

    @pl.when((r == 0) & (j == 0))
    def _():
        lax.fori_loop(0, tm, lambda i, c: (row_copy(r, i).start(), c)[1], 0)

    @pl.when((j == 0) & ((r == 0) | (tv_ref[jnp.maximum(r - 1, 0)] != 0)))
    def _():
        pltpu.make_async_copy(x_hbm.at[pl.ds(0, tm * nc)], xbuf.at[r % 2],
                              sems.at[r % 2]).wait()
        xb_ref[...] = _load_rows_from_tiles(xbuf.at[r % 2], nc).astype(BF16)

    @pl.when(j == 0)
    def _():
        acc_ref[...] = jnp.zeros_like(acc_ref)

    @pl.when(valid & has_next)
    def _():
        for i in range(part):
            row_copy(r + 1, j * part + i).start(priority=i % DMA_THREADS)
        swiglu_step()

    @pl.when(valid & jnp.logical_not(has_next))
    def _():
        swiglu_step()

    @pl.when(j == nj - 1)
    def _():
        _store_rows_as_tiles(y_ref, acc_ref[...])


def _experts(src_rows, tile_expert, tile_valid, x, w_gate, w_up, w_down, *, tm, tf):
    n_rows = src_rows.shape[0]
    d = w_gate.shape[1]
    nc = d // LANES
    f = w_gate.shape[2]
    nj = f // tf
    col = lambda r, j, tv: jnp.where(tv[r] != 0, j, nj - 1)
    return pl.pallas_call(
        _expert_kernel,
        grid_spec=pltpu.PrefetchScalarGridSpec(
            num_scalar_prefetch=3,
            grid=(n_rows // tm, nj),
            in_specs=[
                pl.BlockSpec(memory_space=pl.ANY),
                pl.BlockSpec((None, d, tf), lambda r, j, s, te, tv: (te[r], 0, col(r, j, tv))),
                pl.BlockSpec((None, d, tf), lambda r, j, s, te, tv: (te[r], 0, col(r, j, tv))),
                pl.BlockSpec((None, tf, d), lambda r, j, s, te, tv: (te[r], col(r, j, tv), 0)),
            ],
            out_specs=pl.BlockSpec((tm * nc, LANES), lambda r, j, s, te, tv: (r, 0)),
            scratch_shapes=[pltpu.VMEM((2, tm * nc, LANES), F32), pltpu.VMEM((tm, d), BF16),
                            pltpu.VMEM((tm, d), F32), pltpu.SemaphoreType.DMA((2,))],
        ),
        out_shape=jax.ShapeDtypeStruct((n_rows * nc, LANES), F32),
        compiler_params=pltpu.CompilerParams(
            dimension_semantics=("arbitrary", "arbitrary"), vmem_limit_bytes=VMEM_LIMIT),
        name="moe_experts",
    )(src_rows, tile_expert, tile_valid, x, w_gate, w_up, w_down)


def _combine_kernel(d1_ref, d2_ref, y_hbm, h_ref, meta_ref, fg_ref, out_ref, a_ref, b_ref, sem):
    tc = h_ref.shape[0]
    base = pl.program_id(0) * tc

    nc = h_ref.shape[1] // LANES

    def issue(r, _):
        for thread, (d_ref, buf) in enumerate(((d1_ref, a_ref), (d2_ref, b_ref))):
            src = pl.multiple_of(d_ref[base + r] * nc, nc)
            pltpu.make_async_copy(y_hbm.at[pl.ds(src, nc)], buf.at[pl.ds(r * nc, nc)],
                                  sem).start(priority=thread % DMA_THREADS)
        return 0

    lax.fori_loop(0, tc, issue, 0, unroll=16)
    pltpu.make_async_copy(y_hbm.at[pl.ds(0, tc * nc)], a_ref, sem).wait()
    pltpu.make_async_copy(y_hbm.at[pl.ds(0, tc * nc)], b_ref, sem).wait()
    meta = meta_ref[...]
    w1 = meta[:, 2:3]
    w2 = meta[:, 3:4]
    moe = w1 * _load_rows_from_tiles(a_ref, nc) + w2 * _load_rows_from_tiles(b_ref, nc)
    out_ref[...] = _rms(h_ref[...] + moe, fg_ref[...])


def _combine(dest1, dest2, y, h2d, meta, fg, *, tc):
    t, d = h2d.shape
    return pl.pallas_call(
        _combine_kernel,
        grid_spec=pltpu.PrefetchScalarGridSpec(
            num_scalar_prefetch=2,
            grid=(t // tc,),
            in_specs=[
                pl.BlockSpec(memory_space=pl.ANY),
                pl.BlockSpec((tc, d), lambda i, d1, d2: (i, 0)),
                pl.BlockSpec((tc, LANES), lambda i, d1, d2: (i, 0)),
                pl.BlockSpec((1, d), lambda i, d1, d2: (0, 0)),
            ],
            out_specs=pl.BlockSpec((tc, d), lambda i, d1, d2: (i, 0)),
            scratch_shapes=[pltpu.VMEM((tc * d // LANES, LANES), F32),
                            pltpu.VMEM((tc * d // LANES, LANES), F32),
                            pltpu.SemaphoreType.DMA(())],
        ),
        out_shape=jax.ShapeDtypeStruct((t, d), F32),
        compiler_params=pltpu.CompilerParams(
            dimension_semantics=("arbitrary",), vmem_limit_bytes=VMEM_LIMIT),
        name="moe_combine",
    )(dest1, dest2, y, h2d, meta, fg)


def _routing_plan(meta, counts, *, tm, n_tiles):
    e1 = meta[:, 0].astype(jnp.int32)
    e2 = meta[:, 1].astype(jnp.int32)
    cnt = counts[0, :N_EXPERTS].astype(jnp.int32)
    padded = (cnt + tm - 1) // tm * tm
    ends = jnp.cumsum(padded)
    offs = ends - padded
    dest1 = offs[e1] + meta[:, 4].astype(jnp.int32)
    dest2 = offs[e2] + meta[:, 5].astype(jnp.int32)
    start = jnp.arange(n_tiles, dtype=jnp.int32) * tm
    tile_valid = (start < ends[-1]).astype(jnp.int32)
    last = jnp.sum((ends < ends[-1]).astype(jnp.int32))
    passed = jnp.sum((ends[None, :] <= start[:, None]).astype(jnp.int32), axis=1)
    src_rows = _source_rows(dest1, dest2, n_tiles * tm)
    return dest1, dest2, src_rows, jnp.minimum(passed, last), tile_valid


def _source_rows_kernel(d1_ref, d2_ref, src_ref):
    def clear(i, _):
        src_ref[i] = 0
        return 0

    def place(t, _):
        src_ref[d1_ref[t]] = t
        src_ref[d2_ref[t]] = t
        return 0

    lax.fori_loop(0, src_ref.shape[0], clear, 0, unroll=16)
    lax.fori_loop(0, d1_ref.shape[0], place, 0, unroll=8)


def _source_rows(dest1, dest2, n_rows):
    return pl.pallas_call(
        _source_rows_kernel,
        in_specs=[pl.BlockSpec(memory_space=pltpu.SMEM), pl.BlockSpec(memory_space=pltpu.SMEM)],
        out_specs=pl.BlockSpec(memory_space=pltpu.SMEM),
        out_shape=jax.ShapeDtypeStruct((n_rows,), jnp.int32),
        name="moe_source_rows",
    )(dest1, dest2)


def _pad_heads(w, head_dim):
    k = w.shape[0]
    w = w.reshape(k, MLA_HEADS, head_dim)
    w = jnp.pad(w, ((0, 0), (0, 0), (0, LANES - head_dim)))
    return w.reshape(k, MLA_HEADS * LANES)


def _pick(n, cap):
    t = min(n, cap)
    while n % t:
        t //= 2
    return t


def kernel(x, positions, mix_norm, ffn_norm, final_norm, mla_w_in, mla_q_norm, mla_w_q_up,
           mla_kv_norm, mla_w_kv_up, mla_w_o, hgrn_w_in, hgrn_lower_bounds, hgrn_out_norm,
           hgrn_w_o, ffn_w_gate, ffn_w_up, ffn_w_down, moe_router, moe_w_gate, moe_w_up,
           moe_w_down):
    batch, seq, d = x.shape
    t = batch * seq
    x2d = x.reshape(t, d)
    row = lambda v: v.reshape(1, -1)

    w_in = mla_w_in[0]
    n_lat = MLA_Q_LORA + MLA_KV_LORA
    w_in_p = w_in[:, :n_lat].astype(BF16)
    wkr_t = jnp.pad(w_in[:, n_lat:].T,
                    ((MLA_NOPE, LANES - MLA_NOPE - MLA_ROPE), (0, 0))).astype(BF16)
    wq_t = _pad_heads(mla_w_q_up[0], MLA_NOPE + MLA_ROPE).T.astype(BF16)
    w_kv = mla_w_kv_up[0].reshape(MLA_KV_LORA, MLA_HEADS, MLA_NOPE + MLA_V)
    wk_p = _pad_heads(w_kv[:, :, :MLA_NOPE].reshape(MLA_KV_LORA, -1), MLA_NOPE).astype(BF16)
    wv_t = w_kv[:, :, MLA_NOPE:].reshape(MLA_KV_LORA, -1).T.astype(BF16)
    inv_freq = ROPE_THETA ** (-jnp.arange(0, MLA_ROPE, 2, dtype=F32) / MLA_ROPE)
    freq = inv_freq.reshape(MLA_ROPE // 2, 1)
    pos = positions.astype(F32).reshape(1, t)

    q_t, k, v_t = _mla_proj(x2d, pos, row(mix_norm[0]), w_in_p, wkr_t, row(mla_q_norm[0]), wq_t,
                            row(mla_kv_norm[0]), wk_p, wv_t, freq,
                            batch=batch, seq=seq, tm=_pick(seq, 512))
    o = _mla_attn(q_t, k, v_t, tq=_pick(seq, 512), hp=4).reshape(t, MLA_HEADS * MLA_V)
    h = _attn_ffn(x2d, o, mla_w_o[0].astype(BF16), row(ffn_norm[0]),
                  ffn_w_gate[0].astype(BF16), ffn_w_up[0].astype(BF16),
                  ffn_w_down[0].astype(BF16), tm=_pick(t, 1024), tf=896)

    router_p = jnp.pad(moe_router[0], ((0, 0), (0, LANES - N_EXPERTS)))
    h, hn, meta, counts = _hgrn(h, row(mix_norm[1]), hgrn_w_in[0].astype(BF16),
                                hgrn_lower_bounds, row(hgrn_out_norm[0]),
                                hgrn_w_o[0].astype(BF16), row(ffn_norm[1]), router_p,
                                batch=batch, seq=seq, tt=_pick(seq, 256), layer=1)
    tm = _pick(t, 512)
    n_tiles = 2 * t // tm + N_EXPERTS
    dest1, dest2, src_rows, tile_expert, tile_valid = _routing_plan(
        meta, counts, tm=tm, n_tiles=n_tiles)
    y = _experts(src_rows, tile_expert, tile_valid, hn, moe_w_gate[0].astype(BF16),
                 moe_w_up[0].astype(BF16), moe_w_down[0].astype(BF16), tm=tm, tf=1792)
    out = _combine(dest1, dest2, y, h, meta, row(final_norm), tc=_pick(t, 512))
    return out.reshape(batch, seq, d)
```

```python
import functools

import jax
import jax.numpy as jnp
from jax import lax
from jax.experimental import pallas as pl
from jax.experimental.pallas import tpu as pltpu

EPS = 1e-6
LANES = 128
SUBLANES = 8

MLA_HEADS = 16
MLA_Q_LORA = 384
MLA_KV_LORA = 256
MLA_NOPE = 64
MLA_ROPE = 32
MLA_V = 64
ROPE_THETA = 10000.0

HG_HEADS = 8
HG_DK = 128
HG_CHUNK = 64

N_EXPERTS = 8

VMEM_LIMIT = 56 * 1024 * 1024
DMA_THREADS = 2

BF16 = jnp.bfloat16
F32 = jnp.float32


def _dot(a, b):
    return jnp.dot(a, b, preferred_element_type=F32)


def _dot_nt(a, b):
    return lax.dot_general(a, b, (((1,), (1,)), ((), ())), preferred_element_type=F32)


def _dot_tn(a, b):
    return lax.dot_general(a, b, (((0,), (0,)), ((), ())), preferred_element_type=F32)


def _rms(x, g):
    return x * lax.rsqrt(jnp.mean(x * x, axis=-1, keepdims=True) + EPS) * g


def _silu(x):
    return x * (1.0 / (1.0 + jnp.exp(-x)))


def _store_rows_as_tiles(ref, x):
    n, d = x.shape
    nc = d // LANES
    for c in range(nc):
        ref[pl.ds(c, n, stride=nc), :] = x[:, c * LANES:(c + 1) * LANES]


def _load_rows_from_tiles(ref, nc):
    n = ref.shape[0] // nc
    return jnp.concatenate([ref[pl.ds(c, n, stride=nc), :] for c in range(nc)], axis=1)


def _split3(x):
    hi = x.astype(BF16)
    r = x - hi.astype(F32)
    mid = r.astype(BF16)
    lo = (r - mid.astype(F32)).astype(BF16)
    return hi, mid, lo


def _mla_proj_kernel(x_ref, pos_ref, g_ref, win_ref, wkr_ref, qg_ref, wq_ref, kvg_ref, wk_ref,
                     wv_ref, freq_ref, q_out, k_out, v_out, *, scale):
    half = MLA_ROPE // 2
    x1_rows = slice(MLA_NOPE, MLA_NOPE + half)
    x2_rows = slice(MLA_NOPE + half, MLA_NOPE + MLA_ROPE)
    hn = _rms(x_ref[...], g_ref[...]).astype(BF16)
    proj = _dot(hn, win_ref[...])
    cqn = _rms(proj[:, :MLA_Q_LORA], qg_ref[...]).astype(BF16)
    ckvn = _rms(proj[:, MLA_Q_LORA:], kvg_ref[...]).astype(BF16)

    ang = freq_ref[...] * pos_ref[...]
    cos = jnp.cos(ang)
    sin = jnp.sin(ang)

    def rope_t(slot, mul):
        x1 = slot[x1_rows]
        x2 = slot[x2_rows]
        return jnp.concatenate([slot[:MLA_NOPE] * mul, (x1 * cos - x2 * sin) * mul,
                                (x2 * cos + x1 * sin) * mul, slot[MLA_NOPE + MLA_ROPE:]], axis=0)

    kr_t = _dot_nt(wkr_ref[...], hn)
    kr_roped = rope_t(kr_t, 1.0).T
    q_t = _dot_nt(wq_ref[...], cqn)
    k_pad = _dot(ckvn, wk_ref[...])
    for h in range(MLA_HEADS):
        sl = slice(h * LANES, (h + 1) * LANES)
        q_out[h] = rope_t(q_t[sl], scale).astype(BF16)
        k_out[h] = (k_pad[:, sl] + kr_roped).astype(BF16)
    v_out[...] = _dot_nt(wv_ref[...], ckvn).astype(BF16)


def _mla_proj(x2d, pos, g, w_in_p, wkr_t, qg, wq_t, kvg, wk_p, wv_t, freq, *, batch, seq, tm):
    t = x2d.shape[0]
    d = x2d.shape[1]
    nt = seq // tm
    const = lambda shape: pl.BlockSpec(shape, lambda i: (0,) * len(shape))
    scale = (MLA_NOPE + MLA_ROPE) ** -0.5 * 1.4426950408889634
    return pl.pallas_call(
        functools.partial(_mla_proj_kernel, scale=scale),
        grid=(t // tm,),
        in_specs=[
            pl.BlockSpec((tm, d), lambda i: (i, 0)),
            pl.BlockSpec((1, tm), lambda i: (0, i)),
            const((1, d)),
            const(w_in_p.shape),
            const(wkr_t.shape),
            const((1, MLA_Q_LORA)),
            const(wq_t.shape),
            const((1, MLA_KV_LORA)),
            const(wk_p.shape),
            const(wv_t.shape),
            const(freq.shape),
        ],
        out_specs=[
            pl.BlockSpec((None, MLA_HEADS, LANES, tm), lambda i: (i // nt, 0, 0, i % nt)),
            pl.BlockSpec((None, MLA_HEADS, tm, LANES), lambda i: (i // nt, 0, i % nt, 0)),
            pl.BlockSpec((None, MLA_HEADS * MLA_V, tm), lambda i: (i // nt, 0, i % nt)),
        ],
        out_shape=[
            jax.ShapeDtypeStruct((batch, MLA_HEADS, LANES, seq), BF16),
            jax.ShapeDtypeStruct((batch, MLA_HEADS, seq, LANES), BF16),
            jax.ShapeDtypeStruct((batch, MLA_HEADS * MLA_V, seq), BF16),
        ],
        compiler_params=pltpu.CompilerParams(
            dimension_semantics=("parallel",), vmem_limit_bytes=VMEM_LIMIT),
        name="mla_proj",
    )(x2d, pos, g, w_in_p, wkr_t, qg, wq_t, kvg, wk_p, wv_t, freq)


def _attn_kernel(q_ref, k_ref, v_ref, o_ref, acc_ref, m_ref, vx_ref, sa_ref, sb_ref, *, tq):
    seq = k_ref.shape[1]
    nq = seq // tq
    hp = k_ref.shape[0]
    for h in range(hp):
        vx_ref[h, :MLA_V] = v_ref[h * MLA_V:(h + 1) * MLA_V]
        vx_ref[h, MLA_V:] = jnp.ones((vx_ref.shape[1] - MLA_V, seq), BF16)
    causal = (lax.broadcasted_iota(jnp.int32, (tq, tq), 0)
              <= lax.broadcasted_iota(jnp.int32, (tq, tq), 1))

    def scores(dst, blk, q0):
        k0 = pl.multiple_of(blk * tq, tq)
        for h in range(hp):
            dst[h] = _dot(k_ref[h, pl.ds(k0, tq), :], q_ref[h, :, pl.ds(q0, tq)])

    def softmax_pv(src, blk, masked):
        k0 = pl.multiple_of(blk * tq, tq)
        for h in range(hp):
            s = src[h]
            if masked:
                s = jnp.where(causal, s, -jnp.inf)
            m = m_ref[h]
            m_new = jnp.maximum(m, jnp.max(s, axis=0, keepdims=True))
            p = jnp.exp2(s - m_new)
            alpha = jnp.exp2(m - m_new)
            m_ref[h] = m_new
            acc_ref[h] = alpha * acc_ref[h] + _dot(vx_ref[h, :, pl.ds(k0, tq)], p.astype(BF16))

    def q_block(qi, _):
        q0 = pl.multiple_of(qi * tq, tq)
        acc_ref[...] = jnp.zeros_like(acc_ref)
        m_ref[...] = jnp.full_like(m_ref, -jnp.inf)

        scores(sa_ref, 0, q0)

        def pair(jp, _):
            scores(sb_ref, 2 * jp + 1, q0)
            softmax_pv(sa_ref, 2 * jp, False)
            scores(sa_ref, 2 * jp + 2, q0)
            softmax_pv(sb_ref, 2 * jp + 1, False)
            return 0

        lax.fori_loop(0, qi // 2, pair, 0)

        @pl.when(qi % 2 == 1)
        def _():
            scores(sb_ref, qi, q0)
            softmax_pv(sa_ref, qi - 1, False)
            softmax_pv(sb_ref, qi, True)

        @pl.when(qi % 2 == 0)
        def _():
            softmax_pv(sa_ref, qi, True)

        out = jnp.concatenate([acc_ref[h, :MLA_V] / acc_ref[h, MLA_V:MLA_V + 1]
                               for h in range(hp)], axis=0)
        o_ref[pl.ds(q0, tq), :] = out.T.astype(BF16)
        return 0

    lax.fori_loop(0, nq, q_block, 0)


def _mla_attn(q_t, k, v_t, *, tq, hp):
    batch, heads, seq, _ = k.shape
    return pl.pallas_call(
        functools.partial(_attn_kernel, tq=tq),
        grid=(batch, heads // hp),
        in_specs=[
            pl.BlockSpec((None, hp, LANES, seq), lambda b, g: (b, g, 0, 0)),
            pl.BlockSpec((None, hp, seq, LANES), lambda b, g: (b, g, 0, 0)),
            pl.BlockSpec((None, hp * MLA_V, seq), lambda b, g: (b, g, 0)),
        ],
        out_specs=pl.BlockSpec((None, seq, hp * MLA_V), lambda b, g: (b, 0, g)),
        out_shape=jax.ShapeDtypeStruct((batch, seq, heads * MLA_V), BF16),
        scratch_shapes=[pltpu.VMEM((hp, MLA_V + 16, tq), F32),
                        pltpu.VMEM((hp, 1, tq), F32), pltpu.VMEM((hp, MLA_V + 16, seq), BF16),
                        pltpu.VMEM((hp, tq, tq), F32), pltpu.VMEM((hp, tq, tq), F32)],
        compiler_params=pltpu.CompilerParams(
            dimension_semantics=("parallel", "parallel"), vmem_limit_bytes=VMEM_LIMIT),
        name="mla_attn",
    )(q_t, k, v_t)


def _attn_ffn_kernel(x_ref, o_ref, wo_ref, g_ref, wg_ref, wu_ref, wd_ref, out_ref, hn_ref):
    j = pl.program_id(1)

    @pl.when(j == 0)
    def _():
        h1 = x_ref[...] + _dot(o_ref[...], wo_ref[...])
        out_ref[...] = h1
        hn_ref[...] = _rms(h1, g_ref[...]).astype(BF16)

    hn = hn_ref[...]
    a = _silu(_dot(hn, wg_ref[...])) * _dot(hn, wu_ref[...])
    out_ref[...] += _dot(a.astype(BF16), wd_ref[...])


def _attn_ffn(x2d, o, w_o, g, w_gate, w_up, w_down, *, tm, tf):
    t, d = x2d.shape
    f = w_gate.shape[1]
    return pl.pallas_call(
        _attn_ffn_kernel,
        grid=(t // tm, f // tf),
        in_specs=[
            pl.BlockSpec((tm, d), lambda i, j: (i, 0)),
            pl.BlockSpec((tm, o.shape[1]), lambda i, j: (i, 0)),
            pl.BlockSpec(w_o.shape, lambda i, j: (0, 0)),
            pl.BlockSpec((1, d), lambda i, j: (0, 0)),
            pl.BlockSpec((d, tf), lambda i, j: (0, j)),
            pl.BlockSpec((d, tf), lambda i, j: (0, j)),
            pl.BlockSpec((tf, d), lambda i, j: (j, 0)),
        ],
        out_specs=pl.BlockSpec((tm, d), lambda i, j: (i, 0)),
        out_shape=jax.ShapeDtypeStruct((t, d), F32),
        scratch_shapes=[pltpu.VMEM((tm, d), BF16)],
        compiler_params=pltpu.CompilerParams(
            dimension_semantics=("parallel", "arbitrary"), vmem_limit_bytes=VMEM_LIMIT),
        name="attn_ffn",
    )(x2d, o, w_o, g, w_gate, w_up, w_down)


def _hgrn_kernel(h_ref, g_ref, win_ref, lbraw_ref, og_ref, wo_ref, fg_ref, router_ref,
                 h_out, hn_out, meta_out, meta_t_out, cnt_out, st_ref, o_scr, cnt_ref, *, layer):
    tt = h_ref.shape[0]
    width = HG_HEADS * HG_DK
    nchunk = tt // HG_CHUNK

    @pl.when(pl.program_id(1) == 0)
    def _():
        st_ref[...] = jnp.zeros_like(st_ref)

    lbraw = lbraw_ref[...]
    e = jnp.exp(lbraw - jnp.max(lbraw, axis=0, keepdims=True))
    sm = e / jnp.sum(e, axis=0, keepdims=True)
    lb = jnp.sum(sm[1:layer + 1], axis=0, keepdims=True)

    h_in = h_ref[...]
    hn = _rms(h_in, g_ref[...]).astype(BF16)
    proj = _dot(hn, win_ref[...])
    q_all = proj[:, :width] * (HG_DK ** -0.5)
    f_all = lb + (1.0 - lb) * (1.0 / (1.0 + jnp.exp(-proj[:, width:2 * width])))
    logf = jnp.log(f_all)
    k_all = 1.0 - f_all

    r = lax.broadcasted_iota(jnp.int32, (tt, tt), 0)
    c = lax.broadcasted_iota(jnp.int32, (tt, tt), 1)
    shift = HG_CHUNK.bit_length() - 1
    tril = (c <= r) & ((r >> shift) == (c >> shift))
    tril_b = jnp.where(tril, 1.0, 0.0).astype(BF16)
    hi, mid, _ = _split3(logf)
    b_all = _dot(tril_b, hi) + _dot(tril_b, mid)

    for h in range(HG_HEADS):
        sl = slice(h * HG_DK, (h + 1) * HG_DK)
        q = q_all[:, sl]
        k = k_all[:, sl]
        b = b_all[:, sl]
        v = proj[:, 2 * width + h * HG_DK:2 * width + (h + 1) * HG_DK]
        gt = proj[:, 3 * width + h * HG_DK:3 * width + (h + 1) * HG_DK]
        v_b = v.astype(BF16)
        qd = (q * jnp.exp(b)).astype(BF16)
        kd = (k * jnp.exp(-b)).astype(BF16)
        a = jnp.where(tril, _dot_nt(qd, kd), 0.0)
        o = _dot(a.astype(BF16), v_b)

        st = st_ref[h]
        inter = []
        for n in range(nchunk):
            rows = slice(n * HG_CHUNK, (n + 1) * HG_CHUNK)
            b_n = b[rows]
            b_last = b_n[HG_CHUNK - 1:HG_CHUNK]
            inter.append(_dot_nt(qd[rows], st.astype(BF16)))
            kdl = (k[rows] * jnp.exp(b_last - b_n)).astype(BF16)
            st = st * jnp.exp(b_last) + _dot_tn(v_b[rows], kdl)
        st_ref[h] = st
        o = o + jnp.concatenate(inter, axis=0)
        o = _rms(o, og_ref[...]) * _silu(gt)
        o_scr[:, sl] = o.astype(BF16)

    h_new = h_in + _dot(o_scr[...], wo_ref[...])
    h_out[...] = h_new
    hn2 = _rms(h_new, fg_ref[...])
    _store_rows_as_tiles(hn_out, hn2)

    x_hi, x_mid, _ = _split3(hn2)
    router = router_ref[...]
    r_hi = router.astype(BF16)
    r_lo = (router - r_hi.astype(F32)).astype(BF16)
    hi_terms = _dot(x_hi, jnp.concatenate([r_hi, r_lo], axis=1))
    logits = hi_terms[:, :LANES] + _dot(x_mid, r_hi) + hi_terms[:, LANES:]
    lane = lax.broadcasted_iota(jnp.int32, logits.shape, 1)
    neg = -jnp.inf
    logits = jnp.where(lane < N_EXPERTS, logits, neg)
    m1 = jnp.max(logits, axis=-1, keepdims=True)
    i1 = jnp.min(jnp.where(logits == m1, lane, LANES), axis=-1, keepdims=True)
    rest = jnp.where(lane == i1, neg, logits)
    m2 = jnp.max(rest, axis=-1, keepdims=True)
    i2 = jnp.min(jnp.where(rest == m2, lane, LANES), axis=-1, keepdims=True)
    e2 = jnp.exp(m2 - m1)
    w1 = 1.0 / (1.0 + e2)
    w2 = e2 / (1.0 + e2)

    @pl.when((pl.program_id(0) == 0) & (pl.program_id(1) == 0))
    def _():
        cnt_ref[...] = jnp.zeros_like(cnt_ref)

    onehot = jnp.where((lane == i1) | (lane == i2), 1.0, 0.0)
    earlier = jnp.where(c < r, 1.0, 0.0).astype(BF16)
    rank = _dot(earlier, onehot.astype(BF16)) + cnt_ref[...]
    cnt = cnt_ref[...] + jnp.sum(onehot, axis=0, keepdims=True)
    cnt_ref[...] = cnt
    cnt_out[...] = jnp.broadcast_to(cnt, cnt_out.shape)
    rank1 = jnp.sum(jnp.where(lane == i1, rank, 0.0), axis=-1, keepdims=True)
    rank2 = jnp.sum(jnp.where(lane == i2, rank, 0.0), axis=-1, keepdims=True)
    fields = (i1.astype(F32), i2.astype(F32), w1, w2, rank1, rank2)
    meta = jnp.zeros(logits.shape, F32)
    for idx, val in enumerate(fields):
        meta = jnp.where(lane == idx, val, meta)
    meta_out[...] = meta
    meta_t_out[...] = meta.T[:SUBLANES]


def _hgrn(h2d, g, w_in, lb_raw, og, w_o, fg, router_p, *, batch, seq, tt, layer):
    t, d = h2d.shape
    nt = seq // tt
    width = HG_HEADS * HG_DK
    tok = lambda cols: pl.BlockSpec((tt, cols), lambda b, i: (b * nt + i, 0))
    const = lambda shape: pl.BlockSpec(shape, lambda b, i: (0,) * len(shape))
    return pl.pallas_call(
        functools.partial(_hgrn_kernel, layer=layer),
        grid=(batch, nt),
        in_specs=[tok(d), const((1, d)), const(w_in.shape), const(lb_raw.shape),
                  const((1, HG_DK)), const(w_o.shape), const((1, d)), const(router_p.shape)],
        out_specs=[tok(d),
                   pl.BlockSpec((tt * (d // LANES), LANES), lambda b, i: (b * nt + i, 0)),
                   tok(LANES), pl.BlockSpec((SUBLANES, tt), lambda b, i: (0, b * nt + i)),
                   const((SUBLANES, LANES))],
        out_shape=[jax.ShapeDtypeStruct((t, d), F32),
                   jax.ShapeDtypeStruct((t * (d // LANES), LANES), F32),
                   jax.ShapeDtypeStruct((t, LANES), F32),
                   jax.ShapeDtypeStruct((SUBLANES, t), F32),
                   jax.ShapeDtypeStruct((SUBLANES, LANES), F32)],
        scratch_shapes=[pltpu.VMEM((HG_HEADS, HG_DK, HG_DK), F32),
                        pltpu.VMEM((tt, width), BF16),
                        pltpu.VMEM((1, LANES), F32)],
        compiler_params=pltpu.CompilerParams(
            dimension_semantics=("arbitrary", "arbitrary"), vmem_limit_bytes=VMEM_LIMIT),
        name="hgrn",
    )(h2d, g, w_in, lb_raw, og, w_o, fg, router_p)


def _expert_kernel(src_ref, te_ref, tv_ref, x_hbm, wg_ref, wu_ref, wd_ref, y_ref,
                   xbuf, xb_ref, acc_ref, sems):
    del te_ref
    r = pl.program_id(0)
    j = pl.program_id(1)
    n_tiles = pl.num_programs(0)
    nj = pl.num_programs(1)
    tm = xb_ref.shape[0]
    nc = xb_ref.shape[1] // LANES
    part = tm // nj
    valid = tv_ref[r] != 0
    has_next = r + 1 < n_tiles

    def row_copy(tile, row):
        slot = tile % 2
        src = pl.multiple_of(src_ref[tile * tm + row] * nc, nc)
        return pltpu.make_async_copy(x_hbm.at[pl.ds(src, nc)],
                                     xbuf.at[slot, pl.ds(row * nc, nc)], sems.at[slot])

    def swiglu_step():
        x = xb_ref[...]
        a = _silu(_dot(x, wg_ref[...])) * _dot(x, wu_ref[...])
        acc_ref[...] += _dot(a.astype(BF16), wd_ref[...])

    @pl.when((r == 0) & (j == 0))
    def _():
        lax.fori_loop(0, tm, lambda i, c: (row_copy(r, i).start(), c)[1], 0)

    @pl.when((j == 0) & ((r == 0) | (tv_ref[jnp.maximum(r - 1, 0)] != 0)))
    def _():
        pltpu.make_async_copy(x_hbm.at[pl.ds(0, tm * nc)], xbuf.at[r % 2],
                              sems.at[r % 2]).wait()
        xb_ref[...] = _load_rows_from_tiles(xbuf.at[r % 2], nc).astype(BF16)

    @pl.when(j == 0)
    def _():
        acc_ref[...] = jnp.zeros_like(acc_ref)

    @pl.when(valid & has_next)
    def _():
        for i in range(part):
            row_copy(r + 1, j * part + i).start(priority=i % DMA_THREADS)
        swiglu_step()

    @pl.when(valid & jnp.logical_not(has_next))
    def _():
        swiglu_step()

    @pl.when(j == nj - 1)
    def _():
        _store_rows_as_tiles(y_ref, acc_ref[...])


def _experts(src_rows, tile_expert, tile_valid, x, w_gate, w_up, w_down, *, tm, tf):
    n_rows = src_rows.shape[0]
    d = w_gate.shape[1]
    nc = d // LANES
    f = w_gate.shape[2]
    nj = f // tf
    col = lambda r, j, tv: jnp.where(tv[r] != 0, j, nj - 1)
    return pl.pallas_call(
        _expert_kernel,
        grid_spec=pltpu.PrefetchScalarGridSpec(
            num_scalar_prefetch=3,
            grid=(n_rows // tm, nj),
            in_specs=[
                pl.BlockSpec(memory_space=pl.ANY),
                pl.BlockSpec((None, d, tf), lambda r, j, s, te, tv: (te[r], 0, col(r, j, tv))),
                pl.BlockSpec((None, d, tf), lambda r, j, s, te, tv: (te[r], 0, col(r, j, tv))),
                pl.BlockSpec((None, tf, d), lambda r, j, s, te, tv: (te[r], col(r, j, tv), 0)),
            ],
            out_specs=pl.BlockSpec((tm * nc, LANES), lambda r, j, s, te, tv: (r, 0)),
            scratch_shapes=[pltpu.VMEM((2, tm * nc, LANES), F32), pltpu.VMEM((tm, d), BF16),
                            pltpu.VMEM((tm, d), F32), pltpu.SemaphoreType.DMA((2,))],
        ),
        out_shape=jax.ShapeDtypeStruct((n_rows * nc, LANES), F32),
        compiler_params=pltpu.CompilerParams(
            dimension_semantics=("arbitrary", "arbitrary"), vmem_limit_bytes=VMEM_LIMIT),
        name="moe_experts",
    )(src_rows, tile_expert, tile_valid, x, w_gate, w_up, w_down)


def _combine_kernel(d1_ref, d2_ref, y_hbm, h_ref, meta_ref, fg_ref, out_ref, a_ref, b_ref, sem):
    tc = h_ref.shape[0]
    base = pl.program_id(0) * tc

    nc = h_ref.shape[1] // LANES

    def issue(r, _):
        for thread, (d_ref, buf) in enumerate(((d1_ref, a_ref), (d2_ref, b_ref))):
            src = pl.multiple_of(d_ref[base + r] * nc, nc)
            pltpu.make_async_copy(y_hbm.at[pl.ds(src, nc)], buf.at[pl.ds(r * nc, nc)],
                                  sem).start(priority=thread % DMA_THREADS)
        return 0

    lax.fori_loop(0, tc, issue, 0, unroll=16)
    pltpu.make_async_copy(y_hbm.at[pl.ds(0, tc * nc)], a_ref, sem).wait()
    pltpu.make_async_copy(y_hbm.at[pl.ds(0, tc * nc)], b_ref, sem).wait()
    meta = meta_ref[...]
    w1 = meta[:, 2:3]
    w2 = meta[:, 3:4]
    moe = w1 * _load_rows_from_tiles(a_ref, nc) + w2 * _load_rows_from_tiles(b_ref, nc)
    out_ref[...] = _rms(h_ref[...] + moe, fg_ref[...])


def _combine(dest1, dest2, y, h2d, meta, fg, *, tc):
    t, d = h2d.shape
    return pl.pallas_call(
        _combine_kernel,
        grid_spec=pltpu.PrefetchScalarGridSpec(
            num_scalar_prefetch=2,
            grid=(t // tc,),
            in_specs=[
                pl.BlockSpec(memory_space=pl.ANY),
                pl.BlockSpec((tc, d), lambda i, d1, d2: (i, 0)),
                pl.BlockSpec((tc, LANES), lambda i, d1, d2: (i, 0)),
                pl.BlockSpec((1, d), lambda i, d1, d2: (0, 0)),
            ],
            out_specs=pl.BlockSpec((tc, d), lambda i, d1, d2: (i, 0)),
            scratch_shapes=[pltpu.VMEM((tc * d // LANES, LANES), F32),
                            pltpu.VMEM((tc * d // LANES, LANES), F32),
                            pltpu.SemaphoreType.DMA(())],
        ),
        out_shape=jax.ShapeDtypeStruct((t, d), F32),
        compiler_params=pltpu.CompilerParams(
            dimension_semantics=("arbitrary",), vmem_limit_bytes=VMEM_LIMIT),
        name="moe_combine",
    )(dest1, dest2, y, h2d, meta, fg)


def _routing_plan(meta_t, counts, *, tm, n_tiles):
    e1 = meta_t[0].astype(jnp.int32)
    e2 = meta_t[1].astype(jnp.int32)
    cnt = counts[0, :N_EXPERTS].astype(jnp.int32)
    padded = (cnt + tm - 1) // tm * tm
    ends = jnp.cumsum(padded)
    offs = ends - padded
    dest1 = offs[e1] + meta_t[4].astype(jnp.int32)
    dest2 = offs[e2] + meta_t[5].astype(jnp.int32)
    start = jnp.arange(n_tiles, dtype=jnp.int32) * tm
    tile_valid = (start < ends[-1]).astype(jnp.int32)
    last = jnp.sum((ends < ends[-1]).astype(jnp.int32))
    passed = jnp.sum((ends[None, :] <= start[:, None]).astype(jnp.int32), axis=1)
    src_rows = _source_rows(dest1, dest2, n_tiles * tm)
    return dest1, dest2, src_rows, jnp.minimum(passed, last), tile_valid


def _source_rows_kernel(d1_ref, d2_ref, src_ref):
    def clear(i, _):
        src_ref[i] = 0
        return 0

    def place(t, _):
        src_ref[d1_ref[t]] = t
        src_ref[d2_ref[t]] = t
        return 0

    lax.fori_loop(0, src_ref.shape[0], clear, 0, unroll=16)
    lax.fori_loop(0, d1_ref.shape[0], place, 0, unroll=8)


def _source_rows(dest1, dest2, n_rows):
    return pl.pallas_call(
        _source_rows_kernel,
        in_specs=[pl.BlockSpec(memory_space=pltpu.SMEM), pl.BlockSpec(memory_space=pltpu.SMEM)],
        out_specs=pl.BlockSpec(memory_space=pltpu.SMEM),
        out_shape=jax.ShapeDtypeStruct((n_rows,), jnp.int32),
        name="moe_source_rows",
    )(dest1, dest2)


def _pad_heads(w, head_dim):
    k = w.shape[0]
    w = w.reshape(k, MLA_HEADS, head_dim)
    w = jnp.pad(w, ((0, 0), (0, 0), (0, LANES - head_dim)))
    return w.reshape(k, MLA_HEADS * LANES)


def _pick(n, cap):
    t = min(n, cap)
    while n % t:
        t //= 2
    return t


def kernel(x, positions, mix_norm, ffn_norm, final_norm, mla_w_in, mla_q_norm, mla_w_q_up,
           mla_kv_norm, mla_w_kv_up, mla_w_o, hgrn_w_in, hgrn_lower_bounds, hgrn_out_norm,
           hgrn_w_o, ffn_w_gate, ffn_w_up, ffn_w_down, moe_router, moe_w_gate, moe_w_up,
           moe_w_down):
    batch, seq, d = x.shape
    t = batch * seq
    x2d = x.reshape(t, d)
    row = lambda v: v.reshape(1, -1)

    w_in = mla_w_in[0]
    n_lat = MLA_Q_LORA + MLA_KV_LORA
    w_in_p = w_in[:, :n_lat].astype(BF16)
    wkr_t = jnp.pad(w_in[:, n_lat:].T,
                    ((MLA_NOPE, LANES - MLA_NOPE - MLA_ROPE), (0, 0))).astype(BF16)
    wq_t = _pad_heads(mla_w_q_up[0], MLA_NOPE + MLA_ROPE).T.astype(BF16)
    w_kv = mla_w_kv_up[0].reshape(MLA_KV_LORA, MLA_HEADS, MLA_NOPE + MLA_V)
    wk_p = _pad_heads(w_kv[:, :, :MLA_NOPE].reshape(MLA_KV_LORA, -1), MLA_NOPE).astype(BF16)
    wv_t = w_kv[:, :, MLA_NOPE:].reshape(MLA_KV_LORA, -1).T.astype(BF16)
    inv_freq = ROPE_THETA ** (-jnp.arange(0, MLA_ROPE, 2, dtype=F32) / MLA_ROPE)
    freq = inv_freq.reshape(MLA_ROPE // 2, 1)
    pos = positions.astype(F32).reshape(1, t)

    q_t, k, v_t = _mla_proj(x2d, pos, row(mix_norm[0]), w_in_p, wkr_t, row(mla_q_norm[0]), wq_t,
                            row(mla_kv_norm[0]), wk_p, wv_t, freq,
                            batch=batch, seq=seq, tm=_pick(seq, 512))
    o = _mla_attn(q_t, k, v_t, tq=_pick(seq, 512), hp=4).reshape(t, MLA_HEADS * MLA_V)
    h = _attn_ffn(x2d, o, mla_w_o[0].astype(BF16), row(ffn_norm[0]),
                  ffn_w_gate[0].astype(BF16), ffn_w_up[0].astype(BF16),
                  ffn_w_down[0].astype(BF16), tm=_pick(t, 512), tf=1792)

    router_p = jnp.pad(moe_router[0], ((0, 0), (0, LANES - N_EXPERTS)))
    h, hn, meta, meta_t, counts = _hgrn(h, row(mix_norm[1]), hgrn_w_in[0].astype(BF16),
                                hgrn_lower_bounds, row(hgrn_out_norm[0]),
                                hgrn_w_o[0].astype(BF16), row(ffn_norm[1]), router_p,
                                batch=batch, seq=seq, tt=_pick(seq, 256), layer=1)
    tm = _pick(t, 512)
    n_tiles = 2 * t // tm + N_EXPERTS
    dest1, dest2, src_rows, tile_expert, tile_valid = _routing_plan(
        meta_t, counts, tm=tm, n_tiles=n_tiles)
    y = _experts(src_rows, tile_expert, tile_valid, hn, moe_w_gate[0].astype(BF16),
                 moe_w_up[0].astype(BF16), moe_w_down[0].astype(BF16), tm=tm, tf=1792)
    out = _combine(dest1, dest2, y, h, meta, row(final_norm), tc=_pick(t, 512))
    return out.reshape(batch, seq, d)
```

```python
import functools

import jax
import jax.numpy as jnp
from jax import lax
from jax.experimental import pallas as pl
from jax.experimental.pallas import tpu as pltpu

EPS = 1e-6
LANES = 128
SUBLANES = 8

MLA_HEADS = 16
MLA_Q_LORA = 384
MLA_KV_LORA = 256
MLA_NOPE = 64
MLA_ROPE = 32
MLA_V = 64
ROPE_THETA = 10000.0

HG_HEADS = 8
HG_DK = 128
HG_CHUNK = 64

N_EXPERTS = 8

VMEM_LIMIT = 56 * 1024 * 1024
DMA_THREADS = 2

BF16 = jnp.bfloat16
F32 = jnp.float32


def _dot(a, b):
    return jnp.dot(a, b, preferred_element_type=F32)


def _dot_nt(a, b):
    return lax.dot_general(a, b, (((1,), (1,)), ((), ())), preferred_element_type=F32)


def _dot_tn(a, b):
    return lax.dot_general(a, b, (((0,), (0,)), ((), ())), preferred_element_type=F32)


def _rms(x, g):
    return x * lax.rsqrt(jnp.mean(x * x, axis=-1, keepdims=True) + EPS) * g


def _silu(x):
    return x * (1.0 / (1.0 + jnp.exp(-x)))


def _store_rows_as_tiles(ref, x):
    n, d = x.shape
    nc = d // LANES
    for c in range(nc):
        ref[pl.ds(c, n, stride=nc), :] = x[:, c * LANES:(c + 1) * LANES]


def _load_rows_from_tiles(ref, nc):
    n = ref.shape[0] // nc
    return jnp.concatenate([ref[pl.ds(c, n, stride=nc), :] for c in range(nc)], axis=1)


def _split3(x):
    hi = x.astype(BF16)
    r = x - hi.astype(F32)
    mid = r.astype(BF16)
    lo = (r - mid.astype(F32)).astype(BF16)
    return hi, mid, lo


def _mla_proj_kernel(x_ref, pos_ref, g_ref, win_ref, wkr_ref, qg_ref, wq_ref, kvg_ref, wk_ref,
                     wv_ref, freq_ref, q_out, k_out, v_out, *, scale):
    half = MLA_ROPE // 2
    x1_rows = slice(MLA_NOPE, MLA_NOPE + half)
    x2_rows = slice(MLA_NOPE + half, MLA_NOPE + MLA_ROPE)
    hn = _rms(x_ref[...], g_ref[...]).astype(BF16)
    proj = _dot(hn, win_ref[...])
    cqn = _rms(proj[:, :MLA_Q_LORA], qg_ref[...]).astype(BF16)
    ckvn = _rms(proj[:, MLA_Q_LORA:], kvg_ref[...]).astype(BF16)

    ang = freq_ref[...] * pos_ref[...]
    cos = jnp.cos(ang)
    sin = jnp.sin(ang)

    def rope_t(slot, mul):
        x1 = slot[x1_rows]
        x2 = slot[x2_rows]
        return jnp.concatenate([slot[:MLA_NOPE] * mul, (x1 * cos - x2 * sin) * mul,
                                (x2 * cos + x1 * sin) * mul, slot[MLA_NOPE + MLA_ROPE:]], axis=0)

    kr_t = _dot_nt(wkr_ref[...], hn)
    kr_roped = rope_t(kr_t, 1.0).T
    q_t = _dot_nt(wq_ref[...], cqn)
    k_pad = _dot(ckvn, wk_ref[...])
    for h in range(MLA_HEADS):
        sl = slice(h * LANES, (h + 1) * LANES)
        q_out[h] = rope_t(q_t[sl], scale).astype(BF16)
        k_out[h] = (k_pad[:, sl] + kr_roped).astype(BF16)
    v_out[...] = _dot_nt(wv_ref[...], ckvn).astype(BF16)


def _mla_proj(x2d, pos, g, w_in_p, wkr_t, qg, wq_t, kvg, wk_p, wv_t, freq, *, batch, seq, tm):
    t = x2d.shape[0]
    d = x2d.shape[1]
    nt = seq // tm
    const = lambda shape: pl.BlockSpec(shape, lambda i: (0,) * len(shape))
    scale = (MLA_NOPE + MLA_ROPE) ** -0.5 * 1.4426950408889634
    return pl.pallas_call(
        functools.partial(_mla_proj_kernel, scale=scale),
        grid=(t // tm,),
        in_specs=[
            pl.BlockSpec((tm, d), lambda i: (i, 0)),
            pl.BlockSpec((1, tm), lambda i: (0, i)),
            const((1, d)),
            const(w_in_p.shape),
            const(wkr_t.shape),
            const((1, MLA_Q_LORA)),
            const(wq_t.shape),
            const((1, MLA_KV_LORA)),
            const(wk_p.shape),
            const(wv_t.shape),
            const(freq.shape),
        ],
        out_specs=[
            pl.BlockSpec((None, MLA_HEADS, LANES, tm), lambda i: (i // nt, 0, 0, i % nt)),
            pl.BlockSpec((None, MLA_HEADS, tm, LANES), lambda i: (i // nt, 0, i % nt, 0)),
            pl.BlockSpec((None, MLA_HEADS * MLA_V, tm), lambda i: (i // nt, 0, i % nt)),
        ],
        out_shape=[
            jax.ShapeDtypeStruct((batch, MLA_HEADS, LANES, seq), BF16),
            jax.ShapeDtypeStruct((batch, MLA_HEADS, seq, LANES), BF16),
            jax.ShapeDtypeStruct((batch, MLA_HEADS * MLA_V, seq), BF16),
        ],
        compiler_params=pltpu.CompilerParams(
            dimension_semantics=("parallel",), vmem_limit_bytes=VMEM_LIMIT),
        name="mla_proj",
    )(x2d, pos, g, w_in_p, wkr_t, qg, wq_t, kvg, wk_p, wv_t, freq)


def _attn_kernel(q_ref, k_ref, v_ref, o_ref, acc_ref, m_ref, vx_ref, sa_ref, sb_ref, *, tq):
    seq = k_ref.shape[1]
    nq = seq // tq
    hp = k_ref.shape[0]
    for h in range(hp):
        vx_ref[h, :MLA_V] = v_ref[h * MLA_V:(h + 1) * MLA_V]
        vx_ref[h, MLA_V:] = jnp.ones((vx_ref.shape[1] - MLA_V, seq), BF16)
    causal = (lax.broadcasted_iota(jnp.int32, (tq, tq), 0)
              <= lax.broadcasted_iota(jnp.int32, (tq, tq), 1))

    def scores(dst, blk, q0):
        k0 = pl.multiple_of(blk * tq, tq)
        for h in range(hp):
            dst[h] = _dot(k_ref[h, pl.ds(k0, tq), :], q_ref[h, :, pl.ds(q0, tq)])

    def softmax_pv(src, blk, masked):
        k0 = pl.multiple_of(blk * tq, tq)
        for h in range(hp):
            s = src[h]
            if masked:
                s = jnp.where(causal, s, -jnp.inf)
            m = m_ref[h]
            m_new = jnp.maximum(m, jnp.max(s, axis=0, keepdims=True))
            p = jnp.exp2(s - m_new)
            alpha = jnp.exp2(m - m_new)
            m_ref[h] = m_new
            acc_ref[h] = alpha * acc_ref[h] + _dot(vx_ref[h, :, pl.ds(k0, tq)], p.astype(BF16))

    def q_block(qi, _):
        q0 = pl.multiple_of(qi * tq, tq)
        acc_ref[...] = jnp.zeros_like(acc_ref)
        m_ref[...] = jnp.full_like(m_ref, -jnp.inf)

        scores(sa_ref, 0, q0)

        def pair(jp, _):
            scores(sb_ref, 2 * jp + 1, q0)
            softmax_pv(sa_ref, 2 * jp, False)
            scores(sa_ref, 2 * jp + 2, q0)
            softmax_pv(sb_ref, 2 * jp + 1, False)
            return 0

        lax.fori_loop(0, qi // 2, pair, 0)

        @pl.when(qi % 2 == 1)
        def _():
            scores(sb_ref, qi, q0)
            softmax_pv(sa_ref, qi - 1, False)
            softmax_pv(sb_ref, qi, True)

        @pl.when(qi % 2 == 0)
        def _():
            softmax_pv(sa_ref, qi, True)

        out = jnp.concatenate([acc_ref[h, :MLA_V] / acc_ref[h, MLA_V:MLA_V + 1]
                               for h in range(hp)], axis=0)
        o_ref[pl.ds(q0, tq), :] = out.T.astype(BF16)
        return 0

    lax.fori_loop(0, nq, q_block, 0)


def _mla_attn(q_t, k, v_t, *, tq, hp):
    batch, heads, seq, _ = k.shape
    return pl.pallas_call(
        functools.partial(_attn_kernel, tq=tq),
        grid=(batch, heads // hp),
        in_specs=[
            pl.BlockSpec((None, hp, LANES, seq), lambda b, g: (b, g, 0, 0)),
            pl.BlockSpec((None, hp, seq, LANES), lambda b, g: (b, g, 0, 0)),
            pl.BlockSpec((None, hp * MLA_V, seq), lambda b, g: (b, g, 0)),
        ],
        out_specs=pl.BlockSpec((None, seq, hp * MLA_V), lambda b, g: (b, 0, g)),
        out_shape=jax.ShapeDtypeStruct((batch, seq, heads * MLA_V), BF16),
        scratch_shapes=[pltpu.VMEM((hp, MLA_V + 16, tq), F32),
                        pltpu.VMEM((hp, 1, tq), F32), pltpu.VMEM((hp, MLA_V + 16, seq), BF16),
                        pltpu.VMEM((hp, tq, tq), F32), pltpu.VMEM((hp, tq, tq), F32)],
        compiler_params=pltpu.CompilerParams(
            dimension_semantics=("parallel", "parallel"), vmem_limit_bytes=VMEM_LIMIT),
        name="mla_attn",
    )(q_t, k, v_t)


def _attn_ffn_kernel(x_ref, o_ref, wo_ref, g_ref, wg_ref, wu_ref, wd_ref, out_ref, hn_ref):
    j = pl.program_id(1)

    @pl.when(j == 0)
    def _():
        h1 = x_ref[...] + _dot(o_ref[...], wo_ref[...])
        out_ref[...] = h1
        hn_ref[...] = _rms(h1, g_ref[...]).astype(BF16)

    hn = hn_ref[...]
    a = _silu(_dot(hn, wg_ref[...])) * _dot(hn, wu_ref[...])
    out_ref[...] += _dot(a.astype(BF16), wd_ref[...])


def _attn_ffn(x2d, o, w_o, g, w_gate, w_up, w_down, *, tm, tf):
    t, d = x2d.shape
    f = w_gate.shape[1]
    return pl.pallas_call(
        _attn_ffn_kernel,
        grid=(t // tm, f // tf),
        in_specs=[
            pl.BlockSpec((tm, d), lambda i, j: (i, 0)),
            pl.BlockSpec((tm, o.shape[1]), lambda i, j: (i, 0)),
            pl.BlockSpec(w_o.shape, lambda i, j: (0, 0)),
            pl.BlockSpec((1, d), lambda i, j: (0, 0)),
            pl.BlockSpec((d, tf), lambda i, j: (0, j)),
            pl.BlockSpec((d, tf), lambda i, j: (0, j)),
            pl.BlockSpec((tf, d), lambda i, j: (j, 0)),
        ],
        out_specs=pl.BlockSpec((tm, d), lambda i, j: (i, 0)),
        out_shape=jax.ShapeDtypeStruct((t, d), F32),
        scratch_shapes=[pltpu.VMEM((tm, d), BF16)],
        compiler_params=pltpu.CompilerParams(
            dimension_semantics=("parallel", "arbitrary"), vmem_limit_bytes=VMEM_LIMIT),
        name="attn_ffn",
    )(x2d, o, w_o, g, w_gate, w_up, w_down)


def _hgrn_kernel(h_ref, g_ref, win_ref, lbraw_ref, og_ref, wo_ref, fg_ref, router_ref,
                 h_out, hn_out, meta_out, meta_t_out, cnt_out, st_ref, o_scr, cnt_ref, *, layer):
    tt = h_ref.shape[0]
    width = HG_HEADS * HG_DK
    nchunk = tt // HG_CHUNK

    @pl.when(pl.program_id(1) == 0)
    def _():
        st_ref[...] = jnp.zeros_like(st_ref)

    lbraw = lbraw_ref[...]
    e = jnp.exp(lbraw - jnp.max(lbraw, axis=0, keepdims=True))
    sm = e / jnp.sum(e, axis=0, keepdims=True)
    lb = jnp.sum(sm[1:layer + 1], axis=0, keepdims=True)

    h_in = h_ref[...]
    hn = _rms(h_in, g_ref[...]).astype(BF16)
    proj = _dot(hn, win_ref[...])
    q_all = proj[:, :width] * (HG_DK ** -0.5)
    f_all = lb + (1.0 - lb) * (1.0 / (1.0 + jnp.exp(-proj[:, width:2 * width])))
    logf = jnp.log(f_all)
    k_all = 1.0 - f_all

    r = lax.broadcasted_iota(jnp.int32, (tt, tt), 0)
    c = lax.broadcasted_iota(jnp.int32, (tt, tt), 1)
    shift = HG_CHUNK.bit_length() - 1
    tril = (c <= r) & ((r >> shift) == (c >> shift))
    in_chunk = lax.broadcasted_iota(jnp.int32, (tt, 1), 0) & (HG_CHUNK - 1)
    b_all = logf
    for step in (1 << i for i in range(shift)):
        b_all = b_all + jnp.where(in_chunk >= step, pltpu.roll(b_all, step, 0), 0.0)

    for h in range(HG_HEADS):
        sl = slice(h * HG_DK, (h + 1) * HG_DK)
        q = q_all[:, sl]
        k = k_all[:, sl]
        b = b_all[:, sl]
        v = proj[:, 2 * width + h * HG_DK:2 * width + (h + 1) * HG_DK]
        gt = proj[:, 3 * width + h * HG_DK:3 * width + (h + 1) * HG_DK]
        v_b = v.astype(BF16)
        qd = (q * jnp.exp(b)).astype(BF16)
        kd = (k * jnp.exp(-b)).astype(BF16)
        a = jnp.where(tril, _dot_nt(qd, kd), 0.0)
        o = _dot(a.astype(BF16), v_b)

        st = st_ref[h]
        inter = []
        for n in range(nchunk):
            rows = slice(n * HG_CHUNK, (n + 1) * HG_CHUNK)
            b_n = b[rows]
            b_last = b_n[HG_CHUNK - 1:HG_CHUNK]
            inter.append(_dot_nt(qd[rows], st.astype(BF16)))
            kdl = (k[rows] * jnp.exp(b_last - b_n)).astype(BF16)
            st = st * jnp.exp(b_last) + _dot_tn(v_b[rows], kdl)
        st_ref[h] = st
        o = o + jnp.concatenate(inter, axis=0)
        o = _rms(o, og_ref[...]) * _silu(gt)
        o_scr[:, sl] = o.astype(BF16)

    h_new = h_in + _dot(o_scr[...], wo_ref[...])
    h_out[...] = h_new
    hn2 = _rms(h_new, fg_ref[...])
    _store_rows_as_tiles(hn_out, hn2)

    x_hi, x_mid, _ = _split3(hn2)
    router = router_ref[...]
    r_hi = router.astype(BF16)
    r_lo = (router - r_hi.astype(F32)).astype(BF16)
    hi_terms = _dot(x_hi, jnp.concatenate([r_hi, r_lo], axis=1))
    logits = hi_terms[:, :LANES] + _dot(x_mid, r_hi) + hi_terms[:, LANES:]
    lane = lax.broadcasted_iota(jnp.int32, logits.shape, 1)
    neg = -jnp.inf
    logits = jnp.where(lane < N_EXPERTS, logits, neg)
    m1 = jnp.max(logits, axis=-1, keepdims=True)
    i1 = jnp.min(jnp.where(logits == m1, lane, LANES), axis=-1, keepdims=True)
    rest = jnp.where(lane == i1, neg, logits)
    m2 = jnp.max(rest, axis=-1, keepdims=True)
    i2 = jnp.min(jnp.where(rest == m2, lane, LANES), axis=-1, keepdims=True)
    e2 = jnp.exp(m2 - m1)
    w1 = 1.0 / (1.0 + e2)
    w2 = e2 / (1.0 + e2)

    @pl.when((pl.program_id(0) == 0) & (pl.program_id(1) == 0))
    def _():
        cnt_ref[...] = jnp.zeros_like(cnt_ref)

    onehot = jnp.where((lane == i1) | (lane == i2), 1.0, 0.0)
    earlier = jnp.where(c < r, 1.0, 0.0).astype(BF16)
    rank = _dot(earlier, onehot.astype(BF16)) + cnt_ref[...]
    cnt = cnt_ref[...] + jnp.sum(onehot, axis=0, keepdims=True)
    cnt_ref[...] = cnt
    cnt_out[...] = jnp.broadcast_to(cnt, cnt_out.shape)
    rank1 = jnp.sum(jnp.where(lane == i1, rank, 0.0), axis=-1, keepdims=True)
    rank2 = jnp.sum(jnp.where(lane == i2, rank, 0.0), axis=-1, keepdims=True)
    fields = (i1.astype(F32), i2.astype(F32), w1, w2, rank1, rank2)
    meta = jnp.zeros(logits.shape, F32)
    for idx, val in enumerate(fields):
        meta = jnp.where(lane == idx, val, meta)
    meta_out[...] = meta
    meta_t_out[...] = meta.T[:SUBLANES]


def _hgrn(h2d, g, w_in, lb_raw, og, w_o, fg, router_p, *, batch, seq, tt, layer):
    t, d = h2d.shape
    nt = seq // tt
    width = HG_HEADS * HG_DK
    tok = lambda cols: pl.BlockSpec((tt, cols), lambda b, i: (b * nt + i, 0))
    const = lambda shape: pl.BlockSpec(shape, lambda b, i: (0,) * len(shape))
    return pl.pallas_call(
        functools.partial(_hgrn_kernel, layer=layer),
        grid=(batch, nt),
        in_specs=[tok(d), const((1, d)), const(w_in.shape), const(lb_raw.shape),
                  const((1, HG_DK)), const(w_o.shape), const((1, d)), const(router_p.shape)],
        out_specs=[tok(d),
                   pl.BlockSpec((tt * (d // LANES), LANES), lambda b, i: (b * nt + i, 0)),
                   tok(LANES), pl.BlockSpec((SUBLANES, tt), lambda b, i: (0, b * nt + i)),
                   const((SUBLANES, LANES))],
        out_shape=[jax.ShapeDtypeStruct((t, d), F32),
                   jax.ShapeDtypeStruct((t * (d // LANES), LANES), F32),
                   jax.ShapeDtypeStruct((t, LANES), F32),
                   jax.ShapeDtypeStruct((SUBLANES, t), F32),
                   jax.ShapeDtypeStruct((SUBLANES, LANES), F32)],
        scratch_shapes=[pltpu.VMEM((HG_HEADS, HG_DK, HG_DK), F32),
                        pltpu.VMEM((tt, width), BF16),
                        pltpu.VMEM((1, LANES), F32)],
        compiler_params=pltpu.CompilerParams(
            dimension_semantics=("arbitrary", "arbitrary"), vmem_limit_bytes=VMEM_LIMIT),
        name="hgrn",
    )(h2d, g, w_in, lb_raw, og, w_o, fg, router_p)


def _expert_kernel(src_ref, te_ref, tv_ref, x_hbm, wg_ref, wu_ref, wd_ref, y_ref,
                   xbuf, xb_ref, acc_ref, sems):
    del te_ref
    r = pl.program_id(0)
    j = pl.program_id(1)
    n_tiles = pl.num_programs(0)
    nj = pl.num_programs(1)
    tm = xb_ref.shape[0]
    nc = xb_ref.shape[1] // LANES
    part = tm // nj
    valid = tv_ref[r] != 0
    has_next = r + 1 < n_tiles

    def row_copy(tile, row):
        slot = tile % 2
        src = pl.multiple_of(src_ref[tile * tm + row] * nc, nc)
        return pltpu.make_async_copy(x_hbm.at[pl.ds(src, nc)],
                                     xbuf.at[slot, pl.ds(row * nc, nc)], sems.at[slot])

    def swiglu(x):
        a = _silu(_dot(x, wg_ref[...])) * _dot(x, wu_ref[...])
        return _dot(a.astype(BF16), wd_ref[...])

    def swiglu_step(first):
        if not first:
            acc_ref[...] += swiglu(xb_ref[...])
            return
        half = tm // 2
        for hlf in range(2):
            rows = slice(hlf * half, (hlf + 1) * half)
            x = _load_rows_from_tiles(xbuf.at[r % 2, pl.ds(hlf * half * nc, half * nc)], nc)
            x = x.astype(BF16)
            xb_ref[rows] = x
            acc_ref[rows] = swiglu(x)

    @pl.when((r == 0) & (j == 0))
    def _():
        lax.fori_loop(0, tm, lambda i, c: (row_copy(r, i).start(), c)[1], 0)

    @pl.when((j == 0) & ((r == 0) | (tv_ref[jnp.maximum(r - 1, 0)] != 0)))
    def _():
        pltpu.make_async_copy(x_hbm.at[pl.ds(0, tm * nc)], xbuf.at[r % 2],
                              sems.at[r % 2]).wait()

    for first in (True, False):
        at_step = (j == 0) if first else (j > 0)

        @pl.when(valid & has_next & at_step)
        def _():
            for i in range(part):
                row_copy(r + 1, j * part + i).start(priority=i % DMA_THREADS)
            swiglu_step(first)

        @pl.when(valid & jnp.logical_not(has_next) & at_step)
        def _():
            swiglu_step(first)

    @pl.when(valid & (j == nj - 1))
    def _():
        _store_rows_as_tiles(y_ref, acc_ref[...])

    @pl.when(jnp.logical_not(valid) & (j == nj - 1))
    def _():
        y_ref[...] = jnp.zeros_like(y_ref)


def _experts(src_rows, tile_expert, tile_valid, x, w_gate, w_up, w_down, *, tm, tf):
    n_rows = src_rows.shape[0]
    d = w_gate.shape[1]
    nc = d // LANES
    f = w_gate.shape[2]
    nj = f // tf
    col = lambda r, j, tv: jnp.where(tv[r] != 0, j, nj - 1)
    return pl.pallas_call(
        _expert_kernel,
        grid_spec=pltpu.PrefetchScalarGridSpec(
            num_scalar_prefetch=3,
            grid=(n_rows // tm, nj),
            in_specs=[
                pl.BlockSpec(memory_space=pl.ANY),
                pl.BlockSpec((None, d, tf), lambda r, j, s, te, tv: (te[r], 0, col(r, j, tv))),
                pl.BlockSpec((None, d, tf), lambda r, j, s, te, tv: (te[r], 0, col(r, j, tv))),
                pl.BlockSpec((None, tf, d), lambda r, j, s, te, tv: (te[r], col(r, j, tv), 0)),
            ],
            out_specs=pl.BlockSpec((tm * nc, LANES), lambda r, j, s, te, tv: (r, 0)),
            scratch_shapes=[pltpu.VMEM((2, tm * nc, LANES), F32), pltpu.VMEM((tm, d), BF16),
                            pltpu.VMEM((tm, d), F32), pltpu.SemaphoreType.DMA((2,))],
        ),
        out_shape=jax.ShapeDtypeStruct((n_rows * nc, LANES), F32),
        compiler_params=pltpu.CompilerParams(
            dimension_semantics=("arbitrary", "arbitrary"), vmem_limit_bytes=VMEM_LIMIT),
        name="moe_experts",
    )(src_rows, tile_expert, tile_valid, x, w_gate, w_up, w_down)


def _combine_kernel(d1_ref, d2_ref, y_hbm, h_ref, meta_ref, fg_ref, out_ref, a_ref, b_ref, sem):
    tc = h_ref.shape[0]
    base = pl.program_id(0) * tc

    nc = h_ref.shape[1] // LANES

    def issue(r, _):
        for thread, (d_ref, buf) in enumerate(((d1_ref, a_ref), (d2_ref, b_ref))):
            src = pl.multiple_of(d_ref[base + r] * nc, nc)
            pltpu.make_async_copy(y_hbm.at[pl.ds(src, nc)], buf.at[pl.ds(r * nc, nc)],
                                  sem).start(priority=thread % DMA_THREADS)
        return 0

    lax.fori_loop(0, tc, issue, 0, unroll=16)
    pltpu.make_async_copy(y_hbm.at[pl.ds(0, tc * nc)], a_ref, sem).wait()
    pltpu.make_async_copy(y_hbm.at[pl.ds(0, tc * nc)], b_ref, sem).wait()
    meta = meta_ref[...]
    w1 = meta[:, 2:3]
    w2 = meta[:, 3:4]
    moe = w1 * _load_rows_from_tiles(a_ref, nc) + w2 * _load_rows_from_tiles(b_ref, nc)
    out_ref[...] = _rms(h_ref[...] + moe, fg_ref[...])


def _combine(dest1, dest2, y, h2d, meta, fg, *, tc):
    t, d = h2d.shape
    return pl.pallas_call(
        _combine_kernel,
        grid_spec=pltpu.PrefetchScalarGridSpec(
            num_scalar_prefetch=2,
            grid=(t // tc,),
            in_specs=[
                pl.BlockSpec(memory_space=pl.ANY),
                pl.BlockSpec((tc, d), lambda i, d1, d2: (i, 0)),
                pl.BlockSpec((tc, LANES), lambda i, d1, d2: (i, 0)),
                pl.BlockSpec((1, d), lambda i, d1, d2: (0, 0)),
            ],
            out_specs=pl.BlockSpec((tc, d), lambda i, d1, d2: (i, 0)),
            scratch_shapes=[pltpu.VMEM((tc * d // LANES, LANES), F32),
                            pltpu.VMEM((tc * d // LANES, LANES), F32),
                            pltpu.SemaphoreType.DMA(())],
        ),
        out_shape=jax.ShapeDtypeStruct((t, d), F32),
        compiler_params=pltpu.CompilerParams(
            dimension_semantics=("arbitrary",), vmem_limit_bytes=VMEM_LIMIT),
        name="moe_combine",
    )(dest1, dest2, y, h2d, meta, fg)


def _routing_plan(meta_t, counts, *, tm, n_tiles):
    e1 = meta_t[0].astype(jnp.int32)
    e2 = meta_t[1].astype(jnp.int32)
    cnt = counts[0, :N_EXPERTS].astype(jnp.int32)
    padded = (cnt + tm - 1) // tm * tm
    ends = jnp.cumsum(padded)
    offs = ends - padded
    dest1 = offs[e1] + meta_t[4].astype(jnp.int32)
    dest2 = offs[e2] + meta_t[5].astype(jnp.int32)
    start = jnp.arange(n_tiles, dtype=jnp.int32) * tm
    tile_valid = (start < ends[-1]).astype(jnp.int32)
    last = jnp.sum((ends < ends[-1]).astype(jnp.int32))
    passed = jnp.sum((ends[None, :] <= start[:, None]).astype(jnp.int32), axis=1)
    src_rows = _source_rows(dest1, dest2, n_tiles * tm)
    return dest1, dest2, src_rows, jnp.minimum(passed, last), tile_valid


def _source_rows_kernel(d1_ref, d2_ref, src_ref):
    def clear(i, _):
        src_ref[i] = 0
        return 0

    def place(t, _):
        src_ref[d1_ref[t]] = t
        src_ref[d2_ref[t]] = t
        return 0

    lax.fori_loop(0, src_ref.shape[0], clear, 0, unroll=16)
    lax.fori_loop(0, d1_ref.shape[0], place, 0, unroll=8)


def _source_rows(dest1, dest2, n_rows):
    return pl.pallas_call(
        _source_rows_kernel,
        in_specs=[pl.BlockSpec(memory_space=pltpu.SMEM), pl.BlockSpec(memory_space=pltpu.SMEM)],
        out_specs=pl.BlockSpec(memory_space=pltpu.SMEM),
        out_shape=jax.ShapeDtypeStruct((n_rows,), jnp.int32),
        name="moe_source_rows",
    )(dest1, dest2)


def _pad_heads(w, head_dim):
    k = w.shape[0]
    w = w.reshape(k, MLA_HEADS, head_dim)
    w = jnp.pad(w, ((0, 0), (0, 0), (0, LANES - head_dim)))
    return w.reshape(k, MLA_HEADS * LANES)


def _pick(n, cap):
    t = min(n, cap)
    while n % t:
        t //= 2
    return t


def kernel(x, positions, mix_norm, ffn_norm, final_norm, mla_w_in, mla_q_norm, mla_w_q_up,
           mla_kv_norm, mla_w_kv_up, mla_w_o, hgrn_w_in, hgrn_lower_bounds, hgrn_out_norm,
           hgrn_w_o, ffn_w_gate, ffn_w_up, ffn_w_down, moe_router, moe_w_gate, moe_w_up,
           moe_w_down):
    batch, seq, d = x.shape
    t = batch * seq
    x2d = x.reshape(t, d)
    row = lambda v: v.reshape(1, -1)

    w_in = mla_w_in[0]
    n_lat = MLA_Q_LORA + MLA_KV_LORA
    w_in_p = w_in[:, :n_lat].astype(BF16)
    wkr_t = jnp.pad(w_in[:, n_lat:].T,
                    ((MLA_NOPE, LANES - MLA_NOPE - MLA_ROPE), (0, 0))).astype(BF16)
    wq_t = _pad_heads(mla_w_q_up[0], MLA_NOPE + MLA_ROPE).T.astype(BF16)
    w_kv = mla_w_kv_up[0].reshape(MLA_KV_LORA, MLA_HEADS, MLA_NOPE + MLA_V)
    wk_p = _pad_heads(w_kv[:, :, :MLA_NOPE].reshape(MLA_KV_LORA, -1), MLA_NOPE).astype(BF16)
    wv_t = w_kv[:, :, MLA_NOPE:].reshape(MLA_KV_LORA, -1).T.astype(BF16)
    inv_freq = ROPE_THETA ** (-jnp.arange(0, MLA_ROPE, 2, dtype=F32) / MLA_ROPE)
    freq = inv_freq.reshape(MLA_ROPE // 2, 1)
    pos = positions.astype(F32).reshape(1, t)

    q_t, k, v_t = _mla_proj(x2d, pos, row(mix_norm[0]), w_in_p, wkr_t, row(mla_q_norm[0]), wq_t,
                            row(mla_kv_norm[0]), wk_p, wv_t, freq,
                            batch=batch, seq=seq, tm=_pick(seq, 512))
    o = _mla_attn(q_t, k, v_t, tq=_pick(seq, 512), hp=4).reshape(t, MLA_HEADS * MLA_V)
    h = _attn_ffn(x2d, o, mla_w_o[0].astype(BF16), row(ffn_norm[0]),
                  ffn_w_gate[0].astype(BF16), ffn_w_up[0].astype(BF16),
                  ffn_w_down[0].astype(BF16), tm=_pick(t, 512), tf=1792)

    router_p = jnp.pad(moe_router[0], ((0, 0), (0, LANES - N_EXPERTS)))
    h, hn, meta, meta_t, counts = _hgrn(h, row(mix_norm[1]), hgrn_w_in[0].astype(BF16),
                                hgrn_lower_bounds, row(hgrn_out_norm[0]),
                                hgrn_w_o[0].astype(BF16), row(ffn_norm[1]), router_p,
                                batch=batch, seq=seq, tt=_pick(seq, 256), layer=1)
    tm = _pick(t, 512)
    n_tiles = 2 * t // tm + N_EXPERTS
    dest1, dest2, src_rows, tile_expert, tile_valid = _routing_plan(
        meta_t, counts, tm=tm, n_tiles=n_tiles)
    y = _experts(src_rows, tile_expert, tile_valid, hn, moe_w_gate[0].astype(BF16),
                 moe_w_up[0].astype(BF16), moe_w_down[0].astype(BF16), tm=tm, tf=1792)
    out = _combine(dest1, dest2, y, h, meta, row(final_norm), tc=_pick(t, 512))
    return out.reshape(batch, seq, d)
```

```python
import functools

import jax
import jax.numpy as jnp
from jax import lax
from jax.experimental import pallas as pl
from jax.experimental.pallas import tpu as pltpu

EPS = 1e-6
LANES = 128
SUBLANES = 8

MLA_HEADS = 16
MLA_Q_LORA = 384
MLA_KV_LORA = 256
MLA_NOPE = 64
MLA_ROPE = 32
MLA_V = 64
ROPE_THETA = 10000.0

HG_HEADS = 8
HG_DK = 128
HG_CHUNK = 64

N_EXPERTS = 8

VMEM_LIMIT = 56 * 1024 * 1024
DMA_THREADS = 2

BF16 = jnp.bfloat16
F32 = jnp.float32


def _dot(a, b):
    return jnp.dot(a, b, preferred_element_type=F32)


def _dot_nt(a, b):
    return lax.dot_general(a, b, (((1,), (1,)), ((), ())), preferred_element_type=F32)


def _dot_tn(a, b):
    return lax.dot_general(a, b, (((0,), (0,)), ((), ())), preferred_element_type=F32)


def _rms(x, g):
    return x * lax.rsqrt(jnp.mean(x * x, axis=-1, keepdims=True) + EPS) * g


def _silu(x):
    return x * (1.0 / (1.0 + jnp.exp(-x)))


def _store_rows_as_tiles(ref, x):
    n, d = x.shape
    nc = d // LANES
    for c in range(nc):
        ref[pl.ds(c, n, stride=nc), :] = x[:, c * LANES:(c + 1) * LANES]


def _load_rows_from_tiles(ref, nc):
    n = ref.shape[0] // nc
    return jnp.concatenate([ref[pl.ds(c, n, stride=nc), :] for c in range(nc)], axis=1)


def _split3(x):
    hi = x.astype(BF16)
    r = x - hi.astype(F32)
    mid = r.astype(BF16)
    lo = (r - mid.astype(F32)).astype(BF16)
    return hi, mid, lo


def _mla_proj_kernel(x_ref, pos_ref, g_ref, win_ref, wkr_ref, qg_ref, wq_ref, kvg_ref, wk_ref,
                     wv_ref, freq_ref, q_out, k_out, v_out, *, scale):
    half = MLA_ROPE // 2
    x1_rows = slice(MLA_NOPE, MLA_NOPE + half)
    x2_rows = slice(MLA_NOPE + half, MLA_NOPE + MLA_ROPE)
    hn = _rms(x_ref[...], g_ref[...]).astype(BF16)
    proj = _dot(hn, win_ref[...])
    cqn = _rms(proj[:, :MLA_Q_LORA], qg_ref[...]).astype(BF16)
    ckvn = _rms(proj[:, MLA_Q_LORA:], kvg_ref[...]).astype(BF16)

    ang = freq_ref[...] * pos_ref[...]
    cos = jnp.cos(ang)
    sin = jnp.sin(ang)

    def rope_t(slot, mul):
        x1 = slot[x1_rows]
        x2 = slot[x2_rows]
        return jnp.concatenate([slot[:MLA_NOPE] * mul, (x1 * cos - x2 * sin) * mul,
                                (x2 * cos + x1 * sin) * mul, slot[MLA_NOPE + MLA_ROPE:]], axis=0)

    kr_t = _dot_nt(wkr_ref[...], hn)
    kr_roped = rope_t(kr_t, 1.0).T
    q_t = _dot_nt(wq_ref[...], cqn)
    k_pad = _dot(ckvn, wk_ref[...])
    for h in range(MLA_HEADS):
        sl = slice(h * LANES, (h + 1) * LANES)
        q_out[h] = rope_t(q_t[sl], scale).astype(BF16)
        k_out[h] = (k_pad[:, sl] + kr_roped).astype(BF16)
    v_out[...] = _dot_nt(wv_ref[...], ckvn).astype(BF16)


def _mla_proj(x2d, pos, g, w_in_p, wkr_t, qg, wq_t, kvg, wk_p, wv_t, freq, *, batch, seq, tm):
    t = x2d.shape[0]
    d = x2d.shape[1]
    nt = seq // tm
    const = lambda shape: pl.BlockSpec(shape, lambda i: (0,) * len(shape))
    scale = (MLA_NOPE + MLA_ROPE) ** -0.5 * 1.4426950408889634
    return pl.pallas_call(
        functools.partial(_mla_proj_kernel, scale=scale),
        grid=(t // tm,),
        in_specs=[
            pl.BlockSpec((tm, d), lambda i: (i, 0)),
            pl.BlockSpec((1, tm), lambda i: (0, i)),
            const((1, d)),
            const(w_in_p.shape),
            const(wkr_t.shape),
            const((1, MLA_Q_LORA)),
            const(wq_t.shape),
            const((1, MLA_KV_LORA)),
            const(wk_p.shape),
            const(wv_t.shape),
            const(freq.shape),
        ],
        out_specs=[
            pl.BlockSpec((None, MLA_HEADS, LANES, tm), lambda i: (i // nt, 0, 0, i % nt)),
            pl.BlockSpec((None, MLA_HEADS, tm, LANES), lambda i: (i // nt, 0, i % nt, 0)),
            pl.BlockSpec((None, MLA_HEADS * MLA_V, tm), lambda i: (i // nt, 0, i % nt)),
        ],
        out_shape=[
            jax.ShapeDtypeStruct((batch, MLA_HEADS, LANES, seq), BF16),
            jax.ShapeDtypeStruct((batch, MLA_HEADS, seq, LANES), BF16),
            jax.ShapeDtypeStruct((batch, MLA_HEADS * MLA_V, seq), BF16),
        ],
        compiler_params=pltpu.CompilerParams(
            dimension_semantics=("parallel",), vmem_limit_bytes=VMEM_LIMIT),
        name="mla_proj",
    )(x2d, pos, g, w_in_p, wkr_t, qg, wq_t, kvg, wk_p, wv_t, freq)


def _attn_kernel(q_ref, k_ref, v_ref, o_ref, acc_ref, m_ref, vx_ref, sa_ref, sb_ref):
    hp, _, tq = q_ref.shape
    seq = k_ref.shape[1]
    qi = pl.program_id(2)

    @pl.when(qi == 0)
    def _():
        for h in range(hp):
            vx_ref[h, :MLA_V] = v_ref[h * MLA_V:(h + 1) * MLA_V]
            vx_ref[h, MLA_V:] = jnp.ones((vx_ref.shape[1] - MLA_V, seq), BF16)

    causal = (lax.broadcasted_iota(jnp.int32, (tq, tq), 0)
              <= lax.broadcasted_iota(jnp.int32, (tq, tq), 1))

    def scores(dst, blk):
        k0 = pl.multiple_of(blk * tq, tq)
        for h in range(hp):
            dst[h] = _dot(k_ref[h, pl.ds(k0, tq), :], q_ref[h])

    def softmax_pv(src, blk, masked):
        k0 = pl.multiple_of(blk * tq, tq)
        for h in range(hp):
            s = src[h]
            if masked:
                s = jnp.where(causal, s, -jnp.inf)
            m = m_ref[h]
            m_new = jnp.maximum(m, jnp.max(s, axis=0, keepdims=True))
            p = jnp.exp2(s - m_new)
            alpha = jnp.exp2(m - m_new)
            m_ref[h] = m_new
            acc_ref[h] = alpha * acc_ref[h] + _dot(vx_ref[h, :, pl.ds(k0, tq)], p.astype(BF16))

    acc_ref[...] = jnp.zeros_like(acc_ref)
    m_ref[...] = jnp.full_like(m_ref, -jnp.inf)

    scores(sa_ref, 0)

    def pair(jp, _):
        scores(sb_ref, 2 * jp + 1)
        softmax_pv(sa_ref, 2 * jp, False)
        scores(sa_ref, 2 * jp + 2)
        softmax_pv(sb_ref, 2 * jp + 1, False)
        return 0

    lax.fori_loop(0, qi // 2, pair, 0)

    @pl.when(qi % 2 == 1)
    def _():
        scores(sb_ref, qi)
        softmax_pv(sa_ref, qi - 1, False)
        softmax_pv(sb_ref, qi, True)

    @pl.when(qi % 2 == 0)
    def _():
        softmax_pv(sa_ref, qi, True)

    out = jnp.concatenate([acc_ref[h, :MLA_V] / acc_ref[h, MLA_V:MLA_V + 1]
                           for h in range(hp)], axis=0)
    o_ref[...] = out.T.astype(BF16)


def _mla_attn(q_t, k, v_t, *, tq, hp):
    batch, heads, seq, _ = k.shape
    return pl.pallas_call(
        _attn_kernel,
        grid=(batch, heads // hp, seq // tq),
        in_specs=[
            pl.BlockSpec((None, hp, LANES, tq), lambda b, g, i: (b, g, 0, i)),
            pl.BlockSpec((None, hp, seq, LANES), lambda b, g, i: (b, g, 0, 0)),
            pl.BlockSpec((None, hp * MLA_V, seq), lambda b, g, i: (b, g, 0)),
        ],
        out_specs=pl.BlockSpec((None, tq, hp * MLA_V), lambda b, g, i: (b, i, g)),
        out_shape=jax.ShapeDtypeStruct((batch, seq, heads * MLA_V), BF16),
        scratch_shapes=[pltpu.VMEM((hp, MLA_V + 16, tq), F32),
                        pltpu.VMEM((hp, 1, tq), F32), pltpu.VMEM((hp, MLA_V + 16, seq), BF16),
                        pltpu.VMEM((hp, tq, tq), F32), pltpu.VMEM((hp, tq, tq), F32)],
        compiler_params=pltpu.CompilerParams(
            dimension_semantics=("parallel", "parallel", "arbitrary"),
            vmem_limit_bytes=VMEM_LIMIT),
        name="mla_attn",
    )(q_t, k, v_t)


def _attn_ffn_kernel(x_ref, o_ref, wo_ref, g_ref, wg_ref, wu_ref, wd_ref, out_ref, hn_ref):
    j = pl.program_id(1)

    @pl.when(j == 0)
    def _():
        h1 = x_ref[...] + _dot(o_ref[...], wo_ref[...])
        out_ref[...] = h1
        hn_ref[...] = _rms(h1, g_ref[...]).astype(BF16)

    hn = hn_ref[...]
    a = _silu(_dot(hn, wg_ref[...])) * _dot(hn, wu_ref[...])
    out_ref[...] += _dot(a.astype(BF16), wd_ref[...])


def _attn_ffn(x2d, o, w_o, g, w_gate, w_up, w_down, *, tm, tf):
    t, d = x2d.shape
    f = w_gate.shape[1]
    return pl.pallas_call(
        _attn_ffn_kernel,
        grid=(t // tm, f // tf),
        in_specs=[
            pl.BlockSpec((tm, d), lambda i, j: (i, 0)),
            pl.BlockSpec((tm, o.shape[1]), lambda i, j: (i, 0)),
            pl.BlockSpec(w_o.shape, lambda i, j: (0, 0)),
            pl.BlockSpec((1, d), lambda i, j: (0, 0)),
            pl.BlockSpec((d, tf), lambda i, j: (0, j)),
            pl.BlockSpec((d, tf), lambda i, j: (0, j)),
            pl.BlockSpec((tf, d), lambda i, j: (j, 0)),
        ],
        out_specs=pl.BlockSpec((tm, d), lambda i, j: (i, 0)),
        out_shape=jax.ShapeDtypeStruct((t, d), F32),
        scratch_shapes=[pltpu.VMEM((tm, d), BF16)],
        compiler_params=pltpu.CompilerParams(
            dimension_semantics=("parallel", "arbitrary"), vmem_limit_bytes=VMEM_LIMIT),
        name="attn_ffn",
    )(x2d, o, w_o, g, w_gate, w_up, w_down)


def _hgrn_kernel(h_ref, g_ref, win_ref, lbraw_ref, og_ref, wo_ref, fg_ref, router_ref,
                 h_out, hn_out, meta_out, meta_t_out, cnt_out, st_ref, o_scr, cnt_ref, *, layer):
    tt = h_ref.shape[0]
    width = HG_HEADS * HG_DK
    nchunk = tt // HG_CHUNK

    @pl.when(pl.program_id(1) == 0)
    def _():
        st_ref[...] = jnp.zeros_like(st_ref)

    lbraw = lbraw_ref[...]
    e = jnp.exp(lbraw - jnp.max(lbraw, axis=0, keepdims=True))
    sm = e / jnp.sum(e, axis=0, keepdims=True)
    lb = jnp.sum(sm[1:layer + 1], axis=0, keepdims=True)

    h_in = h_ref[...]
    hn = _rms(h_in, g_ref[...]).astype(BF16)
    proj = _dot(hn, win_ref[...])
    q_all = proj[:, :width] * (HG_DK ** -0.5)
    f_all = lb + (1.0 - lb) * (1.0 / (1.0 + jnp.exp(-proj[:, width:2 * width])))
    logf = jnp.log(f_all)
    k_all = 1.0 - f_all

    r = lax.broadcasted_iota(jnp.int32, (tt, tt), 0)
    c = lax.broadcasted_iota(jnp.int32, (tt, tt), 1)
    shift = HG_CHUNK.bit_length() - 1
    tril = (c <= r) & ((r >> shift) == (c >> shift))
    in_chunk = lax.broadcasted_iota(jnp.int32, (tt, 1), 0) & (HG_CHUNK - 1)
    b_all = logf
    for step in (1 << i for i in range(shift)):
        b_all = b_all + jnp.where(in_chunk >= step, pltpu.roll(b_all, step, 0), 0.0)

    for h in range(HG_HEADS):
        sl = slice(h * HG_DK, (h + 1) * HG_DK)
        q = q_all[:, sl]
        k = k_all[:, sl]
        b = b_all[:, sl]
        v = proj[:, 2 * width + h * HG_DK:2 * width + (h + 1) * HG_DK]
        gt = proj[:, 3 * width + h * HG_DK:3 * width + (h + 1) * HG_DK]
        v_b = v.astype(BF16)
        qd = (q * jnp.exp(b)).astype(BF16)
        kd = (k * jnp.exp(-b)).astype(BF16)
        a = jnp.where(tril, _dot_nt(qd, kd), 0.0)
        o = _dot(a.astype(BF16), v_b)

        st = st_ref[h]
        inter = []
        for n in range(nchunk):
            rows = slice(n * HG_CHUNK, (n + 1) * HG_CHUNK)
            b_n = b[rows]
            b_last = b_n[HG_CHUNK - 1:HG_CHUNK]
            inter.append(_dot_nt(qd[rows], st.astype(BF16)))
            kdl = (k[rows] * jnp.exp(b_last - b_n)).astype(BF16)
            st = st * jnp.exp(b_last) + _dot_tn(v_b[rows], kdl)
        st_ref[h] = st
        o = o + jnp.concatenate(inter, axis=0)
        o = _rms(o, og_ref[...]) * _silu(gt)
        o_scr[:, sl] = o.astype(BF16)

    h_new = h_in + _dot(o_scr[...], wo_ref[...])
    h_out[...] = h_new
    hn2 = _rms(h_new, fg_ref[...])
    _store_rows_as_tiles(hn_out, hn2)

    x_hi, x_mid, _ = _split3(hn2)
    router = router_ref[...]
    r_hi = router.astype(BF16)
    r_lo = (router - r_hi.astype(F32)).astype(BF16)
    hi_terms = _dot(x_hi, jnp.concatenate([r_hi, r_lo], axis=1))
    logits = hi_terms[:, :LANES] + _dot(x_mid, r_hi) + hi_terms[:, LANES:]
    lane = lax.broadcasted_iota(jnp.int32, logits.shape, 1)
    neg = -jnp.inf
    logits = jnp.where(lane < N_EXPERTS, logits, neg)
    m1 = jnp.max(logits, axis=-1, keepdims=True)
    i1 = jnp.min(jnp.where(logits == m1, lane, LANES), axis=-1, keepdims=True)
    rest = jnp.where(lane == i1, neg, logits)
    m2 = jnp.max(rest, axis=-1, keepdims=True)
    i2 = jnp.min(jnp.where(rest == m2, lane, LANES), axis=-1, keepdims=True)
    e2 = jnp.exp(m2 - m1)
    w1 = 1.0 / (1.0 + e2)
    w2 = e2 / (1.0 + e2)

    @pl.when((pl.program_id(0) == 0) & (pl.program_id(1) == 0))
    def _():
        cnt_ref[...] = jnp.zeros_like(cnt_ref)

    onehot = jnp.where((lane == i1) | (lane == i2), 1.0, 0.0)
    earlier = jnp.where(c < r, 1.0, 0.0).astype(BF16)
    rank = _dot(earlier, onehot.astype(BF16)) + cnt_ref[...]
    cnt = cnt_ref[...] + jnp.sum(onehot, axis=0, keepdims=True)
    cnt_ref[...] = cnt
    cnt_out[...] = jnp.broadcast_to(cnt, cnt_out.shape)
    rank1 = jnp.sum(jnp.where(lane == i1, rank, 0.0), axis=-1, keepdims=True)
    rank2 = jnp.sum(jnp.where(lane == i2, rank, 0.0), axis=-1, keepdims=True)
    fields = (i1.astype(F32), i2.astype(F32), w1, w2, rank1, rank2)
    meta = jnp.zeros(logits.shape, F32)
    for idx, val in enumerate(fields):
        meta = jnp.where(lane == idx, val, meta)
    meta_out[...] = meta
    meta_t_out[...] = meta.T[:SUBLANES]


def _hgrn(h2d, g, w_in, lb_raw, og, w_o, fg, router_p, *, batch, seq, tt, layer):
    t, d = h2d.shape
    nt = seq // tt
    width = HG_HEADS * HG_DK
    tok = lambda cols: pl.BlockSpec((tt, cols), lambda b, i: (b * nt + i, 0))
    const = lambda shape: pl.BlockSpec(shape, lambda b, i: (0,) * len(shape))
    return pl.pallas_call(
        functools.partial(_hgrn_kernel, layer=layer),
        grid=(batch, nt),
        in_specs=[tok(d), const((1, d)), const(w_in.shape), const(lb_raw.shape),
                  const((1, HG_DK)), const(w_o.shape), const((1, d)), const(router_p.shape)],
        out_specs=[tok(d),
                   pl.BlockSpec((tt * (d // LANES), LANES), lambda b, i: (b * nt + i, 0)),
                   tok(LANES), pl.BlockSpec((SUBLANES, tt), lambda b, i: (0, b * nt + i)),
                   const((SUBLANES, LANES))],
        out_shape=[jax.ShapeDtypeStruct((t, d), F32),
                   jax.ShapeDtypeStruct((t * (d // LANES), LANES), F32),
                   jax.ShapeDtypeStruct((t, LANES), F32),
                   jax.ShapeDtypeStruct((SUBLANES, t), F32),
                   jax.ShapeDtypeStruct((SUBLANES, LANES), F32)],
        scratch_shapes=[pltpu.VMEM((HG_HEADS, HG_DK, HG_DK), F32),
                        pltpu.VMEM((tt, width), BF16),
                        pltpu.VMEM((1, LANES), F32)],
        compiler_params=pltpu.CompilerParams(
            dimension_semantics=("arbitrary", "arbitrary"), vmem_limit_bytes=VMEM_LIMIT),
        name="hgrn",
    )(h2d, g, w_in, lb_raw, og, w_o, fg, router_p)


def _expert_kernel(src_ref, te_ref, tv_ref, x_hbm, wg_ref, wu_ref, wd_ref, y_ref,
                   xbuf, xb_ref, acc_ref, sems):
    del te_ref
    r = pl.program_id(0)
    j = pl.program_id(1)
    n_tiles = pl.num_programs(0)
    nj = pl.num_programs(1)
    tm = xb_ref.shape[0]
    nc = xb_ref.shape[1] // LANES
    valid = tv_ref[r] != 0
    has_next = r + 1 < n_tiles

    def row_copy(tile, row):
        slot = tile % 2
        src = pl.multiple_of(src_ref[tile * tm + row] * nc, nc)
        return pltpu.make_async_copy(x_hbm.at[pl.ds(src, nc)],
                                     xbuf.at[slot, pl.ds(row * nc, nc)], sems.at[slot])

    def swiglu(x):
        a = _silu(_dot(x, wg_ref[...])) * _dot(x, wu_ref[...])
        return _dot(a.astype(BF16), wd_ref[...])

    def swiglu_step(first):
        if not first:
            acc_ref[...] += swiglu(xb_ref[...])
            return
        half = tm // 2
        for hlf in range(2):
            rows = slice(hlf * half, (hlf + 1) * half)
            x = _load_rows_from_tiles(xbuf.at[r % 2, pl.ds(hlf * half * nc, half * nc)], nc)
            x = x.astype(BF16)
            xb_ref[rows] = x
            acc_ref[rows] = swiglu(x)

    @pl.when((r == 0) & (j == 0))
    def _():
        lax.fori_loop(0, tm, lambda i, c: (row_copy(r, i).start(), c)[1], 0)

    @pl.when((j == 0) & ((r == 0) | (tv_ref[jnp.maximum(r - 1, 0)] != 0)))
    def _():
        pltpu.make_async_copy(x_hbm.at[pl.ds(0, tm * nc)], xbuf.at[r % 2],
                              sems.at[r % 2]).wait()

    @pl.when(valid & has_next & (j == 0))
    def _():
        for i in range(tm):
            row_copy(r + 1, i).start(priority=i % DMA_THREADS)
        swiglu_step(True)

    @pl.when(valid & jnp.logical_not(has_next) & (j == 0))
    def _():
        swiglu_step(True)

    @pl.when(valid & (j > 0))
    def _():
        swiglu_step(False)

    @pl.when(valid & (j == nj - 1))
    def _():
        _store_rows_as_tiles(y_ref, acc_ref[...])

    @pl.when(jnp.logical_not(valid) & (j == nj - 1))
    def _():
        y_ref[...] = jnp.zeros_like(y_ref)


def _experts(src_rows, tile_expert, tile_valid, x, w_gate, w_up, w_down, *, tm, tf):
    n_rows = src_rows.shape[0]
    d = w_gate.shape[1]
    nc = d // LANES
    f = w_gate.shape[2]
    nj = f // tf
    col = lambda r, j, tv: jnp.where(tv[r] != 0, j, nj - 1)
    return pl.pallas_call(
        _expert_kernel,
        grid_spec=pltpu.PrefetchScalarGridSpec(
            num_scalar_prefetch=3,
            grid=(n_rows // tm, nj),
            in_specs=[
                pl.BlockSpec(memory_space=pl.ANY),
                pl.BlockSpec((None, d, tf), lambda r, j, s, te, tv: (te[r], 0, col(r, j, tv))),
                pl.BlockSpec((None, d, tf), lambda r, j, s, te, tv: (te[r], 0, col(r, j, tv))),
                pl.BlockSpec((None, tf, d), lambda r, j, s, te, tv: (te[r], col(r, j, tv), 0)),
            ],
            out_specs=pl.BlockSpec((tm * nc, LANES), lambda r, j, s, te, tv: (r, 0)),
            scratch_shapes=[pltpu.VMEM((2, tm * nc, LANES), F32), pltpu.VMEM((tm, d), BF16),
                            pltpu.VMEM((tm, d), F32), pltpu.SemaphoreType.DMA((2,))],
        ),
        out_shape=jax.ShapeDtypeStruct((n_rows * nc, LANES), F32),
        compiler_params=pltpu.CompilerParams(
            dimension_semantics=("arbitrary", "arbitrary"), vmem_limit_bytes=VMEM_LIMIT),
        name="moe_experts",
    )(src_rows, tile_expert, tile_valid, x, w_gate, w_up, w_down)


def _combine_kernel(d1_ref, d2_ref, y_hbm, h_ref, meta_ref, fg_ref, out_ref, a_ref, b_ref, sem):
    tc = h_ref.shape[0]
    base = pl.program_id(0) * tc

    nc = h_ref.shape[1] // LANES

    def issue(r, _):
        for thread, (d_ref, buf) in enumerate(((d1_ref, a_ref), (d2_ref, b_ref))):
            src = pl.multiple_of(d_ref[base + r] * nc, nc)
            pltpu.make_async_copy(y_hbm.at[pl.ds(src, nc)], buf.at[pl.ds(r * nc, nc)],
                                  sem).start(priority=thread % DMA_THREADS)
        return 0

    lax.fori_loop(0, tc, issue, 0, unroll=16)
    pltpu.make_async_copy(y_hbm.at[pl.ds(0, tc * nc)], a_ref, sem).wait()
    pltpu.make_async_copy(y_hbm.at[pl.ds(0, tc * nc)], b_ref, sem).wait()
    meta = meta_ref[...]
    w1 = meta[:, 2:3]
    w2 = meta[:, 3:4]
    moe = w1 * _load_rows_from_tiles(a_ref, nc) + w2 * _load_rows_from_tiles(b_ref, nc)
    out_ref[...] = _rms(h_ref[...] + moe, fg_ref[...])


def _combine(dest1, dest2, y, h2d, meta, fg, *, tc):
    t, d = h2d.shape
    return pl.pallas_call(
        _combine_kernel,
        grid_spec=pltpu.PrefetchScalarGridSpec(
            num_scalar_prefetch=2,
            grid=(t // tc,),
            in_specs=[
                pl.BlockSpec(memory_space=pl.ANY),
                pl.BlockSpec((tc, d), lambda i, d1, d2: (i, 0)),
                pl.BlockSpec((tc, LANES), lambda i, d1, d2: (i, 0)),
                pl.BlockSpec((1, d), lambda i, d1, d2: (0, 0)),
            ],
            out_specs=pl.BlockSpec((tc, d), lambda i, d1, d2: (i, 0)),
            scratch_shapes=[pltpu.VMEM((tc * d // LANES, LANES), F32),
                            pltpu.VMEM((tc * d // LANES, LANES), F32),
                            pltpu.SemaphoreType.DMA(())],
        ),
        out_shape=jax.ShapeDtypeStruct((t, d), F32),
        compiler_params=pltpu.CompilerParams(
            dimension_semantics=("arbitrary",), vmem_limit_bytes=VMEM_LIMIT),
        name="moe_combine",
    )(dest1, dest2, y, h2d, meta, fg)


def _routing_plan(meta_t, counts, *, tm, n_tiles):
    e1 = meta_t[0].astype(jnp.int32)
    e2 = meta_t[1].astype(jnp.int32)
    cnt = counts[0, :N_EXPERTS].astype(jnp.int32)
    padded = (cnt + tm - 1) // tm * tm
    ends = jnp.cumsum(padded)
    offs = ends - padded
    dest1 = offs[e1] + meta_t[4].astype(jnp.int32)
    dest2 = offs[e2] + meta_t[5].astype(jnp.int32)
    start = jnp.arange(n_tiles, dtype=jnp.int32) * tm
    tile_valid = (start < ends[-1]).astype(jnp.int32)
    last = jnp.sum((ends < ends[-1]).astype(jnp.int32))
    passed = jnp.sum((ends[None, :] <= start[:, None]).astype(jnp.int32), axis=1)
    src_rows = _source_rows(dest1, dest2, n_tiles * tm)
    return dest1, dest2, src_rows, jnp.minimum(passed, last), tile_valid


def _source_rows_kernel(d1_ref, d2_ref, src_ref):
    def clear(i, _):
        src_ref[i] = 0
        return 0

    def place(t, _):
        src_ref[d1_ref[t]] = t
        src_ref[d2_ref[t]] = t
        return 0

    lax.fori_loop(0, src_ref.shape[0], clear, 0, unroll=16)
    lax.fori_loop(0, d1_ref.shape[0], place, 0, unroll=8)


def _source_rows(dest1, dest2, n_rows):
    return pl.pallas_call(
        _source_rows_kernel,
        in_specs=[pl.BlockSpec(memory_space=pltpu.SMEM), pl.BlockSpec(memory_space=pltpu.SMEM)],
        out_specs=pl.BlockSpec(memory_space=pltpu.SMEM),
        out_shape=jax.ShapeDtypeStruct((n_rows,), jnp.int32),
        name="moe_source_rows",
    )(dest1, dest2)


def _pad_heads(w, head_dim):
    k = w.shape[0]
    w = w.reshape(k, MLA_HEADS, head_dim)
    w = jnp.pad(w, ((0, 0), (0, 0), (0, LANES - head_dim)))
    return w.reshape(k, MLA_HEADS * LANES)


def _pick(n, cap):
    t = min(n, cap)
    while n % t:
        t //= 2
    return t


def kernel(x, positions, mix_norm, ffn_norm, final_norm, mla_w_in, mla_q_norm, mla_w_q_up,
           mla_kv_norm, mla_w_kv_up, mla_w_o, hgrn_w_in, hgrn_lower_bounds, hgrn_out_norm,
           hgrn_w_o, ffn_w_gate, ffn_w_up, ffn_w_down, moe_router, moe_w_gate, moe_w_up,
           moe_w_down):
    batch, seq, d = x.shape
    t = batch * seq
    x2d = x.reshape(t, d)
    row = lambda v: v.reshape(1, -1)

    w_in = mla_w_in[0]
    n_lat = MLA_Q_LORA + MLA_KV_LORA
    w_in_p = w_in[:, :n_lat].astype(BF16)
    wkr_t = jnp.pad(w_in[:, n_lat:].T,
                    ((MLA_NOPE, LANES - MLA_NOPE - MLA_ROPE), (0, 0))).astype(BF16)
    wq_t = _pad_heads(mla_w_q_up[0], MLA_NOPE + MLA_ROPE).T.astype(BF16)
    w_kv = mla_w_kv_up[0].reshape(MLA_KV_LORA, MLA_HEADS, MLA_NOPE + MLA_V)
    wk_p = _pad_heads(w_kv[:, :, :MLA_NOPE].reshape(MLA_KV_LORA, -1), MLA_NOPE).astype(BF16)
    wv_t = w_kv[:, :, MLA_NOPE:].reshape(MLA_KV_LORA, -1).T.astype(BF16)
    inv_freq = ROPE_THETA ** (-jnp.arange(0, MLA_ROPE, 2, dtype=F32) / MLA_ROPE)
    freq = inv_freq.reshape(MLA_ROPE // 2, 1)
    pos = positions.astype(F32).reshape(1, t)

    q_t, k, v_t = _mla_proj(x2d, pos, row(mix_norm[0]), w_in_p, wkr_t, row(mla_q_norm[0]), wq_t,
                            row(mla_kv_norm[0]), wk_p, wv_t, freq,
                            batch=batch, seq=seq, tm=_pick(seq, 512))
    o = _mla_attn(q_t, k, v_t, tq=_pick(seq, 512), hp=8).reshape(t, MLA_HEADS * MLA_V)
    h = _attn_ffn(x2d, o, mla_w_o[0].astype(BF16), row(ffn_norm[0]),
                  ffn_w_gate[0].astype(BF16), ffn_w_up[0].astype(BF16),
                  ffn_w_down[0].astype(BF16), tm=_pick(t, 512), tf=1792)

    router_p = jnp.pad(moe_router[0], ((0, 0), (0, LANES - N_EXPERTS)))
    h, hn, meta, meta_t, counts = _hgrn(h, row(mix_norm[1]), hgrn_w_in[0].astype(BF16),
                                hgrn_lower_bounds, row(hgrn_out_norm[0]),
                                hgrn_w_o[0].astype(BF16), row(ffn_norm[1]), router_p,
                                batch=batch, seq=seq, tt=_pick(seq, 256), layer=1)
    tm = _pick(t, 512)
    n_tiles = 2 * t // tm + N_EXPERTS
    dest1, dest2, src_rows, tile_expert, tile_valid = _routing_plan(
        meta_t, counts, tm=tm, n_tiles=n_tiles)
    y = _experts(src_rows, tile_expert, tile_valid, hn, moe_w_gate[0].astype(BF16),
                 moe_w_up[0].astype(BF16), moe_w_down[0].astype(BF16), tm=tm, tf=1792)
    out = _combine(dest1, dest2, y, h, meta, row(final_norm), tc=_pick(t, 512))
    return out.reshape(batch, seq, d)
```

```python
import functools

import jax
import jax.numpy as jnp
from jax import lax
from jax.experimental import pallas as pl
from jax.experimental.pallas import tpu as pltpu

EPS = 1e-6
LANES = 128
SUBLANES = 8

MLA_HEADS = 16
MLA_Q_LORA = 384
MLA_KV_LORA = 256
MLA_NOPE = 64
MLA_ROPE = 32
MLA_V = 64
ROPE_THETA = 10000.0

HG_HEADS = 8
HG_DK = 128
HG_CHUNK = 64

N_EXPERTS = 8

VMEM_LIMIT = 56 * 1024 * 1024
DMA_THREADS = 2

BF16 = jnp.bfloat16
F32 = jnp.float32


def _dot(a, b):
    return jnp.dot(a, b, preferred_element_type=F32)


def _dot_nt(a, b):
    return lax.dot_general(a, b, (((1,), (1,)), ((), ())), preferred_element_type=F32)


def _dot_tn(a, b):
    return lax.dot_general(a, b, (((0,), (0,)), ((), ())), preferred_element_type=F32)


def _rms(x, g):
    return x * lax.rsqrt(jnp.mean(x * x, axis=-1, keepdims=True) + EPS) * g


def _silu(x):
    return x * (1.0 / (1.0 + jnp.exp(-x)))


def _store_rows_as_tiles(ref, x):
    n, d = x.shape
    nc = d // LANES
    for c in range(nc):
        ref[pl.ds(c, n, stride=nc), :] = x[:, c * LANES:(c + 1) * LANES]


def _load_rows_from_tiles(ref, nc):
    n = ref.shape[0] // nc
    return jnp.concatenate([ref[pl.ds(c, n, stride=nc), :] for c in range(nc)], axis=1)


def _split3(x):
    hi = x.astype(BF16)
    r = x - hi.astype(F32)
    mid = r.astype(BF16)
    lo = (r - mid.astype(F32)).astype(BF16)
    return hi, mid, lo


def _mla_proj_kernel(x_ref, pos_ref, g_ref, win_ref, wkr_ref, qg_ref, wq_ref, kvg_ref, wk_ref,
                     wv_ref, freq_ref, q_out, k_out, v_out, *, scale):
    half = MLA_ROPE // 2
    x1_rows = slice(MLA_NOPE, MLA_NOPE + half)
    x2_rows = slice(MLA_NOPE + half, MLA_NOPE + MLA_ROPE)
    hn = _rms(x_ref[...], g_ref[...]).astype(BF16)
    proj = _dot(hn, win_ref[...])
    cqn = _rms(proj[:, :MLA_Q_LORA], qg_ref[...]).astype(BF16)
    ckvn = _rms(proj[:, MLA_Q_LORA:], kvg_ref[...]).astype(BF16)

    ang = freq_ref[...] * pos_ref[...]
    cos = jnp.cos(ang)
    sin = jnp.sin(ang)

    def rope_t(slot, mul):
        x1 = slot[x1_rows]
        x2 = slot[x2_rows]
        return jnp.concatenate([slot[:MLA_NOPE] * mul, (x1 * cos - x2 * sin) * mul,
                                (x2 * cos + x1 * sin) * mul, slot[MLA_NOPE + MLA_ROPE:]], axis=0)

    kr_t = _dot_nt(wkr_ref[...], hn)
    kr_roped = rope_t(kr_t, 1.0).T
    q_t = _dot_nt(wq_ref[...], cqn)
    k_pad = _dot(ckvn, wk_ref[...])
    for h in range(MLA_HEADS):
        sl = slice(h * LANES, (h + 1) * LANES)
        q_out[h] = rope_t(q_t[sl], scale).astype(BF16)
        k_out[h] = (k_pad[:, sl] + kr_roped).astype(BF16)
    v_out[...] = _dot_nt(wv_ref[...], ckvn).astype(BF16)


def _mla_proj(x2d, pos, g, w_in_p, wkr_t, qg, wq_t, kvg, wk_p, wv_t, freq, *, batch, seq, tm):
    t = x2d.shape[0]
    d = x2d.shape[1]
    nt = seq // tm
    const = lambda shape: pl.BlockSpec(shape, lambda i: (0,) * len(shape))
    scale = (MLA_NOPE + MLA_ROPE) ** -0.5 * 1.4426950408889634
    return pl.pallas_call(
        functools.partial(_mla_proj_kernel, scale=scale),
        grid=(t // tm,),
        in_specs=[
            pl.BlockSpec((tm, d), lambda i: (i, 0)),
            pl.BlockSpec((1, tm), lambda i: (0, i)),
            const((1, d)),
            const(w_in_p.shape),
            const(wkr_t.shape),
            const((1, MLA_Q_LORA)),
            const(wq_t.shape),
            const((1, MLA_KV_LORA)),
            const(wk_p.shape),
            const(wv_t.shape),
            const(freq.shape),
        ],
        out_specs=[
            pl.BlockSpec((None, MLA_HEADS, LANES, tm), lambda i: (i // nt, 0, 0, i % nt)),
            pl.BlockSpec((None, MLA_HEADS, tm, LANES), lambda i: (i // nt, 0, i % nt, 0)),
            pl.BlockSpec((None, MLA_HEADS * MLA_V, tm), lambda i: (i // nt, 0, i % nt)),
        ],
        out_shape=[
            jax.ShapeDtypeStruct((batch, MLA_HEADS, LANES, seq), BF16),
            jax.ShapeDtypeStruct((batch, MLA_HEADS, seq, LANES), BF16),
            jax.ShapeDtypeStruct((batch, MLA_HEADS * MLA_V, seq), BF16),
        ],
        compiler_params=pltpu.CompilerParams(
            dimension_semantics=("parallel",), vmem_limit_bytes=VMEM_LIMIT),
        name="mla_proj",
    )(x2d, pos, g, w_in_p, wkr_t, qg, wq_t, kvg, wk_p, wv_t, freq)


def _attn_kernel(q_ref, k_ref, v_ref, o_ref, acc_ref, m_ref, vx_ref, sa_ref, sb_ref):
    hp, _, tq = q_ref.shape
    seq = k_ref.shape[1]
    qi = pl.program_id(2)

    @pl.when(qi == 0)
    def _():
        for h in range(hp):
            vx_ref[h, :MLA_V] = v_ref[h * MLA_V:(h + 1) * MLA_V]
            vx_ref[h, MLA_V:] = jnp.ones((vx_ref.shape[1] - MLA_V, seq), BF16)

    causal = (lax.broadcasted_iota(jnp.int32, (tq, tq), 0)
              <= lax.broadcasted_iota(jnp.int32, (tq, tq), 1))

    def scores(dst, blk):
        k0 = pl.multiple_of(blk * tq, tq)
        for h in range(hp):
            dst[h] = _dot(k_ref[h, pl.ds(k0, tq), :], q_ref[h])

    def softmax_pv(src, blk, masked):
        k0 = pl.multiple_of(blk * tq, tq)
        for h in range(hp):
            s = src[h]
            if masked:
                s = jnp.where(causal, s, -jnp.inf)
            m = m_ref[h]
            m_new = jnp.maximum(m, jnp.max(s, axis=0, keepdims=True))
            p = jnp.exp2(s - m_new)
            alpha = jnp.exp2(m - m_new)
            m_ref[h] = m_new
            acc_ref[h] = alpha * acc_ref[h] + _dot(vx_ref[h, :, pl.ds(k0, tq)], p.astype(BF16))

    acc_ref[...] = jnp.zeros_like(acc_ref)
    m_ref[...] = jnp.full_like(m_ref, -jnp.inf)

    scores(sa_ref, 0)

    def pair(jp, _):
        scores(sb_ref, 2 * jp + 1)
        softmax_pv(sa_ref, 2 * jp, False)
        scores(sa_ref, 2 * jp + 2)
        softmax_pv(sb_ref, 2 * jp + 1, False)
        return 0

    lax.fori_loop(0, qi // 2, pair, 0)

    @pl.when(qi % 2 == 1)
    def _():
        scores(sb_ref, qi)
        softmax_pv(sa_ref, qi - 1, False)
        softmax_pv(sb_ref, qi, True)

    @pl.when(qi % 2 == 0)
    def _():
        softmax_pv(sa_ref, qi, True)

    out = jnp.concatenate([acc_ref[h, :MLA_V] / acc_ref[h, MLA_V:MLA_V + 1]
                           for h in range(hp)], axis=0)
    o_ref[...] = out.T.astype(BF16)


def _mla_attn(q_t, k, v_t, *, tq, hp):
    batch, heads, seq, _ = k.shape
    return pl.pallas_call(
        _attn_kernel,
        grid=(batch, heads // hp, seq // tq),
        in_specs=[
            pl.BlockSpec((None, hp, LANES, tq), lambda b, g, i: (b, g, 0, i)),
            pl.BlockSpec((None, hp, seq, LANES), lambda b, g, i: (b, g, 0, 0)),
            pl.BlockSpec((None, hp * MLA_V, seq), lambda b, g, i: (b, g, 0)),
        ],
        out_specs=pl.BlockSpec((None, tq, hp * MLA_V), lambda b, g, i: (b, i, g)),
        out_shape=jax.ShapeDtypeStruct((batch, seq, heads * MLA_V), BF16),
        scratch_shapes=[pltpu.VMEM((hp, MLA_V + 16, tq), F32),
                        pltpu.VMEM((hp, 1, tq), F32), pltpu.VMEM((hp, MLA_V + 16, seq), BF16),
                        pltpu.VMEM((hp, tq, tq), F32), pltpu.VMEM((hp, tq, tq), F32)],
        compiler_params=pltpu.CompilerParams(
            dimension_semantics=("parallel", "parallel", "arbitrary"),
            vmem_limit_bytes=VMEM_LIMIT),
        name="mla_attn",
    )(q_t, k, v_t)


def _attn_ffn_kernel(x_ref, o_ref, wo_ref, g_ref, wg_ref, wu_ref, wd_ref, out_ref, hn_ref):
    j = pl.program_id(1)

    @pl.when(j == 0)
    def _():
        h1 = x_ref[...] + _dot(o_ref[...], wo_ref[...])
        out_ref[...] = h1
        hn_ref[...] = _rms(h1, g_ref[...]).astype(BF16)

    hn = hn_ref[...]
    a = _silu(_dot(hn, wg_ref[...])) * _dot(hn, wu_ref[...])
    out_ref[...] += _dot(a.astype(BF16), wd_ref[...])


def _attn_ffn(x2d, o, w_o, g, w_gate, w_up, w_down, *, tm, tf):
    t, d = x2d.shape
    f = w_gate.shape[1]
    return pl.pallas_call(
        _attn_ffn_kernel,
        grid=(t // tm, f // tf),
        in_specs=[
            pl.BlockSpec((tm, d), lambda i, j: (i, 0)),
            pl.BlockSpec((tm, o.shape[1]), lambda i, j: (i, 0)),
            pl.BlockSpec(w_o.shape, lambda i, j: (0, 0)),
            pl.BlockSpec((1, d), lambda i, j: (0, 0)),
            pl.BlockSpec((d, tf), lambda i, j: (0, j)),
            pl.BlockSpec((d, tf), lambda i, j: (0, j)),
            pl.BlockSpec((tf, d), lambda i, j: (j, 0)),
        ],
        out_specs=pl.BlockSpec((tm, d), lambda i, j: (i, 0)),
        out_shape=jax.ShapeDtypeStruct((t, d), F32),
        scratch_shapes=[pltpu.VMEM((tm, d), BF16)],
        compiler_params=pltpu.CompilerParams(
            dimension_semantics=("parallel", "arbitrary"), vmem_limit_bytes=VMEM_LIMIT),
        name="attn_ffn",
    )(x2d, o, w_o, g, w_gate, w_up, w_down)


def _hgrn_kernel(h_ref, g_ref, win_ref, lbraw_ref, og_ref, wo_ref, fg_ref, router_ref,
                 h_out, hn_out, meta_out, meta_t_out, cnt_out, st_ref, o_scr, cnt_ref, *, layer):
    tt = h_ref.shape[0]
    width = HG_HEADS * HG_DK
    nchunk = tt // HG_CHUNK

    @pl.when(pl.program_id(1) == 0)
    def _():
        st_ref[...] = jnp.zeros_like(st_ref)

    lbraw = lbraw_ref[...]
    e = jnp.exp(lbraw - jnp.max(lbraw, axis=0, keepdims=True))
    sm = e / jnp.sum(e, axis=0, keepdims=True)
    lb = jnp.sum(sm[1:layer + 1], axis=0, keepdims=True)

    h_in = h_ref[...]
    hn = _rms(h_in, g_ref[...]).astype(BF16)
    proj = _dot(hn, win_ref[...])
    q_all = proj[:, :width] * (HG_DK ** -0.5)
    f_all = lb + (1.0 - lb) * (1.0 / (1.0 + jnp.exp(-proj[:, width:2 * width])))
    logf = jnp.log(f_all)
    k_all = 1.0 - f_all

    r = lax.broadcasted_iota(jnp.int32, (tt, tt), 0)
    c = lax.broadcasted_iota(jnp.int32, (tt, tt), 1)
    shift = HG_CHUNK.bit_length() - 1
    tril = (c <= r) & ((r >> shift) == (c >> shift))
    in_chunk = lax.broadcasted_iota(jnp.int32, (tt, 1), 0) & (HG_CHUNK - 1)
    b_all = logf
    for step in (1 << i for i in range(shift)):
        b_all = b_all + jnp.where(in_chunk >= step, pltpu.roll(b_all, step, 0), 0.0)

    for h in range(HG_HEADS):
        sl = slice(h * HG_DK, (h + 1) * HG_DK)
        q = q_all[:, sl]
        k = k_all[:, sl]
        b = b_all[:, sl]
        v = proj[:, 2 * width + h * HG_DK:2 * width + (h + 1) * HG_DK]
        gt = proj[:, 3 * width + h * HG_DK:3 * width + (h + 1) * HG_DK]
        v_b = v.astype(BF16)
        qd = (q * jnp.exp(b)).astype(BF16)
        kd = (k * jnp.exp(-b)).astype(BF16)
        a = jnp.where(tril, _dot_nt(qd, kd), 0.0)
        o = _dot(a.astype(BF16), v_b)

        st = st_ref[h]
        inter = []
        for n in range(nchunk):
            rows = slice(n * HG_CHUNK, (n + 1) * HG_CHUNK)
            b_n = b[rows]
            b_last = b_n[HG_CHUNK - 1:HG_CHUNK]
            inter.append(_dot_nt(qd[rows], st.astype(BF16)))
            kdl = (k[rows] * jnp.exp(b_last - b_n)).astype(BF16)
            st = st * jnp.exp(b_last) + _dot_tn(v_b[rows], kdl)
        st_ref[h] = st
        o = o + jnp.concatenate(inter, axis=0)
        o = _rms(o, og_ref[...]) * _silu(gt)
        o_scr[:, sl] = o.astype(BF16)

    h_new = h_in + _dot(o_scr[...], wo_ref[...])
    h_out[...] = h_new
    hn2 = _rms(h_new, fg_ref[...])
    _store_rows_as_tiles(hn_out, hn2)

    x_hi, x_mid, _ = _split3(hn2)
    router = router_ref[...]
    r_hi = router.astype(BF16)
    r_lo = (router - r_hi.astype(F32)).astype(BF16)
    hi_terms = _dot(x_hi, jnp.concatenate([r_hi, r_lo], axis=1))
    logits = hi_terms[:, :LANES] + _dot(x_mid, r_hi) + hi_terms[:, LANES:]
    lane = lax.broadcasted_iota(jnp.int32, logits.shape, 1)
    neg = -jnp.inf
    logits = jnp.where(lane < N_EXPERTS, logits, neg)
    m1 = jnp.max(logits, axis=-1, keepdims=True)
    i1 = jnp.min(jnp.where(logits == m1, lane, LANES), axis=-1, keepdims=True)
    rest = jnp.where(lane == i1, neg, logits)
    m2 = jnp.max(rest, axis=-1, keepdims=True)
    i2 = jnp.min(jnp.where(rest == m2, lane, LANES), axis=-1, keepdims=True)
    e2 = jnp.exp(m2 - m1)
    w1 = 1.0 / (1.0 + e2)
    w2 = e2 / (1.0 + e2)

    @pl.when((pl.program_id(0) == 0) & (pl.program_id(1) == 0))
    def _():
        cnt_ref[...] = jnp.zeros_like(cnt_ref)

    onehot = jnp.where((lane == i1) | (lane == i2), 1.0, 0.0)
    earlier = jnp.where(c < r, 1.0, 0.0).astype(BF16)
    rank = _dot(earlier, onehot.astype(BF16)) + cnt_ref[...]
    cnt = cnt_ref[...] + jnp.sum(onehot, axis=0, keepdims=True)
    cnt_ref[...] = cnt
    cnt_out[...] = jnp.broadcast_to(cnt, cnt_out.shape)
    rank1 = jnp.sum(jnp.where(lane == i1, rank, 0.0), axis=-1, keepdims=True)
    rank2 = jnp.sum(jnp.where(lane == i2, rank, 0.0), axis=-1, keepdims=True)
    fields = (i1.astype(F32), i2.astype(F32), w1, w2, rank1, rank2)
    meta = jnp.zeros(logits.shape, F32)
    for idx, val in enumerate(fields):
        meta = jnp.where(lane == idx, val, meta)
    meta_out[...] = meta
    meta_t_out[...] = meta.T[:SUBLANES]


def _hgrn(h2d, g, w_in, lb_raw, og, w_o, fg, router_p, *, batch, seq, tt, layer):
    t, d = h2d.shape
    nt = seq // tt
    width = HG_HEADS * HG_DK
    tok = lambda cols: pl.BlockSpec((tt, cols), lambda b, i: (b * nt + i, 0))
    const = lambda shape: pl.BlockSpec(shape, lambda b, i: (0,) * len(shape))
    return pl.pallas_call(
        functools.partial(_hgrn_kernel, layer=layer),
        grid=(batch, nt),
        in_specs=[tok(d), const((1, d)), const(w_in.shape), const(lb_raw.shape),
                  const((1, HG_DK)), const(w_o.shape), const((1, d)), const(router_p.shape)],
        out_specs=[tok(d),
                   pl.BlockSpec((tt * (d // LANES), LANES), lambda b, i: (b * nt + i, 0)),
                   tok(LANES), pl.BlockSpec((SUBLANES, tt), lambda b, i: (0, b * nt + i)),
                   const((SUBLANES, LANES))],
        out_shape=[jax.ShapeDtypeStruct((t, d), F32),
                   jax.ShapeDtypeStruct((t * (d // LANES), LANES), F32),
                   jax.ShapeDtypeStruct((t, LANES), F32),
                   jax.ShapeDtypeStruct((SUBLANES, t), F32),
                   jax.ShapeDtypeStruct((SUBLANES, LANES), F32)],
        scratch_shapes=[pltpu.VMEM((HG_HEADS, HG_DK, HG_DK), F32),
                        pltpu.VMEM((tt, width), BF16),
                        pltpu.VMEM((1, LANES), F32)],
        compiler_params=pltpu.CompilerParams(
            dimension_semantics=("arbitrary", "arbitrary"), vmem_limit_bytes=VMEM_LIMIT),
        name="hgrn",
    )(h2d, g, w_in, lb_raw, og, w_o, fg, router_p)


def _expert_kernel(src_ref, te_ref, tv_ref, x_hbm, wg_ref, wu_ref, wd_ref, y_ref,
                   xbuf, xb_ref, acc_ref, sems):
    del te_ref
    r = pl.program_id(0)
    j = pl.program_id(1)
    n_tiles = pl.num_programs(0)
    nj = pl.num_programs(1)
    tm = xb_ref.shape[0]
    nc = xb_ref.shape[1] // LANES
    valid = tv_ref[r] != 0
    has_next = r + 1 < n_tiles

    def row_copy(tile, row):
        slot = tile % 2
        src = pl.multiple_of(src_ref[tile * tm + row] * nc, nc)
        return pltpu.make_async_copy(x_hbm.at[pl.ds(src, nc)],
                                     xbuf.at[slot, pl.ds(row * nc, nc)], sems.at[slot])

    def swiglu(x):
        a = _silu(_dot(x, wg_ref[...])) * _dot(x, wu_ref[...])
        return _dot(a.astype(BF16), wd_ref[...])

    def swiglu_step(first):
        if not first:
            acc_ref[...] += swiglu(xb_ref[...])
            return
        half = tm // 2
        for hlf in range(2):
            rows = slice(hlf * half, (hlf + 1) * half)
            x = _load_rows_from_tiles(xbuf.at[r % 2, pl.ds(hlf * half * nc, half * nc)], nc)
            x = x.astype(BF16)
            xb_ref[rows] = x
            acc_ref[rows] = swiglu(x)

    @pl.when((r == 0) & (j == 0))
    def _():
        lax.fori_loop(0, tm, lambda i, c: (row_copy(r, i).start(), c)[1], 0)

    @pl.when((j == 0) & ((r == 0) | (tv_ref[jnp.maximum(r - 1, 0)] != 0)))
    def _():
        pltpu.make_async_copy(x_hbm.at[pl.ds(0, tm * nc)], xbuf.at[r % 2],
                              sems.at[r % 2]).wait()

    part = tm // nj
    for first in (True, False):
        at_step = (j == 0) if first else (j > 0)

        @pl.when(valid & has_next & at_step)
        def _():
            for i in range(part):
                row_copy(r + 1, j * part + i).start(priority=i % DMA_THREADS)
            swiglu_step(first)

        @pl.when(valid & jnp.logical_not(has_next) & at_step)
        def _():
            swiglu_step(first)

    @pl.when(valid & (j == nj - 1))
    def _():
        _store_rows_as_tiles(y_ref, acc_ref[...])

    @pl.when(jnp.logical_not(valid) & (j == nj - 1))
    def _():
        y_ref[...] = jnp.zeros_like(y_ref)


def _experts(src_rows, tile_expert, tile_valid, x, w_gate, w_up, w_down, *, tm, tf):
    n_rows = src_rows.shape[0]
    d = w_gate.shape[1]
    nc = d // LANES
    f = w_gate.shape[2]
    nj = f // tf
    col = lambda r, j, tv: jnp.where(tv[r] != 0, j, nj - 1)
    return pl.pallas_call(
        _expert_kernel,
        grid_spec=pltpu.PrefetchScalarGridSpec(
            num_scalar_prefetch=3,
            grid=(n_rows // tm, nj),
            in_specs=[
                pl.BlockSpec(memory_space=pl.ANY),
                pl.BlockSpec((None, d, tf), lambda r, j, s, te, tv: (te[r], 0, col(r, j, tv))),
                pl.BlockSpec((None, d, tf), lambda r, j, s, te, tv: (te[r], 0, col(r, j, tv))),
                pl.BlockSpec((None, tf, d), lambda r, j, s, te, tv: (te[r], col(r, j, tv), 0)),
            ],
            out_specs=pl.BlockSpec((tm * nc, LANES), lambda r, j, s, te, tv: (r, 0)),
            scratch_shapes=[pltpu.VMEM((2, tm * nc, LANES), F32), pltpu.VMEM((tm, d), BF16),
                            pltpu.VMEM((tm, d), F32), pltpu.SemaphoreType.DMA((2,))],
        ),
        out_shape=jax.ShapeDtypeStruct((n_rows * nc, LANES), F32),
        compiler_params=pltpu.CompilerParams(
            dimension_semantics=("arbitrary", "arbitrary"), vmem_limit_bytes=VMEM_LIMIT),
        name="moe_experts",
    )(src_rows, tile_expert, tile_valid, x, w_gate, w_up, w_down)


def _combine_kernel(d1_ref, d2_ref, y_hbm, h_ref, meta_ref, fg_ref, out_ref, a_ref, b_ref, sem):
    tc = h_ref.shape[0]
    base = pl.program_id(0) * tc

    nc = h_ref.shape[1] // LANES

    def issue(r, _):
        for thread, (d_ref, buf) in enumerate(((d1_ref, a_ref), (d2_ref, b_ref))):
            src = pl.multiple_of(d_ref[base + r] * nc, nc)
            pltpu.make_async_copy(y_hbm.at[pl.ds(src, nc)], buf.at[pl.ds(r * nc, nc)],
                                  sem).start(priority=thread % DMA_THREADS)
        return 0

    lax.fori_loop(0, tc, issue, 0, unroll=16)
    pltpu.make_async_copy(y_hbm.at[pl.ds(0, tc * nc)], a_ref, sem).wait()
    pltpu.make_async_copy(y_hbm.at[pl.ds(0, tc * nc)], b_ref, sem).wait()
    meta = meta_ref[...]
    w1 = meta[:, 2:3]
    w2 = meta[:, 3:4]
    moe = w1 * _load_rows_from_tiles(a_ref, nc) + w2 * _load_rows_from_tiles(b_ref, nc)
    out_ref[...] = _rms(h_ref[...] + moe, fg_ref[...])


def _combine(dest1, dest2, y, h2d, meta, fg, *, tc):
    t, d = h2d.shape
    return pl.pallas_call(
        _combine_kernel,
        grid_spec=pltpu.PrefetchScalarGridSpec(
            num_scalar_prefetch=2,
            grid=(t // tc,),
            in_specs=[
                pl.BlockSpec(memory_space=pl.ANY),
                pl.BlockSpec((tc, d), lambda i, d1, d2: (i, 0)),
                pl.BlockSpec((tc, LANES), lambda i, d1, d2: (i, 0)),
                pl.BlockSpec((1, d), lambda i, d1, d2: (0, 0)),
            ],
            out_specs=pl.BlockSpec((tc, d), lambda i, d1, d2: (i, 0)),
            scratch_shapes=[pltpu.VMEM((tc * d // LANES, LANES), F32),
                            pltpu.VMEM((tc * d // LANES, LANES), F32),
                            pltpu.SemaphoreType.DMA(())],
        ),
        out_shape=jax.ShapeDtypeStruct((t, d), F32),
        compiler_params=pltpu.CompilerParams(
            dimension_semantics=("arbitrary",), vmem_limit_bytes=VMEM_LIMIT),
        name="moe_combine",
    )(dest1, dest2, y, h2d, meta, fg)


def _routing_plan(meta_t, counts, *, tm, n_tiles):
    e1 = meta_t[0].astype(jnp.int32)
    e2 = meta_t[1].astype(jnp.int32)
    cnt = counts[0, :N_EXPERTS].astype(jnp.int32)
    padded = (cnt + tm - 1) // tm * tm
    ends = jnp.cumsum(padded)
    offs = ends - padded
    dest1 = offs[e1] + meta_t[4].astype(jnp.int32)
    dest2 = offs[e2] + meta_t[5].astype(jnp.int32)
    start = jnp.arange(n_tiles, dtype=jnp.int32) * tm
    tile_valid = (start < ends[-1]).astype(jnp.int32)
    last = jnp.sum((ends < ends[-1]).astype(jnp.int32))
    passed = jnp.sum((ends[None, :] <= start[:, None]).astype(jnp.int32), axis=1)
    src_rows = _source_rows(dest1, dest2, n_tiles * tm)
    return dest1, dest2, src_rows, jnp.minimum(passed, last), tile_valid


def _source_rows_kernel(d1_ref, d2_ref, src_ref):
    def clear(i, _):
        src_ref[i] = 0
        return 0

    def place(t, _):
        src_ref[d1_ref[t]] = t
        src_ref[d2_ref[t]] = t
        return 0

    lax.fori_loop(0, src_ref.shape[0], clear, 0, unroll=16)
    lax.fori_loop(0, d1_ref.shape[0], place, 0, unroll=8)


def _source_rows(dest1, dest2, n_rows):
    return pl.pallas_call(
        _source_rows_kernel,
        in_specs=[pl.BlockSpec(memory_space=pltpu.SMEM), pl.BlockSpec(memory_space=pltpu.SMEM)],
        out_specs=pl.BlockSpec(memory_space=pltpu.SMEM),
        out_shape=jax.ShapeDtypeStruct((n_rows,), jnp.int32),
        name="moe_source_rows",
    )(dest1, dest2)


def _pad_heads(w, head_dim):
    k = w.shape[0]
    w = w.reshape(k, MLA_HEADS, head_dim)
    w = jnp.pad(w, ((0, 0), (0, 0), (0, LANES - head_dim)))
    return w.reshape(k, MLA_HEADS * LANES)


def _pick(n, cap):
    t = min(n, cap)
    while n % t:
        t //= 2
    return t


def kernel(x, positions, mix_norm, ffn_norm, final_norm, mla_w_in, mla_q_norm, mla_w_q_up,
           mla_kv_norm, mla_w_kv_up, mla_w_o, hgrn_w_in, hgrn_lower_bounds, hgrn_out_norm,
           hgrn_w_o, ffn_w_gate, ffn_w_up, ffn_w_down, moe_router, moe_w_gate, moe_w_up,
           moe_w_down):
    batch, seq, d = x.shape
    t = batch * seq
    x2d = x.reshape(t, d)
    row = lambda v: v.reshape(1, -1)

    w_in = mla_w_in[0]
    n_lat = MLA_Q_LORA + MLA_KV_LORA
    w_in_p = w_in[:, :n_lat].astype(BF16)
    wkr_t = jnp.pad(w_in[:, n_lat:].T,
                    ((MLA_NOPE, LANES - MLA_NOPE - MLA_ROPE), (0, 0))).astype(BF16)
    wq_t = _pad_heads(mla_w_q_up[0], MLA_NOPE + MLA_ROPE).T.astype(BF16)
    w_kv = mla_w_kv_up[0].reshape(MLA_KV_LORA, MLA_HEADS, MLA_NOPE + MLA_V)
    wk_p = _pad_heads(w_kv[:, :, :MLA_NOPE].reshape(MLA_KV_LORA, -1), MLA_NOPE).astype(BF16)
    wv_t = w_kv[:, :, MLA_NOPE:].reshape(MLA_KV_LORA, -1).T.astype(BF16)
    inv_freq = ROPE_THETA ** (-jnp.arange(0, MLA_ROPE, 2, dtype=F32) / MLA_ROPE)
    freq = inv_freq.reshape(MLA_ROPE // 2, 1)
    pos = positions.astype(F32).reshape(1, t)

    q_t, k, v_t = _mla_proj(x2d, pos, row(mix_norm[0]), w_in_p, wkr_t, row(mla_q_norm[0]), wq_t,
                            row(mla_kv_norm[0]), wk_p, wv_t, freq,
                            batch=batch, seq=seq, tm=_pick(seq, 512))
    o = _mla_attn(q_t, k, v_t, tq=_pick(seq, 512), hp=8).reshape(t, MLA_HEADS * MLA_V)
    h = _attn_ffn(x2d, o, mla_w_o[0].astype(BF16), row(ffn_norm[0]),
                  ffn_w_gate[0].astype(BF16), ffn_w_up[0].astype(BF16),
                  ffn_w_down[0].astype(BF16), tm=_pick(t, 512), tf=1792)

    router_p = jnp.pad(moe_router[0], ((0, 0), (0, LANES - N_EXPERTS)))
    h, hn, meta, meta_t, counts = _hgrn(h, row(mix_norm[1]), hgrn_w_in[0].astype(BF16),
                                hgrn_lower_bounds, row(hgrn_out_norm[0]),
                                hgrn_w_o[0].astype(BF16), row(ffn_norm[1]), router_p,
                                batch=batch, seq=seq, tt=_pick(seq, 256), layer=1)
    tm = _pick(t, 512)
    n_tiles = 2 * t // tm + N_EXPERTS
    dest1, dest2, src_rows, tile_expert, tile_valid = _routing_plan(
        meta_t, counts, tm=tm, n_tiles=n_tiles)
    y = _experts(src_rows, tile_expert, tile_valid, hn, moe_w_gate[0].astype(BF16),
                 moe_w_up[0].astype(BF16), moe_w_down[0].astype(BF16), tm=tm, tf=1792)
    out = _combine(dest1, dest2, y, h, meta, row(final_norm), tc=_pick(t, 512))
    return out.reshape(batch, seq, d)
```

```python
import functools

import jax
import jax.numpy as jnp
from jax import lax
from jax.experimental import pallas as pl
from jax.experimental.pallas import tpu as pltpu

EPS = 1e-6
LANES = 128
SUBLANES = 8

MLA_HEADS = 16
MLA_Q_LORA = 384
MLA_KV_LORA = 256
MLA_NOPE = 64
MLA_ROPE = 32
MLA_V = 64
ROPE_THETA = 10000.0

HG_HEADS = 8
HG_DK = 128
HG_CHUNK = 64

N_EXPERTS = 8

VMEM_LIMIT = 56 * 1024 * 1024
DMA_THREADS = 2

BF16 = jnp.bfloat16
F32 = jnp.float32


def _dot(a, b):
    return jnp.dot(a, b, preferred_element_type=F32)


def _dot_nt(a, b):
    return lax.dot_general(a, b, (((1,), (1,)), ((), ())), preferred_element_type=F32)


def _dot_tn(a, b):
    return lax.dot_general(a, b, (((0,), (0,)), ((), ())), preferred_element_type=F32)


def _rms(x, g):
    return x * lax.rsqrt(jnp.mean(x * x, axis=-1, keepdims=True) + EPS) * g


def _silu(x):
    return x * (1.0 / (1.0 + jnp.exp(-x)))


def _store_rows_as_tiles(ref, x):
    n, d = x.shape
    nc = d // LANES
    for c in range(nc):
        ref[pl.ds(c, n, stride=nc), :] = x[:, c * LANES:(c + 1) * LANES]


def _load_rows_from_tiles(ref, nc):
    n = ref.shape[0] // nc
    return jnp.concatenate([ref[pl.ds(c, n, stride=nc), :] for c in range(nc)], axis=1)


def _split3(x):
    hi = x.astype(BF16)
    r = x - hi.astype(F32)
    mid = r.astype(BF16)
    lo = (r - mid.astype(F32)).astype(BF16)
    return hi, mid, lo


def _mla_proj_kernel(x_ref, pos_ref, g_ref, win_ref, wkr_ref, qg_ref, wq_ref, kvg_ref, wk_ref,
                     wv_ref, freq_ref, q_out, k_out, v_out, *, scale):
    half = MLA_ROPE // 2
    x1_rows = slice(MLA_NOPE, MLA_NOPE + half)
    x2_rows = slice(MLA_NOPE + half, MLA_NOPE + MLA_ROPE)
    hn = _rms(x_ref[...], g_ref[...]).astype(BF16)
    proj = _dot(hn, win_ref[...])
    cqn = _rms(proj[:, :MLA_Q_LORA], qg_ref[...]).astype(BF16)
    ckvn = _rms(proj[:, MLA_Q_LORA:], kvg_ref[...]).astype(BF16)

    ang = freq_ref[...] * pos_ref[...]
    cos = jnp.cos(ang)
    sin = jnp.sin(ang)

    def rope_t(slot, mul):
        x1 = slot[x1_rows]
        x2 = slot[x2_rows]
        return jnp.concatenate([slot[:MLA_NOPE] * mul, (x1 * cos - x2 * sin) * mul,
                                (x2 * cos + x1 * sin) * mul, slot[MLA_NOPE + MLA_ROPE:]], axis=0)

    kr_t = _dot_nt(wkr_ref[...], hn)
    kr_roped = rope_t(kr_t, 1.0).T
    q_t = _dot_nt(wq_ref[...], cqn)
    k_pad = _dot(ckvn, wk_ref[...])
    for h in range(MLA_HEADS):
        sl = slice(h * LANES, (h + 1) * LANES)
        q_out[h] = rope_t(q_t[sl], scale).astype(BF16)
        k_out[h] = (k_pad[:, sl] + kr_roped).astype(BF16)
    v_out[...] = _dot_nt(wv_ref[...], ckvn).astype(BF16)


def _mla_proj(x2d, pos, g, w_in_p, wkr_t, qg, wq_t, kvg, wk_p, wv_t, freq, *, batch, seq, tm):
    t = x2d.shape[0]
    d = x2d.shape[1]
    nt = seq // tm
    const = lambda shape: pl.BlockSpec(shape, lambda i: (0,) * len(shape))
    scale = (MLA_NOPE + MLA_ROPE) ** -0.5 * 1.4426950408889634
    return pl.pallas_call(
        functools.partial(_mla_proj_kernel, scale=scale),
        grid=(t // tm,),
        in_specs=[
            pl.BlockSpec((tm, d), lambda i: (i, 0)),
            pl.BlockSpec((1, tm), lambda i: (0, i)),
            const((1, d)),
            const(w_in_p.shape),
            const(wkr_t.shape),
            const((1, MLA_Q_LORA)),
            const(wq_t.shape),
            const((1, MLA_KV_LORA)),
            const(wk_p.shape),
            const(wv_t.shape),
            const(freq.shape),
        ],
        out_specs=[
            pl.BlockSpec((None, MLA_HEADS, LANES, tm), lambda i: (i // nt, 0, 0, i % nt)),
            pl.BlockSpec((None, MLA_HEADS, tm, LANES), lambda i: (i // nt, 0, i % nt, 0)),
            pl.BlockSpec((None, MLA_HEADS * MLA_V, tm), lambda i: (i // nt, 0, i % nt)),
        ],
        out_shape=[
            jax.ShapeDtypeStruct((batch, MLA_HEADS, LANES, seq), BF16),
            jax.ShapeDtypeStruct((batch, MLA_HEADS, seq, LANES), BF16),
            jax.ShapeDtypeStruct((batch, MLA_HEADS * MLA_V, seq), BF16),
        ],
        compiler_params=pltpu.CompilerParams(
            dimension_semantics=("parallel",), vmem_limit_bytes=VMEM_LIMIT),
        name="mla_proj",
    )(x2d, pos, g, w_in_p, wkr_t, qg, wq_t, kvg, wk_p, wv_t, freq)


def _attn_kernel(q_ref, k_ref, v_ref, o_ref, acc_ref, m_ref, vx_ref, sa_ref, sb_ref):
    hp, _, tq = q_ref.shape
    seq = k_ref.shape[1]
    qi = pl.program_id(2)

    @pl.when(qi == 0)
    def _():
        for h in range(hp):
            vx_ref[h, :MLA_V] = v_ref[h * MLA_V:(h + 1) * MLA_V]
            vx_ref[h, MLA_V:] = jnp.ones((vx_ref.shape[1] - MLA_V, seq), BF16)

    causal = (lax.broadcasted_iota(jnp.int32, (tq, tq), 0)
              <= lax.broadcasted_iota(jnp.int32, (tq, tq), 1))

    def scores(dst, blk):
        k0 = pl.multiple_of(blk * tq, tq)
        for h in range(hp):
            dst[h] = _dot(k_ref[h, pl.ds(k0, tq), :], q_ref[h])

    def softmax_pv(src, blk, masked):
        k0 = pl.multiple_of(blk * tq, tq)
        for h in range(hp):
            s = src[h]
            if masked:
                s = jnp.where(causal, s, -jnp.inf)
            m = m_ref[h]
            m_new = jnp.maximum(m, jnp.max(s, axis=0, keepdims=True))
            p = jnp.exp2(s - m_new)
            alpha = jnp.exp2(m - m_new)
            m_ref[h] = m_new
            acc_ref[h] = alpha * acc_ref[h] + _dot(vx_ref[h, :, pl.ds(k0, tq)], p.astype(BF16))

    acc_ref[...] = jnp.zeros_like(acc_ref)
    m_ref[...] = jnp.full_like(m_ref, -jnp.inf)

    scores(sa_ref, 0)

    def pair(jp, _):
        scores(sb_ref, 2 * jp + 1)
        softmax_pv(sa_ref, 2 * jp, False)
        scores(sa_ref, 2 * jp + 2)
        softmax_pv(sb_ref, 2 * jp + 1, False)
        return 0

    lax.fori_loop(0, qi // 2, pair, 0)

    @pl.when(qi % 2 == 1)
    def _():
        scores(sb_ref, qi)
        softmax_pv(sa_ref, qi - 1, False)
        softmax_pv(sb_ref, qi, True)

    @pl.when(qi % 2 == 0)
    def _():
        softmax_pv(sa_ref, qi, True)

    out = jnp.concatenate([acc_ref[h, :MLA_V] / acc_ref[h, MLA_V:MLA_V + 1]
                           for h in range(hp)], axis=0)
    o_ref[...] = out.T.astype(BF16)


def _mla_attn(q_t, k, v_t, *, tq, hp):
    batch, heads, seq, _ = k.shape
    return pl.pallas_call(
        _attn_kernel,
        grid=(batch, heads // hp, seq // tq),
        in_specs=[
            pl.BlockSpec((None, hp, LANES, tq), lambda b, g, i: (b, g, 0, i)),
            pl.BlockSpec((None, hp, seq, LANES), lambda b, g, i: (b, g, 0, 0)),
            pl.BlockSpec((None, hp * MLA_V, seq), lambda b, g, i: (b, g, 0)),
        ],
        out_specs=pl.BlockSpec((None, tq, hp * MLA_V), lambda b, g, i: (b, i, g)),
        out_shape=jax.ShapeDtypeStruct((batch, seq, heads * MLA_V), BF16),
        scratch_shapes=[pltpu.VMEM((hp, MLA_V + 16, tq), F32),
                        pltpu.VMEM((hp, 1, tq), F32), pltpu.VMEM((hp, MLA_V + 16, seq), BF16),
                        pltpu.VMEM((hp, tq, tq), F32), pltpu.VMEM((hp, tq, tq), F32)],
        compiler_params=pltpu.CompilerParams(
            dimension_semantics=("parallel", "parallel", "arbitrary"),
            vmem_limit_bytes=VMEM_LIMIT),
        name="mla_attn",
    )(q_t, k, v_t)


def _attn_ffn_kernel(x_ref, o_ref, wo_ref, g_ref, wg_ref, wu_ref, wd_ref, out_ref, hn_ref):
    j = pl.program_id(1)

    @pl.when(j == 0)
    def _():
        h1 = x_ref[...] + _dot(o_ref[...], wo_ref[...])
        out_ref[...] = h1
        hn_ref[...] = _rms(h1, g_ref[...]).astype(BF16)

    hn = hn_ref[...]
    a = _silu(_dot(hn, wg_ref[...])) * _dot(hn, wu_ref[...])
    out_ref[...] += _dot(a.astype(BF16), wd_ref[...])


def _attn_ffn(x2d, o, w_o, g, w_gate, w_up, w_down, *, tm, tf):
    t, d = x2d.shape
    f = w_gate.shape[1]
    return pl.pallas_call(
        _attn_ffn_kernel,
        grid=(t // tm, f // tf),
        in_specs=[
            pl.BlockSpec((tm, d), lambda i, j: (i, 0)),
            pl.BlockSpec((tm, o.shape[1]), lambda i, j: (i, 0)),
            pl.BlockSpec(w_o.shape, lambda i, j: (0, 0)),
            pl.BlockSpec((1, d), lambda i, j: (0, 0)),
            pl.BlockSpec((d, tf), lambda i, j: (0, j)),
            pl.BlockSpec((d, tf), lambda i, j: (0, j)),
            pl.BlockSpec((tf, d), lambda i, j: (j, 0)),
        ],
        out_specs=pl.BlockSpec((tm, d), lambda i, j: (i, 0)),
        out_shape=jax.ShapeDtypeStruct((t, d), F32),
        scratch_shapes=[pltpu.VMEM((tm, d), BF16)],
        compiler_params=pltpu.CompilerParams(
            dimension_semantics=("parallel", "arbitrary"), vmem_limit_bytes=VMEM_LIMIT),
        name="attn_ffn",
    )(x2d, o, w_o, g, w_gate, w_up, w_down)


def _hgrn_kernel(h_ref, g_ref, win_ref, lbraw_ref, og_ref, wo_ref, fg_ref, router_ref,
                 h_out, hn_out, meta_out, meta_t_out, cnt_out, st_ref, o_scr, cnt_ref, *, layer):
    tt = h_ref.shape[0]
    width = HG_HEADS * HG_DK
    nchunk = tt // HG_CHUNK

    @pl.when(pl.program_id(1) == 0)
    def _():
        st_ref[...] = jnp.zeros_like(st_ref)

    lbraw = lbraw_ref[...]
    e = jnp.exp(lbraw - jnp.max(lbraw, axis=0, keepdims=True))
    sm = e / jnp.sum(e, axis=0, keepdims=True)
    lb = jnp.sum(sm[1:layer + 1], axis=0, keepdims=True)

    h_in = h_ref[...]
    hn = _rms(h_in, g_ref[...]).astype(BF16)
    proj = _dot(hn, win_ref[...])
    q_all = proj[:, :width] * (HG_DK ** -0.5)
    f_all = lb + (1.0 - lb) * (1.0 / (1.0 + jnp.exp(-proj[:, width:2 * width])))
    logf = jnp.log(f_all)
    k_all = 1.0 - f_all

    r = lax.broadcasted_iota(jnp.int32, (tt, tt), 0)
    c = lax.broadcasted_iota(jnp.int32, (tt, tt), 1)
    shift = HG_CHUNK.bit_length() - 1
    tril = (c <= r) & ((r >> shift) == (c >> shift))
    in_chunk = lax.broadcasted_iota(jnp.int32, (tt, 1), 0) & (HG_CHUNK - 1)
    b_all = logf
    for step in (1 << i for i in range(shift)):
        b_all = b_all + jnp.where(in_chunk >= step, pltpu.roll(b_all, step, 0), 0.0)

    for h in range(HG_HEADS):
        sl = slice(h * HG_DK, (h + 1) * HG_DK)
        q = q_all[:, sl]
        k = k_all[:, sl]
        b = b_all[:, sl]
        v = proj[:, 2 * width + h * HG_DK:2 * width + (h + 1) * HG_DK]
        gt = proj[:, 3 * width + h * HG_DK:3 * width + (h + 1) * HG_DK]
        v_b = v.astype(BF16)
        qd = (q * jnp.exp(b)).astype(BF16)
        kd = (k * jnp.exp(-b)).astype(BF16)
        a = jnp.where(tril, _dot_nt(qd, kd), 0.0)
        o = _dot(a.astype(BF16), v_b)

        st = st_ref[h]
        inter = []
        for n in range(nchunk):
            rows = slice(n * HG_CHUNK, (n + 1) * HG_CHUNK)
            b_n = b[rows]
            b_last = b_n[HG_CHUNK - 1:HG_CHUNK]
            inter.append(_dot_nt(qd[rows], st.astype(BF16)))
            kdl = (k[rows] * jnp.exp(b_last - b_n)).astype(BF16)
            st = st * jnp.exp(b_last) + _dot_tn(v_b[rows], kdl)
        st_ref[h] = st
        o = o + jnp.concatenate(inter, axis=0)
        o = _rms(o, og_ref[...]) * _silu(gt)
        o_scr[:, sl] = o.astype(BF16)

    h_new = h_in + _dot(o_scr[...], wo_ref[...])
    h_out[...] = h_new
    hn2 = _rms(h_new, fg_ref[...])
    _store_rows_as_tiles(hn_out, hn2)

    x_hi, x_mid, _ = _split3(hn2)
    router = router_ref[...]
    r_hi = router.astype(BF16)
    r_lo = (router - r_hi.astype(F32)).astype(BF16)
    hi_terms = _dot(x_hi, jnp.concatenate([r_hi, r_lo], axis=1))
    logits = hi_terms[:, :LANES] + _dot(x_mid, r_hi) + hi_terms[:, LANES:]
    lane = lax.broadcasted_iota(jnp.int32, logits.shape, 1)
    neg = -jnp.inf
    logits = jnp.where(lane < N_EXPERTS, logits, neg)
    m1 = jnp.max(logits, axis=-1, keepdims=True)
    i1 = jnp.min(jnp.where(logits == m1, lane, LANES), axis=-1, keepdims=True)
    rest = jnp.where(lane == i1, neg, logits)
    m2 = jnp.max(rest, axis=-1, keepdims=True)
    i2 = jnp.min(jnp.where(rest == m2, lane, LANES), axis=-1, keepdims=True)
    e2 = jnp.exp(m2 - m1)
    w1 = 1.0 / (1.0 + e2)
    w2 = e2 / (1.0 + e2)

    @pl.when((pl.program_id(0) == 0) & (pl.program_id(1) == 0))
    def _():
        cnt_ref[...] = jnp.zeros_like(cnt_ref)

    onehot = jnp.where((lane == i1) | (lane == i2), 1.0, 0.0)
    earlier = jnp.where(c < r, 1.0, 0.0).astype(BF16)
    rank = _dot(earlier, onehot.astype(BF16)) + cnt_ref[...]
    cnt = cnt_ref[...] + jnp.sum(onehot, axis=0, keepdims=True)
    cnt_ref[...] = cnt
    cnt_out[...] = jnp.broadcast_to(cnt, cnt_out.shape)
    rank1 = jnp.sum(jnp.where(lane == i1, rank, 0.0), axis=-1, keepdims=True)
    rank2 = jnp.sum(jnp.where(lane == i2, rank, 0.0), axis=-1, keepdims=True)
    fields = (i1.astype(F32), i2.astype(F32), w1, w2, rank1, rank2)
    meta = jnp.zeros(logits.shape, F32)
    for idx, val in enumerate(fields):
        meta = jnp.where(lane == idx, val, meta)
    meta_out[...] = meta
    meta_t_out[...] = meta.T[:SUBLANES]


def _hgrn(h2d, g, w_in, lb_raw, og, w_o, fg, router_p, *, batch, seq, tt, layer):
    t, d = h2d.shape
    nt = seq // tt
    width = HG_HEADS * HG_DK
    tok = lambda cols: pl.BlockSpec((tt, cols), lambda b, i: (b * nt + i, 0))
    const = lambda shape: pl.BlockSpec(shape, lambda b, i: (0,) * len(shape))
    return pl.pallas_call(
        functools.partial(_hgrn_kernel, layer=layer),
        grid=(batch, nt),
        in_specs=[tok(d), const((1, d)), const(w_in.shape), const(lb_raw.shape),
                  const((1, HG_DK)), const(w_o.shape), const((1, d)), const(router_p.shape)],
        out_specs=[tok(d),
                   pl.BlockSpec((tt * (d // LANES), LANES), lambda b, i: (b * nt + i, 0)),
                   tok(LANES), pl.BlockSpec((SUBLANES, tt), lambda b, i: (0, b * nt + i)),
                   const((SUBLANES, LANES))],
        out_shape=[jax.ShapeDtypeStruct((t, d), F32),
                   jax.ShapeDtypeStruct((t * (d // LANES), LANES), F32),
                   jax.ShapeDtypeStruct((t, LANES), F32),
                   jax.ShapeDtypeStruct((SUBLANES, t), F32),
                   jax.ShapeDtypeStruct((SUBLANES, LANES), F32)],
        scratch_shapes=[pltpu.VMEM((HG_HEADS, HG_DK, HG_DK), F32),
                        pltpu.VMEM((tt, width), BF16),
                        pltpu.VMEM((1, LANES), F32)],
        compiler_params=pltpu.CompilerParams(
            dimension_semantics=("arbitrary", "arbitrary"), vmem_limit_bytes=VMEM_LIMIT),
        name="hgrn",
    )(h2d, g, w_in, lb_raw, og, w_o, fg, router_p)


def _expert_kernel(src_ref, te_ref, tv_ref, x_hbm, wg_ref, wu_ref, wd_ref, y_ref,
                   xbuf, xb_ref, acc_ref, sems):
    del te_ref
    r = pl.program_id(0)
    j = pl.program_id(1)
    n_tiles = pl.num_programs(0)
    nj = pl.num_programs(1)
    tm = xb_ref.shape[0]
    nc = xb_ref.shape[1] // LANES
    valid = tv_ref[r] != 0
    has_next = r + 1 < n_tiles

    def row_copy(tile, row):
        slot = tile % 2
        src = pl.multiple_of(src_ref[tile * tm + row] * nc, nc)
        return pltpu.make_async_copy(x_hbm.at[pl.ds(src, nc)],
                                     xbuf.at[slot, pl.ds(row * nc, nc)], sems.at[slot])

    def swiglu(x):
        a = _silu(_dot(x, wg_ref[...])) * _dot(x, wu_ref[...])
        return _dot(a.astype(BF16), wd_ref[...])

    def swiglu_step(first):
        if not first:
            acc_ref[...] += swiglu(xb_ref[...])
            return
        half = tm // 2
        for hlf in range(2):
            rows = slice(hlf * half, (hlf + 1) * half)
            x = _load_rows_from_tiles(xbuf.at[r % 2, pl.ds(hlf * half * nc, half * nc)], nc)
            x = x.astype(BF16)
            xb_ref[rows] = x
            acc_ref[rows] = swiglu(x)

    @pl.when((r == 0) & (j == 0))
    def _():
        lax.fori_loop(0, tm, lambda i, c: (row_copy(r, i).start(), c)[1], 0)

    @pl.when((j == 0) & ((r == 0) | (tv_ref[jnp.maximum(r - 1, 0)] != 0)))
    def _():
        pltpu.make_async_copy(x_hbm.at[pl.ds(0, tm * nc)], xbuf.at[r % 2],
                              sems.at[r % 2]).wait()

    part = tm // (nj - 1)

    @pl.when(valid & (j == 0))
    def _():
        swiglu_step(True)

    @pl.when(valid & has_next & (j > 0))
    def _():
        for i in range(part):
            row_copy(r + 1, (j - 1) * part + i).start(priority=i % DMA_THREADS)
        swiglu_step(False)

    @pl.when(valid & jnp.logical_not(has_next) & (j > 0))
    def _():
        swiglu_step(False)

    @pl.when(valid & (j == nj - 1))
    def _():
        _store_rows_as_tiles(y_ref, acc_ref[...])

    @pl.when(jnp.logical_not(valid) & (j == nj - 1))
    def _():
        y_ref[...] = jnp.zeros_like(y_ref)


def _experts(src_rows, tile_expert, tile_valid, x, w_gate, w_up, w_down, *, tm, tf):
    n_rows = src_rows.shape[0]
    d = w_gate.shape[1]
    nc = d // LANES
    f = w_gate.shape[2]
    nj = f // tf
    col = lambda r, j, tv: jnp.where(tv[r] != 0, j, nj - 1)
    return pl.pallas_call(
        _expert_kernel,
        grid_spec=pltpu.PrefetchScalarGridSpec(
            num_scalar_prefetch=3,
            grid=(n_rows // tm, nj),
            in_specs=[
                pl.BlockSpec(memory_space=pl.ANY),
                pl.BlockSpec((None, d, tf), lambda r, j, s, te, tv: (te[r], 0, col(r, j, tv))),
                pl.BlockSpec((None, d, tf), lambda r, j, s, te, tv: (te[r], 0, col(r, j, tv))),
                pl.BlockSpec((None, tf, d), lambda r, j, s, te, tv: (te[r], col(r, j, tv), 0)),
            ],
            out_specs=pl.BlockSpec((tm * nc, LANES), lambda r, j, s, te, tv: (r, 0)),
            scratch_shapes=[pltpu.VMEM((2, tm * nc, LANES), F32), pltpu.VMEM((tm, d), BF16),
                            pltpu.VMEM((tm, d), F32), pltpu.SemaphoreType.DMA((2,))],
        ),
        out_shape=jax.ShapeDtypeStruct((n_rows * nc, LANES), F32),
        compiler_params=pltpu.CompilerParams(
            dimension_semantics=("arbitrary", "arbitrary"), vmem_limit_bytes=VMEM_LIMIT),
        name="moe_experts",
    )(src_rows, tile_expert, tile_valid, x, w_gate, w_up, w_down)


def _combine_kernel(d1_ref, d2_ref, y_hbm, h_ref, meta_ref, fg_ref, out_ref, a_ref, b_ref, sem):
    tc = h_ref.shape[0]
    base = pl.program_id(0) * tc

    nc = h_ref.shape[1] // LANES

    def issue(r, _):
        for thread, (d_ref, buf) in enumerate(((d1_ref, a_ref), (d2_ref, b_ref))):
            src = pl.multiple_of(d_ref[base + r] * nc, nc)
            pltpu.make_async_copy(y_hbm.at[pl.ds(src, nc)], buf.at[pl.ds(r * nc, nc)],
                                  sem).start(priority=thread % DMA_THREADS)
        return 0

    lax.fori_loop(0, tc, issue, 0, unroll=16)
    pltpu.make_async_copy(y_hbm.at[pl.ds(0, tc * nc)], a_ref, sem).wait()
    pltpu.make_async_copy(y_hbm.at[pl.ds(0, tc * nc)], b_ref, sem).wait()
    meta = meta_ref[...]
    w1 = meta[:, 2:3]
    w2 = meta[:, 3:4]
    moe = w1 * _load_rows_from_tiles(a_ref, nc) + w2 * _load_rows_from_tiles(b_ref, nc)
    out_ref[...] = _rms(h_ref[...] + moe, fg_ref[...])


def _combine(dest1, dest2, y, h2d, meta, fg, *, tc):
    t, d = h2d.shape
    return pl.pallas_call(
        _combine_kernel,
        grid_spec=pltpu.PrefetchScalarGridSpec(
            num_scalar_prefetch=2,
            grid=(t // tc,),
            in_specs=[
                pl.BlockSpec(memory_space=pl.ANY),
                pl.BlockSpec((tc, d), lambda i, d1, d2: (i, 0)),
                pl.BlockSpec((tc, LANES), lambda i, d1, d2: (i, 0)),
                pl.BlockSpec((1, d), lambda i, d1, d2: (0, 0)),
            ],
            out_specs=pl.BlockSpec((tc, d), lambda i, d1, d2: (i, 0)),
            scratch_shapes=[pltpu.VMEM((tc * d // LANES, LANES), F32),
                            pltpu.VMEM((tc * d // LANES, LANES), F32),
                            pltpu.SemaphoreType.DMA(())],
        ),
        out_shape=jax.ShapeDtypeStruct((t, d), F32),
        compiler_params=pltpu.CompilerParams(
            dimension_semantics=("arbitrary",), vmem_limit_bytes=VMEM_LIMIT),
        name="moe_combine",
    )(dest1, dest2, y, h2d, meta, fg)


def _routing_plan(meta_t, counts, *, tm, n_tiles):
    e1 = meta_t[0].astype(jnp.int32)
    e2 = meta_t[1].astype(jnp.int32)
    cnt = counts[0, :N_EXPERTS].astype(jnp.int32)
    padded = (cnt + tm - 1) // tm * tm
    ends = jnp.cumsum(padded)
    offs = ends - padded
    dest1 = offs[e1] + meta_t[4].astype(jnp.int32)
    dest2 = offs[e2] + meta_t[5].astype(jnp.int32)
    start = jnp.arange(n_tiles, dtype=jnp.int32) * tm
    tile_valid = (start < ends[-1]).astype(jnp.int32)
    last = jnp.sum((ends < ends[-1]).astype(jnp.int32))
    passed = jnp.sum((ends[None, :] <= start[:, None]).astype(jnp.int32), axis=1)
    src_rows = _source_rows(dest1, dest2, n_tiles * tm)
    return dest1, dest2, src_rows, jnp.minimum(passed, last), tile_valid


def _source_rows_kernel(d1_ref, d2_ref, src_ref):
    def clear(i, _):
        src_ref[i] = 0
        return 0

    def place(t, _):
        src_ref[d1_ref[t]] = t
        src_ref[d2_ref[t]] = t
        return 0

    lax.fori_loop(0, src_ref.shape[0], clear, 0, unroll=16)
    lax.fori_loop(0, d1_ref.shape[0], place, 0, unroll=8)


def _source_rows(dest1, dest2, n_rows):
    return pl.pallas_call(
        _source_rows_kernel,
        in_specs=[pl.BlockSpec(memory_space=pltpu.SMEM), pl.BlockSpec(memory_space=pltpu.SMEM)],
        out_specs=pl.BlockSpec(memory_space=pltpu.SMEM),
        out_shape=jax.ShapeDtypeStruct((n_rows,), jnp.int32),
        name="moe_source_rows",
    )(dest1, dest2)


def _pad_heads(w, head_dim):
    k = w.shape[0]
    w = w.reshape(k, MLA_HEADS, head_dim)
    w = jnp.pad(w, ((0, 0), (0, 0), (0, LANES - head_dim)))
    return w.reshape(k, MLA_HEADS * LANES)


def _pick(n, cap):
    t = min(n, cap)
    while n % t:
        t //= 2
    return t


def kernel(x, positions, mix_norm, ffn_norm, final_norm, mla_w_in, mla_q_norm, mla_w_q_up,
           mla_kv_norm, mla_w_kv_up, mla_w_o, hgrn_w_in, hgrn_lower_bounds, hgrn_out_norm,
           hgrn_w_o, ffn_w_gate, ffn_w_up, ffn_w_down, moe_router, moe_w_gate, moe_w_up,
           moe_w_down):
    batch, seq, d = x.shape
    t = batch * seq
    x2d = x.reshape(t, d)
    row = lambda v: v.reshape(1, -1)

    w_in = mla_w_in[0]
    n_lat = MLA_Q_LORA + MLA_KV_LORA
    w_in_p = w_in[:, :n_lat].astype(BF16)
    wkr_t = jnp.pad(w_in[:, n_lat:].T,
                    ((MLA_NOPE, LANES - MLA_NOPE - MLA_ROPE), (0, 0))).astype(BF16)
    wq_t = _pad_heads(mla_w_q_up[0], MLA_NOPE + MLA_ROPE).T.astype(BF16)
    w_kv = mla_w_kv_up[0].reshape(MLA_KV_LORA, MLA_HEADS, MLA_NOPE + MLA_V)
    wk_p = _pad_heads(w_kv[:, :, :MLA_NOPE].reshape(MLA_KV_LORA, -1), MLA_NOPE).astype(BF16)
    wv_t = w_kv[:, :, MLA_NOPE:].reshape(MLA_KV_LORA, -1).T.astype(BF16)
    inv_freq = ROPE_THETA ** (-jnp.arange(0, MLA_ROPE, 2, dtype=F32) / MLA_ROPE)
    freq = inv_freq.reshape(MLA_ROPE // 2, 1)
    pos = positions.astype(F32).reshape(1, t)

    q_t, k, v_t = _mla_proj(x2d, pos, row(mix_norm[0]), w_in_p, wkr_t, row(mla_q_norm[0]), wq_t,
                            row(mla_kv_norm[0]), wk_p, wv_t, freq,
                            batch=batch, seq=seq, tm=_pick(seq, 512))
    o = _mla_attn(q_t, k, v_t, tq=_pick(seq, 512), hp=8).reshape(t, MLA_HEADS * MLA_V)
    h = _attn_ffn(x2d, o, mla_w_o[0].astype(BF16), row(ffn_norm[0]),
                  ffn_w_gate[0].astype(BF16), ffn_w_up[0].astype(BF16),
                  ffn_w_down[0].astype(BF16), tm=_pick(t, 512), tf=1792)

    router_p = jnp.pad(moe_router[0], ((0, 0), (0, LANES - N_EXPERTS)))
    h, hn, meta, meta_t, counts = _hgrn(h, row(mix_norm[1]), hgrn_w_in[0].astype(BF16),
                                hgrn_lower_bounds, row(hgrn_out_norm[0]),
                                hgrn_w_o[0].astype(BF16), row(ffn_norm[1]), router_p,
                                batch=batch, seq=seq, tt=_pick(seq, 256), layer=1)
    tm = _pick(t, 512)
    n_tiles = 2 * t // tm + N_EXPERTS
    dest1, dest2, src_rows, tile_expert, tile_valid = _routing_plan(
        meta_t, counts, tm=tm, n_tiles=n_tiles)
    y = _experts(src_rows, tile_expert, tile_valid, hn, moe_w_gate[0].astype(BF16),
                 moe_w_up[0].astype(BF16), moe_w_down[0].astype(BF16), tm=tm, tf=1792)
    out = _combine(dest1, dest2, y, h, meta, row(final_norm), tc=_pick(t, 512))
    return out.reshape(batch, seq, d)
```

```python
import functools

import jax
import jax.numpy as jnp
from jax import lax
from jax.experimental import pallas as pl
from jax.experimental.pallas import tpu as pltpu

EPS = 1e-6
LANES = 128
SUBLANES = 8

MLA_HEADS = 16
MLA_Q_LORA = 384
MLA_KV_LORA = 256
MLA_NOPE = 64
MLA_ROPE = 32
MLA_V = 64
ROPE_THETA = 10000.0

HG_HEADS = 8
HG_DK = 128
HG_CHUNK = 64

N_EXPERTS = 8

VMEM_LIMIT = 56 * 1024 * 1024
DMA_THREADS = 2

BF16 = jnp.bfloat16
F32 = jnp.float32


def _dot(a, b):
    return jnp.dot(a, b, preferred_element_type=F32)


def _dot_nt(a, b):
    return lax.dot_general(a, b, (((1,), (1,)), ((), ())), preferred_element_type=F32)


def _dot_tn(a, b):
    return lax.dot_general(a, b, (((0,), (0,)), ((), ())), preferred_element_type=F32)


def _rms(x, g):
    return x * lax.rsqrt(jnp.mean(x * x, axis=-1, keepdims=True) + EPS) * g


def _silu(x):
    return x * (1.0 / (1.0 + jnp.exp(-x)))


def _store_rows_as_tiles(ref, x):
    n, d = x.shape
    nc = d // LANES
    for c in range(nc):
        ref[pl.ds(c, n, stride=nc), :] = x[:, c * LANES:(c + 1) * LANES]


def _load_rows_from_tiles(ref, nc):
    n = ref.shape[0] // nc
    return jnp.concatenate([ref[pl.ds(c, n, stride=nc), :] for c in range(nc)], axis=1)


def _split3(x):
    hi = x.astype(BF16)
    r = x - hi.astype(F32)
    mid = r.astype(BF16)
    lo = (r - mid.astype(F32)).astype(BF16)
    return hi, mid, lo


def _mla_proj_kernel(x_ref, pos_ref, g_ref, win_ref, wkr_ref, qg_ref, wq_ref, kvg_ref, wk_ref,
                     wv_ref, freq_ref, q_out, k_out, v_out, *, scale):
    half = MLA_ROPE // 2
    x1_rows = slice(MLA_NOPE, MLA_NOPE + half)
    x2_rows = slice(MLA_NOPE + half, MLA_NOPE + MLA_ROPE)
    hn = _rms(x_ref[...], g_ref[...]).astype(BF16)
    proj = _dot(hn, win_ref[...])
    cqn = _rms(proj[:, :MLA_Q_LORA], qg_ref[...]).astype(BF16)
    ckvn = _rms(proj[:, MLA_Q_LORA:], kvg_ref[...]).astype(BF16)

    ang = freq_ref[...] * pos_ref[...]
    cos = jnp.cos(ang)
    sin = jnp.sin(ang)

    def rope_t(slot, mul):
        x1 = slot[x1_rows]
        x2 = slot[x2_rows]
        return jnp.concatenate([slot[:MLA_NOPE] * mul, (x1 * cos - x2 * sin) * mul,
                                (x2 * cos + x1 * sin) * mul, slot[MLA_NOPE + MLA_ROPE:]], axis=0)

    kr_t = _dot_nt(wkr_ref[...], hn)
    kr_roped = rope_t(kr_t, 1.0).T
    q_t = _dot_nt(wq_ref[...], cqn)
    k_pad = _dot(ckvn, wk_ref[...])
    for h in range(MLA_HEADS):
        sl = slice(h * LANES, (h + 1) * LANES)
        q_out[h] = rope_t(q_t[sl], scale).astype(BF16)
        k_out[h] = (k_pad[:, sl] + kr_roped).astype(BF16)
    v_out[...] = _dot_nt(wv_ref[...], ckvn).astype(BF16)


def _mla_proj(x2d, pos, g, w_in_p, wkr_t, qg, wq_t, kvg, wk_p, wv_t, freq, *, batch, seq, tm):
    t = x2d.shape[0]
    d = x2d.shape[1]
    nt = seq // tm
    const = lambda shape: pl.BlockSpec(shape, lambda i: (0,) * len(shape))
    scale = (MLA_NOPE + MLA_ROPE) ** -0.5 * 1.4426950408889634
    return pl.pallas_call(
        functools.partial(_mla_proj_kernel, scale=scale),
        grid=(t // tm,),
        in_specs=[
            pl.BlockSpec((tm, d), lambda i: (i, 0)),
            pl.BlockSpec((1, tm), lambda i: (0, i)),
            const((1, d)),
            const(w_in_p.shape),
            const(wkr_t.shape),
            const((1, MLA_Q_LORA)),
            const(wq_t.shape),
            const((1, MLA_KV_LORA)),
            const(wk_p.shape),
            const(wv_t.shape),
            const(freq.shape),
        ],
        out_specs=[
            pl.BlockSpec((None, MLA_HEADS, LANES, tm), lambda i: (i // nt, 0, 0, i % nt)),
            pl.BlockSpec((None, MLA_HEADS, tm, LANES), lambda i: (i // nt, 0, i % nt, 0)),
            pl.BlockSpec((None, MLA_HEADS * MLA_V, tm), lambda i: (i // nt, 0, i % nt)),
        ],
        out_shape=[
            jax.ShapeDtypeStruct((batch, MLA_HEADS, LANES, seq), BF16),
            jax.ShapeDtypeStruct((batch, MLA_HEADS, seq, LANES), BF16),
            jax.ShapeDtypeStruct((batch, MLA_HEADS * MLA_V, seq), BF16),
        ],
        compiler_params=pltpu.CompilerParams(
            dimension_semantics=("parallel",), vmem_limit_bytes=VMEM_LIMIT),
        name="mla_proj",
    )(x2d, pos, g, w_in_p, wkr_t, qg, wq_t, kvg, wk_p, wv_t, freq)


def _attn_kernel(q_ref, k_ref, v_ref, o_ref, acc_ref, m_ref, vx_ref, sa_ref, sb_ref):
    hp, _, tq = q_ref.shape
    seq = k_ref.shape[1]
    qi = pl.program_id(2)

    @pl.when(qi == 0)
    def _():
        for h in range(hp):
            vx_ref[h, :MLA_V] = v_ref[h * MLA_V:(h + 1) * MLA_V]
            vx_ref[h, MLA_V:] = jnp.ones((vx_ref.shape[1] - MLA_V, seq), BF16)

    causal = (lax.broadcasted_iota(jnp.int32, (tq, tq), 0)
              <= lax.broadcasted_iota(jnp.int32, (tq, tq), 1))

    def scores(dst, blk):
        k0 = pl.multiple_of(blk * tq, tq)
        for h in range(hp):
            dst[h] = _dot(k_ref[h, pl.ds(k0, tq), :], q_ref[h])

    def softmax_pv(src, blk, masked):
        k0 = pl.multiple_of(blk * tq, tq)
        for h in range(hp):
            s = src[h]
            if masked:
                s = jnp.where(causal, s, -jnp.inf)
            m = m_ref[h]
            m_new = jnp.maximum(m, jnp.max(s, axis=0, keepdims=True))
            p = jnp.exp2(s - m_new)
            alpha = jnp.exp2(m - m_new)
            m_ref[h] = m_new
            acc_ref[h] = alpha * acc_ref[h] + _dot(vx_ref[h, :, pl.ds(k0, tq)], p.astype(BF16))

    acc_ref[...] = jnp.zeros_like(acc_ref)
    m_ref[...] = jnp.full_like(m_ref, -jnp.inf)

    scores(sa_ref, 0)

    def pair(jp, _):
        scores(sb_ref, 2 * jp + 1)
        softmax_pv(sa_ref, 2 * jp, False)
        scores(sa_ref, 2 * jp + 2)
        softmax_pv(sb_ref, 2 * jp + 1, False)
        return 0

    lax.fori_loop(0, qi // 2, pair, 0)

    @pl.when(qi % 2 == 1)
    def _():
        scores(sb_ref, qi)
        softmax_pv(sa_ref, qi - 1, False)
        softmax_pv(sb_ref, qi, True)

    @pl.when(qi % 2 == 0)
    def _():
        softmax_pv(sa_ref, qi, True)

    out = jnp.concatenate([acc_ref[h, :MLA_V] / acc_ref[h, MLA_V:MLA_V + 1]
                           for h in range(hp)], axis=0)
    o_ref[...] = out.T.astype(BF16)


def _mla_attn(q_t, k, v_t, *, tq, hp):
    batch, heads, seq, _ = k.shape
    return pl.pallas_call(
        _attn_kernel,
        grid=(batch, heads // hp, seq // tq),
        in_specs=[
            pl.BlockSpec((None, hp, LANES, tq), lambda b, g, i: (b, g, 0, i)),
            pl.BlockSpec((None, hp, seq, LANES), lambda b, g, i: (b, g, 0, 0)),
            pl.BlockSpec((None, hp * MLA_V, seq), lambda b, g, i: (b, g, 0)),
        ],
        out_specs=pl.BlockSpec((None, tq, hp * MLA_V), lambda b, g, i: (b, i, g)),
        out_shape=jax.ShapeDtypeStruct((batch, seq, heads * MLA_V), BF16),
        scratch_shapes=[pltpu.VMEM((hp, MLA_V + 16, tq), F32),
                        pltpu.VMEM((hp, 1, tq), F32), pltpu.VMEM((hp, MLA_V + 16, seq), BF16),
                        pltpu.VMEM((hp, tq, tq), F32), pltpu.VMEM((hp, tq, tq), F32)],
        compiler_params=pltpu.CompilerParams(
            dimension_semantics=("parallel", "parallel", "arbitrary"),
            vmem_limit_bytes=VMEM_LIMIT),
        name="mla_attn",
    )(q_t, k, v_t)


def _attn_ffn_kernel(x_ref, o_ref, wo_ref, g_ref, wg_ref, wu_ref, wd_ref, out_ref, hn_ref):
    j = pl.program_id(1)

    @pl.when(j == 0)
    def _():
        h1 = x_ref[...] + _dot(o_ref[...], wo_ref[...])
        out_ref[...] = h1
        hn_ref[...] = _rms(h1, g_ref[...]).astype(BF16)

    hn = hn_ref[...]
    a = _silu(_dot(hn, wg_ref[...])) * _dot(hn, wu_ref[...])
    out_ref[...] += _dot(a.astype(BF16), wd_ref[...])


def _attn_ffn(x2d, o, w_o, g, w_gate, w_up, w_down, *, tm, tf):
    t, d = x2d.shape
    f = w_gate.shape[1]
    return pl.pallas_call(
        _attn_ffn_kernel,
        grid=(t // tm, f // tf),
        in_specs=[
            pl.BlockSpec((tm, d), lambda i, j: (i, 0)),
            pl.BlockSpec((tm, o.shape[1]), lambda i, j: (i, 0)),
            pl.BlockSpec(w_o.shape, lambda i, j: (0, 0)),
            pl.BlockSpec((1, d), lambda i, j: (0, 0)),
            pl.BlockSpec((d, tf), lambda i, j: (0, j)),
            pl.BlockSpec((d, tf), lambda i, j: (0, j)),
            pl.BlockSpec((tf, d), lambda i, j: (j, 0)),
        ],
        out_specs=pl.BlockSpec((tm, d), lambda i, j: (i, 0)),
        out_shape=jax.ShapeDtypeStruct((t, d), F32),
        scratch_shapes=[pltpu.VMEM((tm, d), BF16)],
        compiler_params=pltpu.CompilerParams(
            dimension_semantics=("parallel", "arbitrary"), vmem_limit_bytes=VMEM_LIMIT),
        name="attn_ffn",
    )(x2d, o, w_o, g, w_gate, w_up, w_down)


def _hgrn_kernel(h_ref, g_ref, win_ref, lbraw_ref, og_ref, wo_ref, fg_ref, router_ref,
                 h_out, hn_out, meta_out, meta_t_out, cnt_out, st_ref, o_scr, cnt_ref, *, layer):
    tt = h_ref.shape[0]
    width = HG_HEADS * HG_DK
    nchunk = tt // HG_CHUNK

    @pl.when(pl.program_id(1) == 0)
    def _():
        st_ref[...] = jnp.zeros_like(st_ref)

    lbraw = lbraw_ref[...]
    e = jnp.exp(lbraw - jnp.max(lbraw, axis=0, keepdims=True))
    sm = e / jnp.sum(e, axis=0, keepdims=True)
    lb = jnp.sum(sm[1:layer + 1], axis=0, keepdims=True)

    h_in = h_ref[...]
    hn = _rms(h_in, g_ref[...]).astype(BF16)
    proj = _dot(hn, win_ref[...])
    q_all = proj[:, :width] * (HG_DK ** -0.5)
    f_all = lb + (1.0 - lb) * (1.0 / (1.0 + jnp.exp(-proj[:, width:2 * width])))
    logf = jnp.log(f_all)
    k_all = 1.0 - f_all

    r = lax.broadcasted_iota(jnp.int32, (tt, tt), 0)
    c = lax.broadcasted_iota(jnp.int32, (tt, tt), 1)
    shift = HG_CHUNK.bit_length() - 1
    tril = (c <= r) & ((r >> shift) == (c >> shift))
    in_chunk = lax.broadcasted_iota(jnp.int32, (tt, 1), 0) & (HG_CHUNK - 1)
    b_all = logf
    for step in (1 << i for i in range(shift)):
        b_all = b_all + jnp.where(in_chunk >= step, pltpu.roll(b_all, step, 0), 0.0)

    for h in range(HG_HEADS):
        sl = slice(h * HG_DK, (h + 1) * HG_DK)
        q = q_all[:, sl]
        k = k_all[:, sl]
        b = b_all[:, sl]
        v = proj[:, 2 * width + h * HG_DK:2 * width + (h + 1) * HG_DK]
        gt = proj[:, 3 * width + h * HG_DK:3 * width + (h + 1) * HG_DK]
        v_b = v.astype(BF16)
        qd = (q * jnp.exp(b)).astype(BF16)
        kd = (k * jnp.exp(-b)).astype(BF16)
        a = jnp.where(tril, _dot_nt(qd, kd), 0.0)
        o = _dot(a.astype(BF16), v_b)

        st = st_ref[h]
        inter = []
        for n in range(nchunk):
            rows = slice(n * HG_CHUNK, (n + 1) * HG_CHUNK)
            b_n = b[rows]
            b_last = b_n[HG_CHUNK - 1:HG_CHUNK]
            inter.append(_dot_nt(qd[rows], st.astype(BF16)))
            kdl = (k[rows] * jnp.exp(b_last - b_n)).astype(BF16)
            st = st * jnp.exp(b_last) + _dot_tn(v_b[rows], kdl)
        st_ref[h] = st
        o = o + jnp.concatenate(inter, axis=0)
        o = _rms(o, og_ref[...]) * _silu(gt)
        o_scr[:, sl] = o.astype(BF16)

    h_new = h_in + _dot(o_scr[...], wo_ref[...])
    h_out[...] = h_new
    hn2 = _rms(h_new, fg_ref[...])
    _store_rows_as_tiles(hn_out, hn2)

    x_hi, x_mid, _ = _split3(hn2)
    router = router_ref[...]
    r_hi = router.astype(BF16)
    r_lo = (router - r_hi.astype(F32)).astype(BF16)
    hi_terms = _dot(x_hi, jnp.concatenate([r_hi, r_lo], axis=1))
    logits = hi_terms[:, :LANES] + _dot(x_mid, r_hi) + hi_terms[:, LANES:]
    lane = lax.broadcasted_iota(jnp.int32, logits.shape, 1)
    neg = -jnp.inf
    logits = jnp.where(lane < N_EXPERTS, logits, neg)
    m1 = jnp.max(logits, axis=-1, keepdims=True)
    i1 = jnp.min(jnp.where(logits == m1, lane, LANES), axis=-1, keepdims=True)
    rest = jnp.where(lane == i1, neg, logits)
    m2 = jnp.max(rest, axis=-1, keepdims=True)
    i2 = jnp.min(jnp.where(rest == m2, lane, LANES), axis=-1, keepdims=True)
    e2 = jnp.exp(m2 - m1)
    w1 = 1.0 / (1.0 + e2)
    w2 = e2 / (1.0 + e2)

    @pl.when((pl.program_id(0) == 0) & (pl.program_id(1) == 0))
    def _():
        cnt_ref[...] = jnp.zeros_like(cnt_ref)

    onehot = jnp.where((lane == i1) | (lane == i2), 1.0, 0.0)
    earlier = jnp.where(c < r, 1.0, 0.0).astype(BF16)
    rank = _dot(earlier, onehot.astype(BF16)) + cnt_ref[...]
    cnt = cnt_ref[...] + jnp.sum(onehot, axis=0, keepdims=True)
    cnt_ref[...] = cnt
    cnt_out[...] = jnp.broadcast_to(cnt, cnt_out.shape)
    rank1 = jnp.sum(jnp.where(lane == i1, rank, 0.0), axis=-1, keepdims=True)
    rank2 = jnp.sum(jnp.where(lane == i2, rank, 0.0), axis=-1, keepdims=True)
    fields = (i1.astype(F32), i2.astype(F32), w1, w2, rank1, rank2)
    meta = jnp.zeros(logits.shape, F32)
    for idx, val in enumerate(fields):
        meta = jnp.where(lane == idx, val, meta)
    meta_out[...] = meta
    meta_t_out[...] = meta.T[:SUBLANES]


def _hgrn(h2d, g, w_in, lb_raw, og, w_o, fg, router_p, *, batch, seq, tt, layer):
    t, d = h2d.shape
    nt = seq // tt
    width = HG_HEADS * HG_DK
    tok = lambda cols: pl.BlockSpec((tt, cols), lambda b, i: (b * nt + i, 0))
    const = lambda shape: pl.BlockSpec(shape, lambda b, i: (0,) * len(shape))
    return pl.pallas_call(
        functools.partial(_hgrn_kernel, layer=layer),
        grid=(batch, nt),
        in_specs=[tok(d), const((1, d)), const(w_in.shape), const(lb_raw.shape),
                  const((1, HG_DK)), const(w_o.shape), const((1, d)), const(router_p.shape)],
        out_specs=[tok(d),
                   pl.BlockSpec((tt * (d // LANES), LANES), lambda b, i: (b * nt + i, 0)),
                   tok(LANES), pl.BlockSpec((SUBLANES, tt), lambda b, i: (0, b * nt + i)),
                   const((SUBLANES, LANES))],
        out_shape=[jax.ShapeDtypeStruct((t, d), F32),
                   jax.ShapeDtypeStruct((t * (d // LANES), LANES), F32),
                   jax.ShapeDtypeStruct((t, LANES), F32),
                   jax.ShapeDtypeStruct((SUBLANES, t), F32),
                   jax.ShapeDtypeStruct((SUBLANES, LANES), F32)],
        scratch_shapes=[pltpu.VMEM((HG_HEADS, HG_DK, HG_DK), F32),
                        pltpu.VMEM((tt, width), BF16),
                        pltpu.VMEM((1, LANES), F32)],
        compiler_params=pltpu.CompilerParams(
            dimension_semantics=("arbitrary", "arbitrary"), vmem_limit_bytes=VMEM_LIMIT),
        name="hgrn",
    )(h2d, g, w_in, lb_raw, og, w_o, fg, router_p)


def _expert_kernel(src_ref, te_ref, tv_ref, x_hbm, wg_ref, wu_ref, wd_ref, y_ref,
                   xbuf, xb_ref, acc_ref, sems):
    del te_ref
    r = pl.program_id(0)
    j = pl.program_id(1)
    n_tiles = pl.num_programs(0)
    nj = pl.num_programs(1)
    tm = xb_ref.shape[0]
    nc = xb_ref.shape[1] // LANES
    valid = tv_ref[r] != 0
    has_next = r + 1 < n_tiles

    def row_copy(tile, row):
        slot = tile % 2
        src = pl.multiple_of(src_ref[tile * tm + row] * nc, nc)
        return pltpu.make_async_copy(x_hbm.at[pl.ds(src, nc)],
                                     xbuf.at[slot, pl.ds(row * nc, nc)], sems.at[slot])

    def swiglu(x):
        a = _silu(_dot(x, wg_ref[...])) * _dot(x, wu_ref[...])
        return _dot(a.astype(BF16), wd_ref[...])

    def swiglu_step(first, last):
        if not first:
            y = acc_ref[...] + swiglu(xb_ref[...])
            if last:
                _store_rows_as_tiles(y_ref, y)
            else:
                acc_ref[...] = y
            return
        half = tm // 2
        for hlf in range(2):
            rows = slice(hlf * half, (hlf + 1) * half)
            x = _load_rows_from_tiles(xbuf.at[r % 2, pl.ds(hlf * half * nc, half * nc)], nc)
            x = x.astype(BF16)
            xb_ref[rows] = x
            acc_ref[rows] = swiglu(x)

    @pl.when((r == 0) & (j == 0))
    def _():
        lax.fori_loop(0, tm, lambda i, c: (row_copy(r, i).start(), c)[1], 0)

    @pl.when((j == 0) & ((r == 0) | (tv_ref[jnp.maximum(r - 1, 0)] != 0)))
    def _():
        pltpu.make_async_copy(x_hbm.at[pl.ds(0, tm * nc)], xbuf.at[r % 2],
                              sems.at[r % 2]).wait()

    @pl.when(valid & (j == 0))
    def _():
        swiglu_step(True, False)

    part = tm // (nj - 1)
    for last in (False, True):
        at_step = (j == nj - 1) if last else ((j > 0) & (j < nj - 1))

        @pl.when(valid & has_next & at_step)
        def _():
            for i in range(part):
                row_copy(r + 1, (j - 1) * part + i).start(priority=i % DMA_THREADS)
            swiglu_step(False, last)

        @pl.when(valid & jnp.logical_not(has_next) & at_step)
        def _():
            swiglu_step(False, last)

    @pl.when(jnp.logical_not(valid) & (j == nj - 1))
    def _():
        y_ref[...] = jnp.zeros_like(y_ref)


def _experts(src_rows, tile_expert, tile_valid, x, w_gate, w_up, w_down, *, tm, tf):
    n_rows = src_rows.shape[0]
    d = w_gate.shape[1]
    nc = d // LANES
    f = w_gate.shape[2]
    nj = f // tf
    col = lambda r, j, tv: jnp.where(tv[r] != 0, j, nj - 1)
    return pl.pallas_call(
        _expert_kernel,
        grid_spec=pltpu.PrefetchScalarGridSpec(
            num_scalar_prefetch=3,
            grid=(n_rows // tm, nj),
            in_specs=[
                pl.BlockSpec(memory_space=pl.ANY),
                pl.BlockSpec((None, d, tf), lambda r, j, s, te, tv: (te[r], 0, col(r, j, tv))),
                pl.BlockSpec((None, d, tf), lambda r, j, s, te, tv: (te[r], 0, col(r, j, tv))),
                pl.BlockSpec((None, tf, d), lambda r, j, s, te, tv: (te[r], col(r, j, tv), 0)),
            ],
            out_specs=pl.BlockSpec((tm * nc, LANES), lambda r, j, s, te, tv: (r, 0)),
            scratch_shapes=[pltpu.VMEM((2, tm * nc, LANES), F32), pltpu.VMEM((tm, d), BF16),
                            pltpu.VMEM((tm, d), F32), pltpu.SemaphoreType.DMA((2,))],
        ),
        out_shape=jax.ShapeDtypeStruct((n_rows * nc, LANES), F32),
        compiler_params=pltpu.CompilerParams(
            dimension_semantics=("arbitrary", "arbitrary"), vmem_limit_bytes=VMEM_LIMIT),
        name="moe_experts",
    )(src_rows, tile_expert, tile_valid, x, w_gate, w_up, w_down)


def _combine_kernel(d1_ref, d2_ref, y_hbm, h_ref, meta_ref, fg_ref, out_ref, a_ref, b_ref, sem):
    tc = h_ref.shape[0]
    base = pl.program_id(0) * tc

    nc = h_ref.shape[1] // LANES

    def issue(r, _):
        for thread, (d_ref, buf) in enumerate(((d1_ref, a_ref), (d2_ref, b_ref))):
            src = pl.multiple_of(d_ref[base + r] * nc, nc)
            pltpu.make_async_copy(y_hbm.at[pl.ds(src, nc)], buf.at[pl.ds(r * nc, nc)],
                                  sem).start(priority=thread % DMA_THREADS)
        return 0

    lax.fori_loop(0, tc, issue, 0, unroll=16)
    pltpu.make_async_copy(y_hbm.at[pl.ds(0, tc * nc)], a_ref, sem).wait()
    pltpu.make_async_copy(y_hbm.at[pl.ds(0, tc * nc)], b_ref, sem).wait()
    meta = meta_ref[...]
    w1 = meta[:, 2:3]
    w2 = meta[:, 3:4]
    moe = w1 * _load_rows_from_tiles(a_ref, nc) + w2 * _load_rows_from_tiles(b_ref, nc)
    out_ref[...] = _rms(h_ref[...] + moe, fg_ref[...])


def _combine(dest1, dest2, y, h2d, meta, fg, *, tc):
    t, d = h2d.shape
    return pl.pallas_call(
        _combine_kernel,
        grid_spec=pltpu.PrefetchScalarGridSpec(
            num_scalar_prefetch=2,
            grid=(t // tc,),
            in_specs=[
                pl.BlockSpec(memory_space=pl.ANY),
                pl.BlockSpec((tc, d), lambda i, d1, d2: (i, 0)),
                pl.BlockSpec((tc, LANES), lambda i, d1, d2: (i, 0)),
                pl.BlockSpec((1, d), lambda i, d1, d2: (0, 0)),
            ],
            out_specs=pl.BlockSpec((tc, d), lambda i, d1, d2: (i, 0)),
            scratch_shapes=[pltpu.VMEM((tc * d // LANES, LANES), F32),
                            pltpu.VMEM((tc * d // LANES, LANES), F32),
                            pltpu.SemaphoreType.DMA(())],
        ),
        out_shape=jax.ShapeDtypeStruct((t, d), F32),
        compiler_params=pltpu.CompilerParams(
            dimension_semantics=("arbitrary",), vmem_limit_bytes=VMEM_LIMIT),
        name="moe_combine",
    )(dest1, dest2, y, h2d, meta, fg)


def _routing_plan(meta_t, counts, *, tm, n_tiles):
    e1 = meta_t[0].astype(jnp.int32)
    e2 = meta_t[1].astype(jnp.int32)
    cnt = counts[0, :N_EXPERTS].astype(jnp.int32)
    padded = (cnt + tm - 1) // tm * tm
    ends = jnp.cumsum(padded)
    offs = ends - padded
    dest1 = offs[e1] + meta_t[4].astype(jnp.int32)
    dest2 = offs[e2] + meta_t[5].astype(jnp.int32)
    start = jnp.arange(n_tiles, dtype=jnp.int32) * tm
    tile_valid = (start < ends[-1]).astype(jnp.int32)
    last = jnp.sum((ends < ends[-1]).astype(jnp.int32))
    passed = jnp.sum((ends[None, :] <= start[:, None]).astype(jnp.int32), axis=1)
    src_rows = _source_rows(dest1, dest2, n_tiles * tm)
    return dest1, dest2, src_rows, jnp.minimum(passed, last), tile_valid


def _source_rows_kernel(d1_ref, d2_ref, src_ref):
    def clear(i, _):
        src_ref[i] = 0
        return 0

    def place(t, _):
        src_ref[d1_ref[t]] = t
        src_ref[d2_ref[t]] = t
        return 0

    lax.fori_loop(0, src_ref.shape[0], clear, 0, unroll=16)
    lax.fori_loop(0, d1_ref.shape[0], place, 0, unroll=8)


def _source_rows(dest1, dest2, n_rows):
    return pl.pallas_call(
        _source_rows_kernel,
        in_specs=[pl.BlockSpec(memory_space=pltpu.SMEM), pl.BlockSpec(memory_space=pltpu.SMEM)],
        out_specs=pl.BlockSpec(memory_space=pltpu.SMEM),
        out_shape=jax.ShapeDtypeStruct((n_rows,), jnp.int32),
        name="moe_source_rows",
    )(dest1, dest2)


def _pad_heads(w, head_dim):
    k = w.shape[0]
    w = w.reshape(k, MLA_HEADS, head_dim)
    w = jnp.pad(w, ((0, 0), (0, 0), (0, LANES - head_dim)))
    return w.reshape(k, MLA_HEADS * LANES)


def _pick(n, cap):
    t = min(n, cap)
    while n % t:
        t //= 2
    return t


def kernel(x, positions, mix_norm, ffn_norm, final_norm, mla_w_in, mla_q_norm, mla_w_q_up,
           mla_kv_norm, mla_w_kv_up, mla_w_o, hgrn_w_in, hgrn_lower_bounds, hgrn_out_norm,
           hgrn_w_o, ffn_w_gate, ffn_w_up, ffn_w_down, moe_router, moe_w_gate, moe_w_up,
           moe_w_down):
    batch, seq, d = x.shape
    t = batch * seq
    x2d = x.reshape(t, d)
    row = lambda v: v.reshape(1, -1)

    w_in = mla_w_in[0]
    n_lat = MLA_Q_LORA + MLA_KV_LORA
    w_in_p = w_in[:, :n_lat].astype(BF16)
    wkr_t = jnp.pad(w_in[:, n_lat:].T,
                    ((MLA_NOPE, LANES - MLA_NOPE - MLA_ROPE), (0, 0))).astype(BF16)
    wq_t = _pad_heads(mla_w_q_up[0], MLA_NOPE + MLA_ROPE).T.astype(BF16)
    w_kv = mla_w_kv_up[0].reshape(MLA_KV_LORA, MLA_HEADS, MLA_NOPE + MLA_V)
    wk_p = _pad_heads(w_kv[:, :, :MLA_NOPE].reshape(MLA_KV_LORA, -1), MLA_NOPE).astype(BF16)
    wv_t = w_kv[:, :, MLA_NOPE:].reshape(MLA_KV_LORA, -1).T.astype(BF16)
    inv_freq = ROPE_THETA ** (-jnp.arange(0, MLA_ROPE, 2, dtype=F32) / MLA_ROPE)
    freq = inv_freq.reshape(MLA_ROPE // 2, 1)
    pos = positions.astype(F32).reshape(1, t)

    q_t, k, v_t = _mla_proj(x2d, pos, row(mix_norm[0]), w_in_p, wkr_t, row(mla_q_norm[0]), wq_t,
                            row(mla_kv_norm[0]), wk_p, wv_t, freq,
                            batch=batch, seq=seq, tm=_pick(seq, 512))
    o = _mla_attn(q_t, k, v_t, tq=_pick(seq, 512), hp=8).reshape(t, MLA_HEADS * MLA_V)
    h = _attn_ffn(x2d, o, mla_w_o[0].astype(BF16), row(ffn_norm[0]),
                  ffn_w_gate[0].astype(BF16), ffn_w_up[0].astype(BF16),
                  ffn_w_down[0].astype(BF16), tm=_pick(t, 512), tf=1792)

    router_p = jnp.pad(moe_router[0], ((0, 0), (0, LANES - N_EXPERTS)))
    h, hn, meta, meta_t, counts = _hgrn(h, row(mix_norm[1]), hgrn_w_in[0].astype(BF16),
                                hgrn_lower_bounds, row(hgrn_out_norm[0]),
                                hgrn_w_o[0].astype(BF16), row(ffn_norm[1]), router_p,
                                batch=batch, seq=seq, tt=_pick(seq, 256), layer=1)
    tm = _pick(t, 512)
    n_tiles = 2 * t // tm + N_EXPERTS
    dest1, dest2, src_rows, tile_expert, tile_valid = _routing_plan(
        meta_t, counts, tm=tm, n_tiles=n_tiles)
    y = _experts(src_rows, tile_expert, tile_valid, hn, moe_w_gate[0].astype(BF16),
                 moe_w_up[0].astype(BF16), moe_w_down[0].astype(BF16), tm=tm, tf=1792)
    out = _combine(dest1, dest2, y, h, meta, row(final_norm), tc=_pick(t, 1024))
    return out.reshape(batch, seq, d)
```

```python
import functools

import jax
import jax.numpy as jnp
from jax import lax
from jax.experimental import pallas as pl
from jax.experimental.pallas import tpu as pltpu

EPS = 1e-6
LANES = 128
SUBLANES = 8

MLA_HEADS = 16
MLA_Q_LORA = 384
MLA_KV_LORA = 256
MLA_NOPE = 64
MLA_ROPE = 32
MLA_V = 64
ROPE_THETA = 10000.0

HG_HEADS = 8
HG_DK = 128
HG_CHUNK = 64

N_EXPERTS = 8

VMEM_LIMIT = 56 * 1024 * 1024
DMA_THREADS = 2

BF16 = jnp.bfloat16
F32 = jnp.float32


def _dot(a, b):
    return jnp.dot(a, b, preferred_element_type=F32)


def _dot_nt(a, b):
    return lax.dot_general(a, b, (((1,), (1,)), ((), ())), preferred_element_type=F32)


def _dot_tn(a, b):
    return lax.dot_general(a, b, (((0,), (0,)), ((), ())), preferred_element_type=F32)


def _rms(x, g):
    return x * lax.rsqrt(jnp.mean(x * x, axis=-1, keepdims=True) + EPS) * g


def _silu(x):
    return x * (1.0 / (1.0 + jnp.exp(-x)))


def _store_rows_as_tiles(ref, x):
    n, d = x.shape
    nc = d // LANES
    for c in range(nc):
        ref[pl.ds(c, n, stride=nc), :] = x[:, c * LANES:(c + 1) * LANES]


def _load_rows_from_tiles(ref, nc):
    n = ref.shape[0] // nc
    return jnp.concatenate([ref[pl.ds(c, n, stride=nc), :] for c in range(nc)], axis=1)


def _split3(x):
    hi = x.astype(BF16)
    r = x - hi.astype(F32)
    mid = r.astype(BF16)
    lo = (r - mid.astype(F32)).astype(BF16)
    return hi, mid, lo


def _mla_proj_kernel(x_ref, pos_ref, g_ref, win_ref, wkr_ref, qg_ref, wq_ref, kvg_ref, wk_ref,
                     wv_ref, freq_ref, q_out, k_out, v_out, *, scale):
    half = MLA_ROPE // 2
    x1_rows = slice(MLA_NOPE, MLA_NOPE + half)
    x2_rows = slice(MLA_NOPE + half, MLA_NOPE + MLA_ROPE)
    hn = _rms(x_ref[...], g_ref[...]).astype(BF16)
    proj = _dot(hn, win_ref[...])
    cqn = _rms(proj[:, :MLA_Q_LORA], qg_ref[...]).astype(BF16)
    ckvn = _rms(proj[:, MLA_Q_LORA:], kvg_ref[...]).astype(BF16)

    ang = freq_ref[...] * pos_ref[...]
    cos = jnp.cos(ang)
    sin = jnp.sin(ang)

    def rope_t(slot, mul):
        x1 = slot[x1_rows]
        x2 = slot[x2_rows]
        return jnp.concatenate([slot[:MLA_NOPE] * mul, (x1 * cos - x2 * sin) * mul,
                                (x2 * cos + x1 * sin) * mul, slot[MLA_NOPE + MLA_ROPE:]], axis=0)

    kr_t = _dot_nt(wkr_ref[...], hn)
    kr_roped = rope_t(kr_t, 1.0).T
    q_t = _dot_nt(wq_ref[...], cqn)
    k_pad = _dot(ckvn, wk_ref[...])
    for h in range(MLA_HEADS):
        sl = slice(h * LANES, (h + 1) * LANES)
        q_out[h] = rope_t(q_t[sl], scale).astype(BF16)
        k_out[h] = (k_pad[:, sl] + kr_roped).astype(BF16)
    v_out[...] = _dot_nt(wv_ref[...], ckvn).astype(BF16)


def _mla_proj(x2d, pos, g, w_in_p, wkr_t, qg, wq_t, kvg, wk_p, wv_t, freq, *, batch, seq, tm):
    t = x2d.shape[0]
    d = x2d.shape[1]
    nt = seq // tm
    const = lambda shape: pl.BlockSpec(shape, lambda i: (0,) * len(shape))
    scale = (MLA_NOPE + MLA_ROPE) ** -0.5 * 1.4426950408889634
    return pl.pallas_call(
        functools.partial(_mla_proj_kernel, scale=scale),
        grid=(t // tm,),
        in_specs=[
            pl.BlockSpec((tm, d), lambda i: (i, 0)),
            pl.BlockSpec((1, tm), lambda i: (0, i)),
            const((1, d)),
            const(w_in_p.shape),
            const(wkr_t.shape),
            const((1, MLA_Q_LORA)),
            const(wq_t.shape),
            const((1, MLA_KV_LORA)),
            const(wk_p.shape),
            const(wv_t.shape),
            const(freq.shape),
        ],
        out_specs=[
            pl.BlockSpec((None, MLA_HEADS, LANES, tm), lambda i: (i // nt, 0, 0, i % nt)),
            pl.BlockSpec((None, MLA_HEADS, tm, LANES), lambda i: (i // nt, 0, i % nt, 0)),
            pl.BlockSpec((None, MLA_HEADS * MLA_V, tm), lambda i: (i // nt, 0, i % nt)),
        ],
        out_shape=[
            jax.ShapeDtypeStruct((batch, MLA_HEADS, LANES, seq), BF16),
            jax.ShapeDtypeStruct((batch, MLA_HEADS, seq, LANES), BF16),
            jax.ShapeDtypeStruct((batch, MLA_HEADS * MLA_V, seq), BF16),
        ],
        compiler_params=pltpu.CompilerParams(
            dimension_semantics=("parallel",), vmem_limit_bytes=VMEM_LIMIT),
        name="mla_proj",
    )(x2d, pos, g, w_in_p, wkr_t, qg, wq_t, kvg, wk_p, wv_t, freq)


def _attn_kernel(q_ref, k_ref, v_ref, o_ref, acc_ref, m_ref, vx_ref, sa_ref, sb_ref):
    hp, _, tq = q_ref.shape
    seq = k_ref.shape[1]
    qi = pl.program_id(2)

    @pl.when(qi == 0)
    def _():
        for h in range(hp):
            vx_ref[h, :MLA_V] = v_ref[h * MLA_V:(h + 1) * MLA_V]
            vx_ref[h, MLA_V:] = jnp.ones((vx_ref.shape[1] - MLA_V, seq), BF16)

    causal = (lax.broadcasted_iota(jnp.int32, (tq, tq), 0)
              <= lax.broadcasted_iota(jnp.int32, (tq, tq), 1))

    def scores(dst, blk):
        k0 = pl.multiple_of(blk * tq, tq)
        for h in range(hp):
            dst[h] = _dot(k_ref[h, pl.ds(k0, tq), :], q_ref[h])

    def softmax_pv(src, blk, masked):
        k0 = pl.multiple_of(blk * tq, tq)
        for h in range(hp):
            s = src[h]
            if masked:
                s = jnp.where(causal, s, -jnp.inf)
            m = m_ref[h]
            m_new = jnp.maximum(m, jnp.max(s, axis=0, keepdims=True))
            p = jnp.exp2(s - m_new)
            alpha = jnp.exp2(m - m_new)
            m_ref[h] = m_new
            acc_ref[h] = alpha * acc_ref[h] + _dot(vx_ref[h, :, pl.ds(k0, tq)], p.astype(BF16))

    acc_ref[...] = jnp.zeros_like(acc_ref)
    m_ref[...] = jnp.full_like(m_ref, -jnp.inf)

    scores(sa_ref, 0)

    def pair(jp, _):
        scores(sb_ref, 2 * jp + 1)
        softmax_pv(sa_ref, 2 * jp, False)
        scores(sa_ref, 2 * jp + 2)
        softmax_pv(sb_ref, 2 * jp + 1, False)
        return 0

    lax.fori_loop(0, qi // 2, pair, 0)

    @pl.when(qi % 2 == 1)
    def _():
        scores(sb_ref, qi)
        softmax_pv(sa_ref, qi - 1, False)
        softmax_pv(sb_ref, qi, True)

    @pl.when(qi % 2 == 0)
    def _():
        softmax_pv(sa_ref, qi, True)

    out = jnp.concatenate([acc_ref[h, :MLA_V] / acc_ref[h, MLA_V:MLA_V + 1]
                           for h in range(hp)], axis=0)
    o_ref[...] = out.T.astype(BF16)


def _mla_attn(q_t, k, v_t, *, tq, hp):
    batch, heads, seq, _ = k.shape
    return pl.pallas_call(
        _attn_kernel,
        grid=(batch, heads // hp, seq // tq),
        in_specs=[
            pl.BlockSpec((None, hp, LANES, tq), lambda b, g, i: (b, g, 0, i)),
            pl.BlockSpec((None, hp, seq, LANES), lambda b, g, i: (b, g, 0, 0)),
            pl.BlockSpec((None, hp * MLA_V, seq), lambda b, g, i: (b, g, 0)),
        ],
        out_specs=pl.BlockSpec((None, tq, hp * MLA_V), lambda b, g, i: (b, i, g)),
        out_shape=jax.ShapeDtypeStruct((batch, seq, heads * MLA_V), BF16),
        scratch_shapes=[pltpu.VMEM((hp, MLA_V + 16, tq), F32),
                        pltpu.VMEM((hp, 1, tq), F32), pltpu.VMEM((hp, MLA_V + 16, seq), BF16),
                        pltpu.VMEM((hp, tq, tq), F32), pltpu.VMEM((hp, tq, tq), F32)],
        compiler_params=pltpu.CompilerParams(
            dimension_semantics=("parallel", "parallel", "arbitrary"),
            vmem_limit_bytes=VMEM_LIMIT),
        name="mla_attn",
    )(q_t, k, v_t)


def _attn_ffn_kernel(x_ref, o_ref, wo_ref, g_ref, wg_ref, wu_ref, wd_ref, out_ref, hn_ref):
    j = pl.program_id(1)

    @pl.when(j == 0)
    def _():
        h1 = x_ref[...] + _dot(o_ref[...], wo_ref[...])
        out_ref[...] = h1
        hn_ref[...] = _rms(h1, g_ref[...]).astype(BF16)

    hn = hn_ref[...]
    a = _silu(_dot(hn, wg_ref[...])) * _dot(hn, wu_ref[...])
    out_ref[...] += _dot(a.astype(BF16), wd_ref[...])


def _attn_ffn(x2d, o, w_o, g, w_gate, w_up, w_down, *, tm, tf):
    t, d = x2d.shape
    f = w_gate.shape[1]
    return pl.pallas_call(
        _attn_ffn_kernel,
        grid=(t // tm, f // tf),
        in_specs=[
            pl.BlockSpec((tm, d), lambda i, j: (i, 0)),
            pl.BlockSpec((tm, o.shape[1]), lambda i, j: (i, 0)),
            pl.BlockSpec(w_o.shape, lambda i, j: (0, 0)),
            pl.BlockSpec((1, d), lambda i, j: (0, 0)),
            pl.BlockSpec((d, tf), lambda i, j: (0, j)),
            pl.BlockSpec((d, tf), lambda i, j: (0, j)),
            pl.BlockSpec((tf, d), lambda i, j: (j, 0)),
        ],
        out_specs=pl.BlockSpec((tm, d), lambda i, j: (i, 0)),
        out_shape=jax.ShapeDtypeStruct((t, d), F32),
        scratch_shapes=[pltpu.VMEM((tm, d), BF16)],
        compiler_params=pltpu.CompilerParams(
            dimension_semantics=("parallel", "arbitrary"), vmem_limit_bytes=VMEM_LIMIT),
        name="attn_ffn",
    )(x2d, o, w_o, g, w_gate, w_up, w_down)


def _hgrn_kernel(h_ref, g_ref, win_ref, lbraw_ref, og_ref, wo_ref, fg_ref, router_ref,
                 h_out, hn_out, meta_out, meta_t_out, cnt_out, st_ref, o_scr, cnt_ref, *, layer):
    tt = h_ref.shape[0]
    width = HG_HEADS * HG_DK
    nchunk = tt // HG_CHUNK

    @pl.when(pl.program_id(1) == 0)
    def _():
        st_ref[...] = jnp.zeros_like(st_ref)

    lbraw = lbraw_ref[...]
    e = jnp.exp(lbraw - jnp.max(lbraw, axis=0, keepdims=True))
    sm = e / jnp.sum(e, axis=0, keepdims=True)
    lb = jnp.sum(sm[1:layer + 1], axis=0, keepdims=True)

    h_in = h_ref[...]
    hn = _rms(h_in, g_ref[...]).astype(BF16)
    proj = _dot(hn, win_ref[...])
    q_all = proj[:, :width] * (HG_DK ** -0.5)
    f_all = lb + (1.0 - lb) * (1.0 / (1.0 + jnp.exp(-proj[:, width:2 * width])))
    logf = jnp.log(f_all)
    k_all = 1.0 - f_all

    r = lax.broadcasted_iota(jnp.int32, (tt, tt), 0)
    c = lax.broadcasted_iota(jnp.int32, (tt, tt), 1)
    shift = HG_CHUNK.bit_length() - 1
    tril = (c <= r) & ((r >> shift) == (c >> shift))
    in_chunk = lax.broadcasted_iota(jnp.int32, (tt, 1), 0) & (HG_CHUNK - 1)
    b_all = logf
    for step in (1 << i for i in range(shift)):
        b_all = b_all + jnp.where(in_chunk >= step, pltpu.roll(b_all, step, 0), 0.0)

    for h in range(HG_HEADS):
        sl = slice(h * HG_DK, (h + 1) * HG_DK)
        q = q_all[:, sl]
        k = k_all[:, sl]
        b = b_all[:, sl]
        v = proj[:, 2 * width + h * HG_DK:2 * width + (h + 1) * HG_DK]
        gt = proj[:, 3 * width + h * HG_DK:3 * width + (h + 1) * HG_DK]
        v_b = v.astype(BF16)
        qd = (q * jnp.exp(b)).astype(BF16)
        kd = (k * jnp.exp(-b)).astype(BF16)
        a = jnp.where(tril, _dot_nt(qd, kd), 0.0)
        o = _dot(a.astype(BF16), v_b)

        st = st_ref[h]
        inter = []
        for n in range(nchunk):
            rows = slice(n * HG_CHUNK, (n + 1) * HG_CHUNK)
            b_n = b[rows]
            b_last = b_n[HG_CHUNK - 1:HG_CHUNK]
            inter.append(_dot_nt(qd[rows], st.astype(BF16)))
            kdl = (k[rows] * jnp.exp(b_last - b_n)).astype(BF16)
            st = st * jnp.exp(b_last) + _dot_tn(v_b[rows], kdl)
        st_ref[h] = st
        o = o + jnp.concatenate(inter, axis=0)
        o = _rms(o, og_ref[...]) * _silu(gt)
        o_scr[:, sl] = o.astype(BF16)

    h_new = h_in + _dot(o_scr[...], wo_ref[...])
    h_out[...] = h_new
    hn2 = _rms(h_new, fg_ref[...])
    _store_rows_as_tiles(hn_out, hn2)

    x_hi, x_mid, _ = _split3(hn2)
    router = router_ref[...]
    r_hi = router.astype(BF16)
    r_lo = (router - r_hi.astype(F32)).astype(BF16)
    hi_terms = _dot(x_hi, jnp.concatenate([r_hi, r_lo], axis=1))
    logits = hi_terms[:, :LANES] + _dot(x_mid, r_hi) + hi_terms[:, LANES:]
    lane = lax.broadcasted_iota(jnp.int32, logits.shape, 1)
    neg = -jnp.inf
    logits = jnp.where(lane < N_EXPERTS, logits, neg)
    m1 = jnp.max(logits, axis=-1, keepdims=True)
    i1 = jnp.min(jnp.where(logits == m1, lane, LANES), axis=-1, keepdims=True)
    rest = jnp.where(lane == i1, neg, logits)
    m2 = jnp.max(rest, axis=-1, keepdims=True)
    i2 = jnp.min(jnp.where(rest == m2, lane, LANES), axis=-1, keepdims=True)
    e2 = jnp.exp(m2 - m1)
    w1 = 1.0 / (1.0 + e2)
    w2 = e2 / (1.0 + e2)

    @pl.when((pl.program_id(0) == 0) & (pl.program_id(1) == 0))
    def _():
        cnt_ref[...] = jnp.zeros_like(cnt_ref)

    onehot = jnp.where((lane == i1) | (lane == i2), 1.0, 0.0)
    earlier = jnp.where(c < r, 1.0, 0.0).astype(BF16)
    rank = _dot(earlier, onehot.astype(BF16)) + cnt_ref[...]
    cnt = cnt_ref[...] + jnp.sum(onehot, axis=0, keepdims=True)
    cnt_ref[...] = cnt
    cnt_out[...] = jnp.broadcast_to(cnt, cnt_out.shape)
    rank1 = jnp.sum(jnp.where(lane == i1, rank, 0.0), axis=-1, keepdims=True)
    rank2 = jnp.sum(jnp.where(lane == i2, rank, 0.0), axis=-1, keepdims=True)
    fields = (i1.astype(F32), i2.astype(F32), w1, w2, rank1, rank2)
    meta = jnp.zeros(logits.shape, F32)
    for idx, val in enumerate(fields):
        meta = jnp.where(lane == idx, val, meta)
    meta_out[...] = meta
    meta_t_out[...] = meta.T[:SUBLANES]


def _hgrn(h2d, g, w_in, lb_raw, og, w_o, fg, router_p, *, batch, seq, tt, layer):
    t, d = h2d.shape
    nt = seq // tt
    width = HG_HEADS * HG_DK
    tok = lambda cols: pl.BlockSpec((tt, cols), lambda b, i: (b * nt + i, 0))
    const = lambda shape: pl.BlockSpec(shape, lambda b, i: (0,) * len(shape))
    return pl.pallas_call(
        functools.partial(_hgrn_kernel, layer=layer),
        grid=(batch, nt),
        in_specs=[tok(d), const((1, d)), const(w_in.shape), const(lb_raw.shape),
                  const((1, HG_DK)), const(w_o.shape), const((1, d)), const(router_p.shape)],
        out_specs=[tok(d),
                   pl.BlockSpec((tt * (d // LANES), LANES), lambda b, i: (b * nt + i, 0)),
                   tok(LANES), pl.BlockSpec((SUBLANES, tt), lambda b, i: (0, b * nt + i)),
                   const((SUBLANES, LANES))],
        out_shape=[jax.ShapeDtypeStruct((t, d), F32),
                   jax.ShapeDtypeStruct((t * (d // LANES), LANES), F32),
                   jax.ShapeDtypeStruct((t, LANES), F32),
                   jax.ShapeDtypeStruct((SUBLANES, t), F32),
                   jax.ShapeDtypeStruct((SUBLANES, LANES), F32)],
        scratch_shapes=[pltpu.VMEM((HG_HEADS, HG_DK, HG_DK), F32),
                        pltpu.VMEM((tt, width), BF16),
                        pltpu.VMEM((1, LANES), F32)],
        compiler_params=pltpu.CompilerParams(
            dimension_semantics=("arbitrary", "arbitrary"), vmem_limit_bytes=VMEM_LIMIT),
        name="hgrn",
    )(h2d, g, w_in, lb_raw, og, w_o, fg, router_p)


def _expert_kernel(src_ref, te_ref, tv_ref, x_hbm, wg_ref, wu_ref, wd_ref, y_ref,
                   xbuf, xb_ref, acc_ref, sems):
    del te_ref
    r = pl.program_id(0)
    j = pl.program_id(1)
    n_tiles = pl.num_programs(0)
    nj = pl.num_programs(1)
    tm = xb_ref.shape[0]
    nc = xb_ref.shape[1] // LANES
    valid = tv_ref[r] != 0
    has_next = r + 1 < n_tiles

    def row_copy(tile, row):
        slot = tile % 2
        src = pl.multiple_of(src_ref[tile * tm + row] * nc, nc)
        return pltpu.make_async_copy(x_hbm.at[pl.ds(src, nc)],
                                     xbuf.at[slot, pl.ds(row * nc, nc)], sems.at[slot])

    def swiglu(x):
        a = _silu(_dot(x, wg_ref[...])) * _dot(x, wu_ref[...])
        return _dot(a.astype(BF16), wd_ref[...])

    def swiglu_step(first):
        if not first:
            acc_ref[...] += swiglu(xb_ref[...])
            return
        half = tm // 2
        for hlf in range(2):
            rows = slice(hlf * half, (hlf + 1) * half)
            x = _load_rows_from_tiles(xbuf.at[r % 2, pl.ds(hlf * half * nc, half * nc)], nc)
            x = x.astype(BF16)
            xb_ref[rows] = x
            acc_ref[rows] = swiglu(x)

    @pl.when((r == 0) & (j == 0))
    def _():
        lax.fori_loop(0, tm, lambda i, c: (row_copy(r, i).start(), c)[1], 0)

    @pl.when((j == 0) & ((r == 0) | (tv_ref[jnp.maximum(r - 1, 0)] != 0)))
    def _():
        pltpu.make_async_copy(x_hbm.at[pl.ds(0, tm * nc)], xbuf.at[r % 2],
                              sems.at[r % 2]).wait()

    part = tm // (nj - 1)

    @pl.when(valid & (j == 0))
    def _():
        swiglu_step(True)

    @pl.when(valid & has_next & (j > 0))
    def _():
        for i in range(part):
            row_copy(r + 1, (j - 1) * part + i).start(priority=i % DMA_THREADS)
        swiglu_step(False)

    @pl.when(valid & jnp.logical_not(has_next) & (j > 0))
    def _():
        swiglu_step(False)

    @pl.when(valid & (j == nj - 1))
    def _():
        _store_rows_as_tiles(y_ref, acc_ref[...])

    @pl.when(jnp.logical_not(valid) & (j == nj - 1))
    def _():
        y_ref[...] = jnp.zeros_like(y_ref)


def _experts(src_rows, tile_expert, tile_valid, x, w_gate, w_up, w_down, *, tm, tf):
    n_rows = src_rows.shape[0]
    d = w_gate.shape[1]
    nc = d // LANES
    f = w_gate.shape[2]
    nj = f // tf
    col = lambda r, j, tv: jnp.where(tv[r] != 0, j, nj - 1)
    return pl.pallas_call(
        _expert_kernel,
        grid_spec=pltpu.PrefetchScalarGridSpec(
            num_scalar_prefetch=3,
            grid=(n_rows // tm, nj),
            in_specs=[
                pl.BlockSpec(memory_space=pl.ANY),
                pl.BlockSpec((None, d, tf), lambda r, j, s, te, tv: (te[r], 0, col(r, j, tv))),
                pl.BlockSpec((None, d, tf), lambda r, j, s, te, tv: (te[r], 0, col(r, j, tv))),
                pl.BlockSpec((None, tf, d), lambda r, j, s, te, tv: (te[r], col(r, j, tv), 0)),
            ],
            out_specs=pl.BlockSpec((tm * nc, LANES), lambda r, j, s, te, tv: (r, 0)),
            scratch_shapes=[pltpu.VMEM((2, tm * nc, LANES), F32), pltpu.VMEM((tm, d), BF16),
                            pltpu.VMEM((tm, d), F32), pltpu.SemaphoreType.DMA((2,))],
        ),
        out_shape=jax.ShapeDtypeStruct((n_rows * nc, LANES), F32),
        compiler_params=pltpu.CompilerParams(
            dimension_semantics=("arbitrary", "arbitrary"), vmem_limit_bytes=VMEM_LIMIT),
        name="moe_experts",
    )(src_rows, tile_expert, tile_valid, x, w_gate, w_up, w_down)


def _combine_kernel(d1_ref, d2_ref, y_hbm, h_ref, meta_ref, fg_ref, out_ref, a_ref, b_ref, sem):
    tc = h_ref.shape[0]
    base = pl.program_id(0) * tc

    nc = h_ref.shape[1] // LANES

    def issue(r, _):
        for thread, (d_ref, buf) in enumerate(((d1_ref, a_ref), (d2_ref, b_ref))):
            src = pl.multiple_of(d_ref[base + r] * nc, nc)
            pltpu.make_async_copy(y_hbm.at[pl.ds(src, nc)], buf.at[pl.ds(r * nc, nc)],
                                  sem).start(priority=thread % DMA_THREADS)
        return 0

    lax.fori_loop(0, tc, issue, 0, unroll=16)
    pltpu.make_async_copy(y_hbm.at[pl.ds(0, tc * nc)], a_ref, sem).wait()
    pltpu.make_async_copy(y_hbm.at[pl.ds(0, tc * nc)], b_ref, sem).wait()
    meta = meta_ref[...]
    w1 = meta[:, 2:3]
    w2 = meta[:, 3:4]
    moe = w1 * _load_rows_from_tiles(a_ref, nc) + w2 * _load_rows_from_tiles(b_ref, nc)
    out_ref[...] = _rms(h_ref[...] + moe, fg_ref[...])


def _combine(dest1, dest2, y, h2d, meta, fg, *, tc):
    t, d = h2d.shape
    return pl.pallas_call(
        _combine_kernel,
        grid_spec=pltpu.PrefetchScalarGridSpec(
            num_scalar_prefetch=2,
            grid=(t // tc,),
            in_specs=[
                pl.BlockSpec(memory_space=pl.ANY),
                pl.BlockSpec((tc, d), lambda i, d1, d2: (i, 0)),
                pl.BlockSpec((tc, LANES), lambda i, d1, d2: (i, 0)),
                pl.BlockSpec((1, d), lambda i, d1, d2: (0, 0)),
            ],
            out_specs=pl.BlockSpec((tc, d), lambda i, d1, d2: (i, 0)),
            scratch_shapes=[pltpu.VMEM((tc * d // LANES, LANES), F32),
                            pltpu.VMEM((tc * d // LANES, LANES), F32),
                            pltpu.SemaphoreType.DMA(())],
        ),
        out_shape=jax.ShapeDtypeStruct((t, d), F32),
        compiler_params=pltpu.CompilerParams(
            dimension_semantics=("arbitrary",), vmem_limit_bytes=VMEM_LIMIT),
        name="moe_combine",
    )(dest1, dest2, y, h2d, meta, fg)


def _routing_plan(meta_t, counts, *, tm, n_tiles):
    e1 = meta_t[0].astype(jnp.int32)
    e2 = meta_t[1].astype(jnp.int32)
    cnt = counts[0, :N_EXPERTS].astype(jnp.int32)
    padded = (cnt + tm - 1) // tm * tm
    ends = jnp.cumsum(padded)
    offs = ends - padded
    dest1 = offs[e1] + meta_t[4].astype(jnp.int32)
    dest2 = offs[e2] + meta_t[5].astype(jnp.int32)
    start = jnp.arange(n_tiles, dtype=jnp.int32) * tm
    tile_valid = (start < ends[-1]).astype(jnp.int32)
    last = jnp.sum((ends < ends[-1]).astype(jnp.int32))
    passed = jnp.sum((ends[None, :] <= start[:, None]).astype(jnp.int32), axis=1)
    src_rows = _source_rows(dest1, dest2, n_tiles * tm)
    return dest1, dest2, src_rows, jnp.minimum(passed, last), tile_valid


def _source_rows_kernel(d1_ref, d2_ref, src_ref):
    def clear(i, _):
        src_ref[i] = 0
        return 0

    def place(t, _):
        src_ref[d1_ref[t]] = t
        src_ref[d2_ref[t]] = t
        return 0

    lax.fori_loop(0, src_ref.shape[0], clear, 0, unroll=16)
    lax.fori_loop(0, d1_ref.shape[0], place, 0, unroll=8)


def _source_rows(dest1, dest2, n_rows):
    return pl.pallas_call(
        _source_rows_kernel,
        in_specs=[pl.BlockSpec(memory_space=pltpu.SMEM), pl.BlockSpec(memory_space=pltpu.SMEM)],
        out_specs=pl.BlockSpec(memory_space=pltpu.SMEM),
        out_shape=jax.ShapeDtypeStruct((n_rows,), jnp.int32),
        name="moe_source_rows",
    )(dest1, dest2)


def _pad_heads(w, head_dim):
    k = w.shape[0]
    w = w.reshape(k, MLA_HEADS, head_dim)
    w = jnp.pad(w, ((0, 0), (0, 0), (0, LANES - head_dim)))
    return w.reshape(k, MLA_HEADS * LANES)


def _pick(n, cap):
    t = min(n, cap)
    while n % t:
        t //= 2
    return t


def kernel(x, positions, mix_norm, ffn_norm, final_norm, mla_w_in, mla_q_norm, mla_w_q_up,
           mla_kv_norm, mla_w_kv_up, mla_w_o, hgrn_w_in, hgrn_lower_bounds, hgrn_out_norm,
           hgrn_w_o, ffn_w_gate, ffn_w_up, ffn_w_down, moe_router, moe_w_gate, moe_w_up,
           moe_w_down):
    batch, seq, d = x.shape
    t = batch * seq
    x2d = x.reshape(t, d)
    row = lambda v: v.reshape(1, -1)

    w_in = mla_w_in[0]
    n_lat = MLA_Q_LORA + MLA_KV_LORA
    w_in_p = w_in[:, :n_lat].astype(BF16)
    wkr_t = jnp.pad(w_in[:, n_lat:].T,
                    ((MLA_NOPE, LANES - MLA_NOPE - MLA_ROPE), (0, 0))).astype(BF16)
    wq_t = _pad_heads(mla_w_q_up[0], MLA_NOPE + MLA_ROPE).T.astype(BF16)
    w_kv = mla_w_kv_up[0].reshape(MLA_KV_LORA, MLA_HEADS, MLA_NOPE + MLA_V)
    wk_p = _pad_heads(w_kv[:, :, :MLA_NOPE].reshape(MLA_KV_LORA, -1), MLA_NOPE).astype(BF16)
    wv_t = w_kv[:, :, MLA_NOPE:].reshape(MLA_KV_LORA, -1).T.astype(BF16)
    inv_freq = ROPE_THETA ** (-jnp.arange(0, MLA_ROPE, 2, dtype=F32) / MLA_ROPE)
    freq = inv_freq.reshape(MLA_ROPE // 2, 1)
    pos = positions.astype(F32).reshape(1, t)

    q_t, k, v_t = _mla_proj(x2d, pos, row(mix_norm[0]), w_in_p, wkr_t, row(mla_q_norm[0]), wq_t,
                            row(mla_kv_norm[0]), wk_p, wv_t, freq,
                            batch=batch, seq=seq, tm=_pick(seq, 512))
    o = _mla_attn(q_t, k, v_t, tq=_pick(seq, 512), hp=8).reshape(t, MLA_HEADS * MLA_V)
    h = _attn_ffn(x2d, o, mla_w_o[0].astype(BF16), row(ffn_norm[0]),
                  ffn_w_gate[0].astype(BF16), ffn_w_up[0].astype(BF16),
                  ffn_w_down[0].astype(BF16), tm=_pick(t, 512), tf=1792)

    router_p = jnp.pad(moe_router[0], ((0, 0), (0, LANES - N_EXPERTS)))
    h, hn, meta, meta_t, counts = _hgrn(h, row(mix_norm[1]), hgrn_w_in[0].astype(BF16),
                                hgrn_lower_bounds, row(hgrn_out_norm[0]),
                                hgrn_w_o[0].astype(BF16), row(ffn_norm[1]), router_p,
                                batch=batch, seq=seq, tt=_pick(seq, 256), layer=1)
    tm = _pick(t, 512)
    n_tiles = 2 * t // tm + N_EXPERTS
    dest1, dest2, src_rows, tile_expert, tile_valid = _routing_plan(
        meta_t, counts, tm=tm, n_tiles=n_tiles)
    y = _experts(src_rows, tile_expert, tile_valid, hn, moe_w_gate[0].astype(BF16),
                 moe_w_up[0].astype(BF16), moe_w_down[0].astype(BF16), tm=tm, tf=1792)
    out = _combine(dest1, dest2, y, h, meta, row(final_norm), tc=_pick(t, 1024))
    return out.reshape(batch, seq, d)
```

```python
import functools

import jax
import jax.numpy as jnp
from jax import lax
from jax.experimental import pallas as pl
from jax.experimental.pallas import tpu as pltpu

EPS = 1e-6
LANES = 128
SUBLANES = 8

MLA_HEADS = 16
MLA_Q_LORA = 384
MLA_KV_LORA = 256
MLA_NOPE = 64
MLA_ROPE = 32
MLA_V = 64
ROPE_THETA = 10000.0

HG_HEADS = 8
HG_DK = 128
HG_CHUNK = 64

N_EXPERTS = 8

VMEM_LIMIT = 56 * 1024 * 1024
DMA_THREADS = 2

BF16 = jnp.bfloat16
F32 = jnp.float32


def _dot(a, b):
    return jnp.dot(a, b, preferred_element_type=F32)


def _dot_nt(a, b):
    return lax.dot_general(a, b, (((1,), (1,)), ((), ())), preferred_element_type=F32)


def _dot_tn(a, b):
    return lax.dot_general(a, b, (((0,), (0,)), ((), ())), preferred_element_type=F32)


def _rms(x, g):
    return x * lax.rsqrt(jnp.mean(x * x, axis=-1, keepdims=True) + EPS) * g


def _silu(x):
    return x * (1.0 / (1.0 + jnp.exp(-x)))


def _store_rows_as_tiles(ref, x):
    n, d = x.shape
    nc = d // LANES
    for c in range(nc):
        ref[pl.ds(c, n, stride=nc), :] = x[:, c * LANES:(c + 1) * LANES]


def _load_rows_from_tiles(ref, nc):
    n = ref.shape[0] // nc
    return jnp.concatenate([ref[pl.ds(c, n, stride=nc), :] for c in range(nc)], axis=1)


def _split3(x):
    hi = x.astype(BF16)
    r = x - hi.astype(F32)
    mid = r.astype(BF16)
    lo = (r - mid.astype(F32)).astype(BF16)
    return hi, mid, lo


def _mla_proj_kernel(x_ref, pos_ref, g_ref, win_ref, wkr_ref, qg_ref, wq_ref, kvg_ref, wk_ref,
                     wv_ref, freq_ref, q_out, k_out, v_out, *, scale):
    half = MLA_ROPE // 2
    x1_rows = slice(MLA_NOPE, MLA_NOPE + half)
    x2_rows = slice(MLA_NOPE + half, MLA_NOPE + MLA_ROPE)
    hn = _rms(x_ref[...], g_ref[...]).astype(BF16)
    proj = _dot(hn, win_ref[...])
    cqn = _rms(proj[:, :MLA_Q_LORA], qg_ref[...]).astype(BF16)
    ckvn = _rms(proj[:, MLA_Q_LORA:], kvg_ref[...]).astype(BF16)

    ang = freq_ref[...] * pos_ref[...]
    cos = jnp.cos(ang)
    sin = jnp.sin(ang)

    def rope_t(slot, mul):
        x1 = slot[x1_rows]
        x2 = slot[x2_rows]
        return jnp.concatenate([slot[:MLA_NOPE] * mul, (x1 * cos - x2 * sin) * mul,
                                (x2 * cos + x1 * sin) * mul, slot[MLA_NOPE + MLA_ROPE:]], axis=0)

    kr_t = _dot_nt(wkr_ref[...], hn)
    kr_roped = rope_t(kr_t, 1.0).T
    q_t = _dot_nt(wq_ref[...], cqn)
    k_pad = _dot(ckvn, wk_ref[...])
    for h in range(MLA_HEADS):
        sl = slice(h * LANES, (h + 1) * LANES)
        q_out[h] = rope_t(q_t[sl], scale).astype(BF16)
        k_out[h] = (k_pad[:, sl] + kr_roped).astype(BF16)
    v_out[...] = _dot_nt(wv_ref[...], ckvn).astype(BF16)


def _mla_proj(x2d, pos, g, w_in_p, wkr_t, qg, wq_t, kvg, wk_p, wv_t, freq, *, batch, seq, tm):
    t = x2d.shape[0]
    d = x2d.shape[1]
    nt = seq // tm
    const = lambda shape: pl.BlockSpec(shape, lambda i: (0,) * len(shape))
    scale = (MLA_NOPE + MLA_ROPE) ** -0.5 * 1.4426950408889634
    return pl.pallas_call(
        functools.partial(_mla_proj_kernel, scale=scale),
        grid=(t // tm,),
        in_specs=[
            pl.BlockSpec((tm, d), lambda i: (i, 0)),
            pl.BlockSpec((1, tm), lambda i: (0, i)),
            const((1, d)),
            const(w_in_p.shape),
            const(wkr_t.shape),
            const((1, MLA_Q_LORA)),
            const(wq_t.shape),
            const((1, MLA_KV_LORA)),
            const(wk_p.shape),
            const(wv_t.shape),
            const(freq.shape),
        ],
        out_specs=[
            pl.BlockSpec((None, MLA_HEADS, LANES, tm), lambda i: (i // nt, 0, 0, i % nt)),
            pl.BlockSpec((None, MLA_HEADS, tm, LANES), lambda i: (i // nt, 0, i % nt, 0)),
            pl.BlockSpec((None, MLA_HEADS * MLA_V, tm), lambda i: (i // nt, 0, i % nt)),
        ],
        out_shape=[
            jax.ShapeDtypeStruct((batch, MLA_HEADS, LANES, seq), BF16),
            jax.ShapeDtypeStruct((batch, MLA_HEADS, seq, LANES), BF16),
            jax.ShapeDtypeStruct((batch, MLA_HEADS * MLA_V, seq), BF16),
        ],
        compiler_params=pltpu.CompilerParams(
            dimension_semantics=("parallel",), vmem_limit_bytes=VMEM_LIMIT),
        name="mla_proj",
    )(x2d, pos, g, w_in_p, wkr_t, qg, wq_t, kvg, wk_p, wv_t, freq)


def _attn_kernel(q_ref, k_ref, v_ref, o_ref, acc_ref, m_ref, vx_ref, sa_ref, sb_ref):
    hp, _, tq = q_ref.shape
    seq = k_ref.shape[1]
    qi = pl.program_id(2)

    @pl.when(qi == 0)
    def _():
        for h in range(hp):
            vx_ref[h, :MLA_V] = v_ref[h * MLA_V:(h + 1) * MLA_V]
            vx_ref[h, MLA_V:] = jnp.ones((vx_ref.shape[1] - MLA_V, seq), BF16)

    causal = (lax.broadcasted_iota(jnp.int32, (tq, tq), 0)
              <= lax.broadcasted_iota(jnp.int32, (tq, tq), 1))

    def scores(dst, blk):
        k0 = pl.multiple_of(blk * tq, tq)
        for h in range(hp):
            dst[h] = _dot(k_ref[h, pl.ds(k0, tq), :], q_ref[h])

    def softmax_pv(src, blk, masked):
        k0 = pl.multiple_of(blk * tq, tq)
        for h in range(hp):
            s = src[h]
            if masked:
                s = jnp.where(causal, s, -jnp.inf)
            m = m_ref[h]
            m_new = jnp.maximum(m, jnp.max(s, axis=0, keepdims=True))
            p = jnp.exp2(s - m_new)
            alpha = jnp.exp2(m - m_new)
            m_ref[h] = m_new
            acc_ref[h] = alpha * acc_ref[h] + _dot(vx_ref[h, :, pl.ds(k0, tq)], p.astype(BF16))

    acc_ref[...] = jnp.zeros_like(acc_ref)
    m_ref[...] = jnp.full_like(m_ref, -jnp.inf)

    scores(sa_ref, 0)

    def pair(jp, _):
        scores(sb_ref, 2 * jp + 1)
        softmax_pv(sa_ref, 2 * jp, False)
        scores(sa_ref, 2 * jp + 2)
        softmax_pv(sb_ref, 2 * jp + 1, False)
        return 0

    lax.fori_loop(0, qi // 2, pair, 0)

    @pl.when(qi % 2 == 1)
    def _():
        scores(sb_ref, qi)
        softmax_pv(sa_ref, qi - 1, False)
        softmax_pv(sb_ref, qi, True)

    @pl.when(qi % 2 == 0)
    def _():
        softmax_pv(sa_ref, qi, True)

    out = jnp.concatenate([acc_ref[h, :MLA_V] / acc_ref[h, MLA_V:MLA_V + 1]
                           for h in range(hp)], axis=0)
    o_ref[...] = out.T.astype(BF16)


def _mla_attn(q_t, k, v_t, *, tq, hp):
    batch, heads, seq, _ = k.shape
    return pl.pallas_call(
        _attn_kernel,
        grid=(batch, heads // hp, seq // tq),
        in_specs=[
            pl.BlockSpec((None, hp, LANES, tq), lambda b, g, i: (b, g, 0, i)),
            pl.BlockSpec((None, hp, seq, LANES), lambda b, g, i: (b, g, 0, 0)),
            pl.BlockSpec((None, hp * MLA_V, seq), lambda b, g, i: (b, g, 0)),
        ],
        out_specs=pl.BlockSpec((None, tq, hp * MLA_V), lambda b, g, i: (b, i, g)),
        out_shape=jax.ShapeDtypeStruct((batch, seq, heads * MLA_V), BF16),
        scratch_shapes=[pltpu.VMEM((hp, MLA_V + 16, tq), F32),
                        pltpu.VMEM((hp, 1, tq), F32), pltpu.VMEM((hp, MLA_V + 16, seq), BF16),
                        pltpu.VMEM((hp, tq, tq), F32), pltpu.VMEM((hp, tq, tq), F32)],
        compiler_params=pltpu.CompilerParams(
            dimension_semantics=("parallel", "parallel", "arbitrary"),
            vmem_limit_bytes=VMEM_LIMIT),
        name="mla_attn",
    )(q_t, k, v_t)


def _attn_ffn_kernel(x_ref, o_ref, wo_ref, g_ref, wg_ref, wu_ref, wd_ref, out_ref, hn_ref):
    j = pl.program_id(1)

    @pl.when(j == 0)
    def _():
        h1 = x_ref[...] + _dot(o_ref[...], wo_ref[...])
        out_ref[...] = h1
        hn_ref[...] = _rms(h1, g_ref[...]).astype(BF16)

    hn = hn_ref[...]
    a = _silu(_dot(hn, wg_ref[...])) * _dot(hn, wu_ref[...])
    out_ref[...] += _dot(a.astype(BF16), wd_ref[...])


def _attn_ffn(x2d, o, w_o, g, w_gate, w_up, w_down, *, tm, tf):
    t, d = x2d.shape
    f = w_gate.shape[1]
    return pl.pallas_call(
        _attn_ffn_kernel,
        grid=(t // tm, f // tf),
        in_specs=[
            pl.BlockSpec((tm, d), lambda i, j: (i, 0)),
            pl.BlockSpec((tm, o.shape[1]), lambda i, j: (i, 0)),
            pl.BlockSpec(w_o.shape, lambda i, j: (0, 0)),
            pl.BlockSpec((1, d), lambda i, j: (0, 0)),
            pl.BlockSpec((d, tf), lambda i, j: (0, j)),
            pl.BlockSpec((d, tf), lambda i, j: (0, j)),
            pl.BlockSpec((tf, d), lambda i, j: (j, 0)),
        ],
        out_specs=pl.BlockSpec((tm, d), lambda i, j: (i, 0)),
        out_shape=jax.ShapeDtypeStruct((t, d), F32),
        scratch_shapes=[pltpu.VMEM((tm, d), BF16)],
        compiler_params=pltpu.CompilerParams(
            dimension_semantics=("parallel", "arbitrary"), vmem_limit_bytes=VMEM_LIMIT),
        name="attn_ffn",
    )(x2d, o, w_o, g, w_gate, w_up, w_down)


def _hgrn_kernel(h_ref, g_ref, win_ref, lbraw_ref, og_ref, wo_ref, fg_ref, router_ref,
                 h_out, hn_out, meta_out, meta_t_out, cnt_out, st_ref, o_scr, cnt_ref, *, layer):
    tt = h_ref.shape[0]
    width = HG_HEADS * HG_DK
    nchunk = tt // HG_CHUNK

    @pl.when(pl.program_id(1) == 0)
    def _():
        st_ref[...] = jnp.zeros_like(st_ref)

    lbraw = lbraw_ref[...]
    e = jnp.exp(lbraw - jnp.max(lbraw, axis=0, keepdims=True))
    sm = e / jnp.sum(e, axis=0, keepdims=True)
    lb = jnp.sum(sm[1:layer + 1], axis=0, keepdims=True)

    h_in = h_ref[...]
    hn = _rms(h_in, g_ref[...]).astype(BF16)
    proj = _dot(hn, win_ref[...])
    q_all = proj[:, :width] * (HG_DK ** -0.5)
    f_all = lb + (1.0 - lb) * (1.0 / (1.0 + jnp.exp(-proj[:, width:2 * width])))
    logf = jnp.log(f_all)
    k_all = 1.0 - f_all

    r = lax.broadcasted_iota(jnp.int32, (tt, tt), 0)
    c = lax.broadcasted_iota(jnp.int32, (tt, tt), 1)
    shift = HG_CHUNK.bit_length() - 1
    tril = (c <= r) & ((r >> shift) == (c >> shift))
    in_chunk = lax.broadcasted_iota(jnp.int32, (tt, 1), 0) & (HG_CHUNK - 1)
    b_all = logf
    for step in (1 << i for i in range(shift)):
        b_all = b_all + jnp.where(in_chunk >= step, pltpu.roll(b_all, step, 0), 0.0)

    for h in range(HG_HEADS):
        sl = slice(h * HG_DK, (h + 1) * HG_DK)
        q = q_all[:, sl]
        k = k_all[:, sl]
        b = b_all[:, sl]
        v = proj[:, 2 * width + h * HG_DK:2 * width + (h + 1) * HG_DK]
        gt = proj[:, 3 * width + h * HG_DK:3 * width + (h + 1) * HG_DK]
        v_b = v.astype(BF16)
        qd = (q * jnp.exp(b)).astype(BF16)
        kd = (k * jnp.exp(-b)).astype(BF16)
        a = jnp.where(tril, _dot_nt(qd, kd), 0.0)
        o = _dot(a.astype(BF16), v_b)

        st = st_ref[h]
        inter = []
        for n in range(nchunk):
            rows = slice(n * HG_CHUNK, (n + 1) * HG_CHUNK)
            b_n = b[rows]
            b_last = b_n[HG_CHUNK - 1:HG_CHUNK]
            inter.append(_dot_nt(qd[rows], st.astype(BF16)))
            kdl = (k[rows] * jnp.exp(b_last - b_n)).astype(BF16)
            st = st * jnp.exp(b_last) + _dot_tn(v_b[rows], kdl)
        st_ref[h] = st
        o = o + jnp.concatenate(inter, axis=0)
        o = _rms(o, og_ref[...]) * _silu(gt)
        o_scr[:, sl] = o.astype(BF16)

    h_new = h_in + _dot(o_scr[...], wo_ref[...])
    h_out[...] = h_new
    hn2 = _rms(h_new, fg_ref[...])
    _store_rows_as_tiles(hn_out, hn2)

    x_hi, x_mid, _ = _split3(hn2)
    router = router_ref[...]
    r_hi = router.astype(BF16)
    r_lo = (router - r_hi.astype(F32)).astype(BF16)
    hi_terms = _dot(x_hi, jnp.concatenate([r_hi, r_lo], axis=1))
    logits = hi_terms[:, :LANES] + _dot(x_mid, r_hi) + hi_terms[:, LANES:]
    lane = lax.broadcasted_iota(jnp.int32, logits.shape, 1)
    neg = -jnp.inf
    logits = jnp.where(lane < N_EXPERTS, logits, neg)
    m1 = jnp.max(logits, axis=-1, keepdims=True)
    i1 = jnp.min(jnp.where(logits == m1, lane, LANES), axis=-1, keepdims=True)
    rest = jnp.where(lane == i1, neg, logits)
    m2 = jnp.max(rest, axis=-1, keepdims=True)
    i2 = jnp.min(jnp.where(rest == m2, lane, LANES), axis=-1, keepdims=True)
    e2 = jnp.exp(m2 - m1)
    w1 = 1.0 / (1.0 + e2)
    w2 = e2 / (1.0 + e2)

    @pl.when((pl.program_id(0) == 0) & (pl.program_id(1) == 0))
    def _():
        cnt_ref[...] = jnp.zeros_like(cnt_ref)

    onehot = jnp.where((lane == i1) | (lane == i2), 1.0, 0.0)
    earlier = jnp.where(c < r, 1.0, 0.0).astype(BF16)
    rank = _dot(earlier, onehot.astype(BF16)) + cnt_ref[...]
    cnt = cnt_ref[...] + jnp.sum(onehot, axis=0, keepdims=True)
    cnt_ref[...] = cnt
    cnt_out[...] = jnp.broadcast_to(cnt, cnt_out.shape)
    rank1 = jnp.sum(jnp.where(lane == i1, rank, 0.0), axis=-1, keepdims=True)
    rank2 = jnp.sum(jnp.where(lane == i2, rank, 0.0), axis=-1, keepdims=True)
    fields = (i1.astype(F32), i2.astype(F32), w1, w2, rank1, rank2)
    meta = jnp.zeros(logits.shape, F32)
    for idx, val in enumerate(fields):
        meta = jnp.where(lane == idx, val, meta)
    meta_out[...] = meta
    meta_t_out[...] = meta.T[:SUBLANES]


def _hgrn(h2d, g, w_in, lb_raw, og, w_o, fg, router_p, *, batch, seq, tt, layer):
    t, d = h2d.shape
    nt = seq // tt
    width = HG_HEADS * HG_DK
    tok = lambda cols: pl.BlockSpec((tt, cols), lambda b, i: (b * nt + i, 0))
    const = lambda shape: pl.BlockSpec(shape, lambda b, i: (0,) * len(shape))
    return pl.pallas_call(
        functools.partial(_hgrn_kernel, layer=layer),
        grid=(batch, nt),
        in_specs=[tok(d), const((1, d)), const(w_in.shape), const(lb_raw.shape),
                  const((1, HG_DK)), const(w_o.shape), const((1, d)), const(router_p.shape)],
        out_specs=[tok(d),
                   pl.BlockSpec((tt * (d // LANES), LANES), lambda b, i: (b * nt + i, 0)),
                   tok(LANES), pl.BlockSpec((SUBLANES, tt), lambda b, i: (0, b * nt + i)),
                   const((SUBLANES, LANES))],
        out_shape=[jax.ShapeDtypeStruct((t, d), F32),
                   jax.ShapeDtypeStruct((t * (d // LANES), LANES), F32),
                   jax.ShapeDtypeStruct((t, LANES), F32),
                   jax.ShapeDtypeStruct((SUBLANES, t), F32),
                   jax.ShapeDtypeStruct((SUBLANES, LANES), F32)],
        scratch_shapes=[pltpu.VMEM((HG_HEADS, HG_DK, HG_DK), F32),
                        pltpu.VMEM((tt, width), BF16),
                        pltpu.VMEM((1, LANES), F32)],
        compiler_params=pltpu.CompilerParams(
            dimension_semantics=("arbitrary", "arbitrary"), vmem_limit_bytes=VMEM_LIMIT),
        name="hgrn",
    )(h2d, g, w_in, lb_raw, og, w_o, fg, router_p)


def _expert_kernel(d1_ref, d2_ref, te_ref, tv_ref, x_hbm, wg_ref, wu_ref, wd_ref, y_ref,
                   xbuf, xb_ref, acc_ref, src_ref, sems):
    del te_ref
    r = pl.program_id(0)
    j = pl.program_id(1)
    n_tiles = pl.num_programs(0)
    nj = pl.num_programs(1)
    tm = xb_ref.shape[0]
    nc = xb_ref.shape[1] // LANES
    valid = tv_ref[r] != 0
    has_next = r + 1 < n_tiles

    def row_copy(tile, row):
        slot = tile % 2
        src = pl.multiple_of(src_ref[tile * tm + row] * nc, nc)
        return pltpu.make_async_copy(x_hbm.at[pl.ds(src, nc)],
                                     xbuf.at[slot, pl.ds(row * nc, nc)], sems.at[slot])

    def swiglu(x):
        a = _silu(_dot(x, wg_ref[...])) * _dot(x, wu_ref[...])
        return _dot(a.astype(BF16), wd_ref[...])

    def swiglu_step(first):
        if not first:
            acc_ref[...] += swiglu(xb_ref[...])
            return
        half = tm // 2
        for hlf in range(2):
            rows = slice(hlf * half, (hlf + 1) * half)
            x = _load_rows_from_tiles(xbuf.at[r % 2, pl.ds(hlf * half * nc, half * nc)], nc)
            x = x.astype(BF16)
            xb_ref[rows] = x
            acc_ref[rows] = swiglu(x)

    @pl.when((r == 0) & (j == 0))
    def _():
        def clear(i, c):
            src_ref[i] = 0
            return c

        def place(t, c):
            src_ref[d1_ref[t]] = t
            src_ref[d2_ref[t]] = t
            return c

        lax.fori_loop(0, src_ref.shape[0], clear, 0, unroll=16)
        lax.fori_loop(0, d1_ref.shape[0], place, 0, unroll=8)
        lax.fori_loop(0, tm, lambda i, c: (row_copy(r, i).start(), c)[1], 0)

    @pl.when((j == 0) & ((r == 0) | (tv_ref[jnp.maximum(r - 1, 0)] != 0)))
    def _():
        pltpu.make_async_copy(x_hbm.at[pl.ds(0, tm * nc)], xbuf.at[r % 2],
                              sems.at[r % 2]).wait()

    part = tm // (nj - 1)

    @pl.when(valid & (j == 0))
    def _():
        swiglu_step(True)

    @pl.when(valid & has_next & (j > 0))
    def _():
        for i in range(part):
            row_copy(r + 1, (j - 1) * part + i).start(priority=i % DMA_THREADS)
        swiglu_step(False)

    @pl.when(valid & jnp.logical_not(has_next) & (j > 0))
    def _():
        swiglu_step(False)

    @pl.when(valid & (j == nj - 1))
    def _():
        _store_rows_as_tiles(y_ref, acc_ref[...])

    @pl.when(jnp.logical_not(valid) & (j == nj - 1))
    def _():
        y_ref[...] = jnp.zeros_like(y_ref)


def _experts(dest1, dest2, tile_expert, tile_valid, x, w_gate, w_up, w_down, *, tm, tf):
    n_rows = tile_expert.shape[0] * tm
    d = w_gate.shape[1]
    nc = d // LANES
    f = w_gate.shape[2]
    nj = f // tf
    col = lambda r, j, tv: jnp.where(tv[r] != 0, j, nj - 1)
    return pl.pallas_call(
        _expert_kernel,
        grid_spec=pltpu.PrefetchScalarGridSpec(
            num_scalar_prefetch=4,
            grid=(n_rows // tm, nj),
            in_specs=[
                pl.BlockSpec(memory_space=pl.ANY),
                pl.BlockSpec((None, d, tf), lambda r, j, a, b, te, tv: (te[r], 0, col(r, j, tv))),
                pl.BlockSpec((None, d, tf), lambda r, j, a, b, te, tv: (te[r], 0, col(r, j, tv))),
                pl.BlockSpec((None, tf, d), lambda r, j, a, b, te, tv: (te[r], col(r, j, tv), 0)),
            ],
            out_specs=pl.BlockSpec((tm * nc, LANES), lambda r, j, a, b, te, tv: (r, 0)),
            scratch_shapes=[pltpu.VMEM((2, tm * nc, LANES), F32), pltpu.VMEM((tm, d), BF16),
                            pltpu.VMEM((tm, d), F32), pltpu.SMEM((n_rows,), jnp.int32),
                            pltpu.SemaphoreType.DMA((2,))],
        ),
        out_shape=jax.ShapeDtypeStruct((n_rows * nc, LANES), F32),
        compiler_params=pltpu.CompilerParams(
            dimension_semantics=("arbitrary", "arbitrary"), vmem_limit_bytes=VMEM_LIMIT),
        name="moe_experts",
    )(dest1, dest2, tile_expert, tile_valid, x, w_gate, w_up, w_down)


def _combine_kernel(d1_ref, d2_ref, y_hbm, h_ref, meta_ref, fg_ref, out_ref, a_ref, b_ref, sem):
    tc = h_ref.shape[0]
    base = pl.program_id(0) * tc

    nc = h_ref.shape[1] // LANES

    def issue(r, _):
        for thread, (d_ref, buf) in enumerate(((d1_ref, a_ref), (d2_ref, b_ref))):
            src = pl.multiple_of(d_ref[base + r] * nc, nc)
            pltpu.make_async_copy(y_hbm.at[pl.ds(src, nc)], buf.at[pl.ds(r * nc, nc)],
                                  sem).start(priority=thread % DMA_THREADS)
        return 0

    lax.fori_loop(0, tc, issue, 0, unroll=16)
    pltpu.make_async_copy(y_hbm.at[pl.ds(0, tc * nc)], a_ref, sem).wait()
    pltpu.make_async_copy(y_hbm.at[pl.ds(0, tc * nc)], b_ref, sem).wait()
    meta = meta_ref[...]
    w1 = meta[:, 2:3]
    w2 = meta[:, 3:4]
    moe = w1 * _load_rows_from_tiles(a_ref, nc) + w2 * _load_rows_from_tiles(b_ref, nc)
    out_ref[...] = _rms(h_ref[...] + moe, fg_ref[...])


def _combine(dest1, dest2, y, h2d, meta, fg, *, tc):
    t, d = h2d.shape
    return pl.pallas_call(
        _combine_kernel,
        grid_spec=pltpu.PrefetchScalarGridSpec(
            num_scalar_prefetch=2,
            grid=(t // tc,),
            in_specs=[
                pl.BlockSpec(memory_space=pl.ANY),
                pl.BlockSpec((tc, d), lambda i, d1, d2: (i, 0)),
                pl.BlockSpec((tc, LANES), lambda i, d1, d2: (i, 0)),
                pl.BlockSpec((1, d), lambda i, d1, d2: (0, 0)),
            ],
            out_specs=pl.BlockSpec((tc, d), lambda i, d1, d2: (i, 0)),
            scratch_shapes=[pltpu.VMEM((tc * d // LANES, LANES), F32),
                            pltpu.VMEM((tc * d // LANES, LANES), F32),
                            pltpu.SemaphoreType.DMA(())],
        ),
        out_shape=jax.ShapeDtypeStruct((t, d), F32),
        compiler_params=pltpu.CompilerParams(
            dimension_semantics=("arbitrary",), vmem_limit_bytes=VMEM_LIMIT),
        name="moe_combine",
    )(dest1, dest2, y, h2d, meta, fg)


def _routing_plan(meta_t, counts, *, tm, n_tiles):
    e1 = meta_t[0].astype(jnp.int32)
    e2 = meta_t[1].astype(jnp.int32)
    cnt = counts[0, :N_EXPERTS].astype(jnp.int32)
    padded = (cnt + tm - 1) // tm * tm
    ends = jnp.cumsum(padded)
    offs = ends - padded
    dest1 = offs[e1] + meta_t[4].astype(jnp.int32)
    dest2 = offs[e2] + meta_t[5].astype(jnp.int32)
    start = jnp.arange(n_tiles, dtype=jnp.int32) * tm
    tile_valid = (start < ends[-1]).astype(jnp.int32)
    last = jnp.sum((ends < ends[-1]).astype(jnp.int32))
    passed = jnp.sum((ends[None, :] <= start[:, None]).astype(jnp.int32), axis=1)
    return dest1, dest2, jnp.minimum(passed, last), tile_valid


def _pad_heads(w, head_dim):
    k = w.shape[0]
    w = w.reshape(k, MLA_HEADS, head_dim)
    w = jnp.pad(w, ((0, 0), (0, 0), (0, LANES - head_dim)))
    return w.reshape(k, MLA_HEADS * LANES)


def _pick(n, cap):
    t = min(n, cap)
    while n % t:
        t //= 2
    return t


def kernel(x, positions, mix_norm, ffn_norm, final_norm, mla_w_in, mla_q_norm, mla_w_q_up,
           mla_kv_norm, mla_w_kv_up, mla_w_o, hgrn_w_in, hgrn_lower_bounds, hgrn_out_norm,
           hgrn_w_o, ffn_w_gate, ffn_w_up, ffn_w_down, moe_router, moe_w_gate, moe_w_up,
           moe_w_down):
    batch, seq, d = x.shape
    t = batch * seq
    x2d = x.reshape(t, d)
    row = lambda v: v.reshape(1, -1)

    w_in = mla_w_in[0]
    n_lat = MLA_Q_LORA + MLA_KV_LORA
    w_in_p = w_in[:, :n_lat].astype(BF16)
    wkr_t = jnp.pad(w_in[:, n_lat:].T,
                    ((MLA_NOPE, LANES - MLA_NOPE - MLA_ROPE), (0, 0))).astype(BF16)
    wq_t = _pad_heads(mla_w_q_up[0], MLA_NOPE + MLA_ROPE).T.astype(BF16)
    w_kv = mla_w_kv_up[0].reshape(MLA_KV_LORA, MLA_HEADS, MLA_NOPE + MLA_V)
    wk_p = _pad_heads(w_kv[:, :, :MLA_NOPE].reshape(MLA_KV_LORA, -1), MLA_NOPE).astype(BF16)
    wv_t = w_kv[:, :, MLA_NOPE:].reshape(MLA_KV_LORA, -1).T.astype(BF16)
    inv_freq = ROPE_THETA ** (-jnp.arange(0, MLA_ROPE, 2, dtype=F32) / MLA_ROPE)
    freq = inv_freq.reshape(MLA_ROPE // 2, 1)
    pos = positions.astype(F32).reshape(1, t)

    q_t, k, v_t = _mla_proj(x2d, pos, row(mix_norm[0]), w_in_p, wkr_t, row(mla_q_norm[0]), wq_t,
                            row(mla_kv_norm[0]), wk_p, wv_t, freq,
                            batch=batch, seq=seq, tm=_pick(seq, 512))
    o = _mla_attn(q_t, k, v_t, tq=_pick(seq, 512), hp=8).reshape(t, MLA_HEADS * MLA_V)
    h = _attn_ffn(x2d, o, mla_w_o[0].astype(BF16), row(ffn_norm[0]),
                  ffn_w_gate[0].astype(BF16), ffn_w_up[0].astype(BF16),
                  ffn_w_down[0].astype(BF16), tm=_pick(t, 512), tf=1792)

    router_p = jnp.pad(moe_router[0], ((0, 0), (0, LANES - N_EXPERTS)))
    h, hn, meta, meta_t, counts = _hgrn(h, row(mix_norm[1]), hgrn_w_in[0].astype(BF16),
                                hgrn_lower_bounds, row(hgrn_out_norm[0]),
                                hgrn_w_o[0].astype(BF16), row(ffn_norm[1]), router_p,
                                batch=batch, seq=seq, tt=_pick(seq, 256), layer=1)
    tm = _pick(t, 512)
    n_tiles = 2 * t // tm + N_EXPERTS
    dest1, dest2, tile_expert, tile_valid = _routing_plan(
        meta_t, counts, tm=tm, n_tiles=n_tiles)
    y = _experts(dest1, dest2, tile_expert, tile_valid, hn, moe_w_gate[0].astype(BF16),
                 moe_w_up[0].astype(BF16), moe_w_down[0].astype(BF16), tm=tm, tf=1792)
    out = _combine(dest1, dest2, y, h, meta, row(final_norm), tc=_pick(t, 1024))
    return out.reshape(batch, seq, d)
```
